```python
import math
import jax, jax.numpy as jnp
from jax import lax
import numpy as np

D_MODEL = 4096
BATCH = 4
SEQ = 4096
DEPTH = 2

CHUNK = 64
Q_BLOCK = 128
ROPE_THETA = 500000.0
ROPE_FRACTION = 4
NORM_EPS = 1e-6

A_HEADS = 8
A_DK = 128
A_DV = 128
A_W = A_HEADS * A_DV

B_HEADS = 16
B_HD = 64
B_W = B_HEADS * B_HD
B_LORA_DECAY = 64
B_LORA_A = 64
B_LORA_G = 160
B_GN_EPS = 64e-5

C_HEADS = 16
C_KV_HEADS = 4
C_HD = 128
C_W = C_HEADS * C_HD
C_KV_W = C_KV_HEADS * C_HD
IDX_HEADS = 16
IDX_HD = 64
TOPK_MAX = 256

D_FF = 11008
CONV_W = 3

A_SIZES = (A_HEADS * A_DK, A_HEADS * A_DK, A_W, A_W)
B_SIZES = (B_W, B_W, B_W, B_LORA_DECAY, B_LORA_A, B_LORA_G)
C_SIZES = (C_W, C_KV_W, C_KV_W, IDX_HEADS * IDX_HD, IDX_HD, IDX_HEADS)
GATE_SIZES = (D_MODEL, D_MODEL, D_MODEL)
A_COLS = sum(A_SIZES)
B_COLS = sum(B_SIZES)
C_COLS = sum(C_SIZES)
IN_COLS = A_COLS + B_COLS + C_COLS + sum(GATE_SIZES)

kernel_name = 'hybrid_hgrn2_rwkv7_dsa_gated_trunk'


def split_cols(t, sizes):
    return jnp.split(t, np.cumsum(sizes)[:-1].tolist(), axis=-1)


def rms_norm(t, gain, eps=NORM_EPS):
    tf = t.astype(jnp.float32)
    y = tf * lax.rsqrt(jnp.mean(tf * tf, axis=-1, keepdims=True) + eps)
    return (y * gain.astype(jnp.float32)).astype(t.dtype)


def causal_shift(t, n):
    pad = [(0, 0)] * t.ndim
    pad[1] = (n, 0)
    return jnp.pad(t, pad)[:, :t.shape[1]]


def rope_tables(positions, head_dim):
    rot = head_dim // ROPE_FRACTION
    inv_freq = ROPE_THETA ** (-jnp.arange(0, rot, 2, dtype=jnp.float32) / rot)
    ang = positions.astype(jnp.float32)[..., None] * inv_freq
    return jnp.cos(ang), jnp.sin(ang)


def apply_partial_rope(t, cos, sin):
    half = cos.shape[-1]
    c = cos[:, :, None, :]
    s = sin[:, :, None, :]
    t1 = t[..., :half].astype(jnp.float32)
    t2 = t[..., half:2 * half].astype(jnp.float32)
    rot = jnp.concatenate([t1 * c - t2 * s, t2 * c + t1 * s], axis=-1).astype(t.dtype)
    return jnp.concatenate([rot, t[..., 2 * half:]], axis=-1)


def hgrn2_mixer(q, f_logit, i, g, lb, out_gain):
    f32 = jnp.float32
    bsz, seq, _ = q.shape
    n_chunks = seq // CHUNK
    zf = f_logit.astype(f32)
    lbf = lb.astype(f32)
    log_f = jnp.log(lbf + (1.0 - lbf) * jax.nn.sigmoid(zf))
    k_in = (1.0 - lbf) * jax.nn.sigmoid(-zf)
    qf = jax.nn.silu(q.astype(f32))

    def heads(t, d):
        return t.reshape(bsz, n_chunks, CHUNK, A_HEADS, d).transpose(1, 0, 3, 2, 4)

    causal = jnp.tril(jnp.ones((CHUNK, CHUNK), dtype=bool))

    def step(state, inp):
        qc, kc, ic, lfc = inp
        b = jnp.cumsum(lfc, axis=2)
        o_inter = jnp.einsum('bhtk,bhkv->bhtv', qc * jnp.exp(b), state)
        diff = b[:, :, :, None, :] - b[:, :, None, :, :]
        dec = jnp.exp(jnp.where(causal[:, :, None], diff, -jnp.inf))
        att = jnp.einsum('bhtk,bhsk,bhtsk->bhts', qc, kc, dec)
        o_intra = jnp.einsum('bhts,bhsv->bhtv', att, ic)
        b_last = b[:, :, -1:, :]
        state = jnp.exp(b_last[:, :, 0, :])[..., None] * state + jnp.einsum(
            'bhsk,bhsv->bhkv', kc * jnp.exp(b_last - b), ic)
        return state, o_inter + o_intra

    s0 = jnp.zeros((bsz, A_HEADS, A_DK, A_DV), f32)
    _, o = lax.scan(step, s0, (heads(qf, A_DK), heads(k_in, A_DK),
                               heads(i.astype(f32), A_DV), heads(log_f, A_DK)))
    o = o.transpose(1, 0, 3, 2, 4).reshape(bsz, seq, A_HEADS, A_DV)
    o = rms_norm(o, out_gain) * jax.nn.silu(g.astype(f32).reshape(bsz, seq, A_HEADS, A_DV))
    return o.reshape(bsz, seq, A_W).astype(q.dtype)


def rwkv7_mixer(p, mu, w0, w2, a0, a2, g2, k_k, k_a, r_k, ln_w, ln_b):
    f32 = jnp.float32
    bsz, seq, _ = p.shape
    p = p + (causal_shift(p, 1) - p) * mu
    r, k, v, wl, al, gl = split_cols(p, B_SIZES)
    w_log = -jax.nn.softplus(-(w0 + jnp.tanh(wl) @ w2).astype(f32)) - 0.5
    decay = jnp.exp(-jnp.exp(w_log))
    a = jax.nn.sigmoid((a0 + al @ a2).astype(f32))
    g = (jax.nn.sigmoid(gl) @ g2).astype(f32)

    def hv(t):
        return t.astype(f32).reshape(bsz, seq, B_HEADS, B_HD)

    r, k, v, a, decay = hv(r), hv(k), hv(v), hv(a), hv(decay)
    hn = lambda t: t.astype(f32).reshape(B_HEADS, B_HD)
    kk = k * hn(k_k)
    kk = kk / jnp.maximum(jnp.linalg.norm(kk, axis=-1, keepdims=True), 1e-12)
    k = k * (1.0 + (a - 1.0) * hn(k_a))

    def step(state, inp):
        r_t, w_t, k_t, v_t, kk_t, a_t = inp
        sa = jnp.einsum('bhvk,bhk->bhv', state, -kk_t)
        state = (state * w_t[:, :, None, :] + sa[..., None] * (kk_t * a_t)[:, :, None, :]
                 + v_t[..., None] * k_t[:, :, None, :])
        return state, jnp.einsum('bhvk,bhk->bhv', state, r_t)

    tm = lambda t: jnp.swapaxes(t, 0, 1)
    s0 = jnp.zeros((bsz, B_HEADS, B_HD, B_HD), f32)
    _, o = lax.scan(step, s0, (tm(r), tm(decay), tm(k), tm(v), tm(kk), tm(a)))
    o = tm(o)
    mean = jnp.mean(o, axis=-1, keepdims=True)
    var = jnp.mean(jnp.square(o - mean), axis=-1, keepdims=True)
    o = (o - mean) * lax.rsqrt(var + B_GN_EPS) * hn(ln_w) + hn(ln_b)
    o = o + jnp.sum(r * k * hn(r_k), axis=-1, keepdims=True) * v
    return (o.reshape(bsz, seq, B_W) * g).astype(p.dtype)


def sparse_attention_mixer(q, k, v, q_idx, k_idx, w_idx, q_gain, k_gain, cos_a, sin_a, cos_i, sin_i):
    f32 = jnp.float32
    bsz, seq, _ = q.shape
    q = apply_partial_rope(rms_norm(q.reshape(bsz, seq, C_HEADS, C_HD), q_gain), cos_a, sin_a)
    k = apply_partial_rope(rms_norm(k.reshape(bsz, seq, C_KV_HEADS, C_HD), k_gain), cos_a, sin_a)
    v = v.reshape(bsz, seq, C_KV_HEADS, C_HD)
    q_idx = apply_partial_rope(q_idx.reshape(bsz, seq, IDX_HEADS, IDX_HD), cos_i, sin_i)
    k_idx = apply_partial_rope(k_idx[:, :, None, :], cos_i, sin_i)[:, :, 0, :]
    w_idx = w_idx * (IDX_HEADS ** -0.5 * IDX_HD ** -0.5)
    topk = min(TOPK_MAX, seq // 4)
    key_chunk = jnp.arange(seq) // CHUNK
    gather = jax.vmap(lambda t, ix: t[ix])

    def block(start):
        qb = lax.dynamic_slice_in_dim(q, start, Q_BLOCK, axis=1)
        qib = lax.dynamic_slice_in_dim(q_idx, start, Q_BLOCK, axis=1)
        wb = lax.dynamic_slice_in_dim(w_idx, start, Q_BLOCK, axis=1)
        q_chunk = (start + jnp.arange(Q_BLOCK)) // CHUNK
        rel = jax.nn.relu(jnp.einsum('bqhd,bsd->bqhs', qib, k_idx).astype(f32))
        score = jnp.einsum('bqh,bqhs->bqs', wb.astype(f32), rel)
        allowed = key_chunk[None, :] <= q_chunk[:, None]
        score = jnp.where(allowed[None], score, -jnp.inf)
        vals, idx = lax.top_k(score, topk)
        valid = jnp.isfinite(vals)
        kb = gather(k, idx)
        vb = gather(v, idx)
        qg = qb.reshape(bsz, Q_BLOCK, C_KV_HEADS, C_HEADS // C_KV_HEADS, C_HD)
        logits = jnp.einsum('bqgrd,bqkgd->bqgrk', qg, kb).astype(f32) * (C_HD ** -0.5)
        logits = jnp.where(valid[:, :, None, None, :], logits, -jnp.inf)
        prob = jax.nn.softmax(logits, axis=-1).astype(vb.dtype)
        o = jnp.einsum('bqgrk,bqkgd->bqgrd', prob, vb)
        return o.reshape(bsz, Q_BLOCK, C_W)

    starts = jnp.arange(seq // Q_BLOCK) * Q_BLOCK
    o = lax.map(block, starts)
    return o.transpose(1, 0, 2, 3).reshape(bsz, seq, C_W)


def conv_ffn(h, w_up, conv_w, conv_b, w_down):
    u = h @ w_up
    u = sum(conv_w[j] * causal_shift(u, CONV_W - 1 - j) for j in range(CONV_W)) + conv_b
    gate, up = jnp.split(u, 2, axis=-1)
    return (jax.nn.silu(gate) * up) @ w_down


def setup_inputs(seed: int = 0) -> dict:
    key = jax.random.key(seed)
    ks = jax.random.split(key, 32)
    f32 = jnp.float32
    L = DEPTH

    def nrm(k, shape, scale):
        return jax.random.normal(k, shape, f32) * scale

    x = nrm(ks[0], (BATCH, SEQ, D_MODEL), 1.0)
    offsets = jax.random.randint(ks[1], (BATCH, 1), 0, 1024, dtype=jnp.int32)
    positions = (offsets + jnp.arange(SEQ, dtype=jnp.int32)[None, :]).astype(jnp.int32)
    return {
        'x': x,
        'positions': positions,
        'norm_mix': 1.0 + nrm(ks[2], (L, D_MODEL), 0.02),
        'w_in': nrm(ks[3], (L, D_MODEL, IN_COLS), D_MODEL ** -0.5),
        'hgrn_lb_logits': nrm(ks[4], (L, A_HEADS * A_DK), 0.5),
        'hgrn_out_norm': 1.0 + nrm(ks[5], (L, A_DV), 0.02),
        'rwkv_mu': jax.random.uniform(ks[6], (L, B_COLS), f32),
        'rwkv_w0': -1.0 + nrm(ks[7], (L, B_W), 0.5),
        'rwkv_w2': nrm(ks[8], (L, B_LORA_DECAY, B_W), B_LORA_DECAY ** -0.5),
        'rwkv_a0': nrm(ks[9], (L, B_W), 0.1),
        'rwkv_a2': nrm(ks[10], (L, B_LORA_A, B_W), B_LORA_A ** -0.5),
        'rwkv_g2': nrm(ks[11], (L, B_LORA_G, B_W), B_LORA_G ** -0.5),
        'rwkv_k_k': 0.85 + nrm(ks[12], (L, B_W), 0.02),
        'rwkv_k_a': 1.0 + nrm(ks[13], (L, B_W), 0.02),
        'rwkv_r_k': nrm(ks[14], (L, B_W), 0.1),
        'rwkv_ln_w': 1.0 + nrm(ks[15], (L, B_W), 0.02),
        'rwkv_ln_b': nrm(ks[16], (L, B_W), 0.02),
        'q_norm': 1.0 + nrm(ks[17], (L, C_HD), 0.02),
        'k_norm': 1.0 + nrm(ks[18], (L, C_HD), 0.02),
        'w_branch_a': nrm(ks[19], (L, A_W, D_MODEL), A_W ** -0.5),
        'w_branch_b': nrm(ks[20], (L, B_W, D_MODEL), B_W ** -0.5),
        'w_branch_c': nrm(ks[21], (L, C_W, D_MODEL), C_W ** -0.5),
        'w_out': nrm(ks[22], (L, D_MODEL, D_MODEL), D_MODEL ** -0.5),
        'norm_ffn': 1.0 + nrm(ks[23], (L, D_MODEL), 0.02),
        'w_up': nrm(ks[24], (L, D_MODEL, 2 * D_FF), D_MODEL ** -0.5),
        'conv_w': nrm(ks[25], (L, CONV_W, 2 * D_FF), CONV_W ** -0.5),
        'conv_b': nrm(ks[26], (L, 2 * D_FF), 0.02),
        'w_down': nrm(ks[27], (L, D_FF, D_MODEL), D_FF ** -0.5),
    }


def reference(x, positions, norm_mix, w_in, hgrn_lb_logits, hgrn_out_norm, rwkv_mu, rwkv_w0, rwkv_w2,
              rwkv_a0, rwkv_a2, rwkv_g2, rwkv_k_k, rwkv_k_a, rwkv_r_k, rwkv_ln_w, rwkv_ln_b,
              q_norm, k_norm, w_branch_a, w_branch_b, w_branch_c, w_out, norm_ffn, w_up, conv_w,
              conv_b, w_down):
    cos_a, sin_a = rope_tables(positions, C_HD)
    cos_i, sin_i = rope_tables(positions, IDX_HD)
    lb_all = jnp.cumsum(jax.nn.softmax(hgrn_lb_logits.astype(jnp.float32), axis=0), axis=0)
    lb_all = lb_all - lb_all[0:1]

    for l in range(DEPTH):
        h = rms_norm(x, norm_mix[l])
        p = h @ w_in[l]
        p_a, p_b, p_c, p_g = split_cols(p, (A_COLS, B_COLS, C_COLS, sum(GATE_SIZES)))

        qa, fa, ia, ga = split_cols(p_a, A_SIZES)
        o_a = hgrn2_mixer(qa, fa, ia, ga, lb_all[l], hgrn_out_norm[l])

        o_b = rwkv7_mixer(p_b, rwkv_mu[l], rwkv_w0[l], rwkv_w2[l], rwkv_a0[l], rwkv_a2[l], rwkv_g2[l],
                          rwkv_k_k[l], rwkv_k_a[l], rwkv_r_k[l], rwkv_ln_w[l], rwkv_ln_b[l])

        qc, kc, vc, qi, ki, wi = split_cols(p_c, C_SIZES)
        o_c = sparse_attention_mixer(qc, kc, vc, qi, ki, wi, q_norm[l], k_norm[l],
                                     cos_a, sin_a, cos_i, sin_i)

        gates = jax.nn.sigmoid(p_g.astype(jnp.float32)).astype(x.dtype)
        g_a, g_b, g_c = split_cols(gates, GATE_SIZES)
        merged = (g_a * (o_a @ w_branch_a[l]) + g_b * (o_b @ w_branch_b[l])
                  + g_c * (o_c @ w_branch_c[l]))
        x = x + merged @ w_out[l]

        x = x + conv_ffn(rms_norm(x, norm_ffn[l]), w_up[l], conv_w[l], conv_b[l], w_down[l])
    return x
```

```python
import functools

import jax
import jax.numpy as jnp
import numpy as np
from jax import lax
from jax.experimental import pallas as pl
from jax.experimental.pallas import tpu as pltpu

F32 = jnp.float32
BF16 = jnp.bfloat16
I32 = jnp.int32

CHUNK = 64
ROPE_THETA = 500000.0
ROPE_FRACTION = 4
NORM_EPS = 1e-6
A_HEADS, A_DK, A_DV = 8, 128, 128
A_W = A_HEADS * A_DV
B_HEADS, B_HD = 16, 64
B_W = B_HEADS * B_HD
B_LORA_DECAY, B_LORA_A, B_LORA_G = 64, 64, 160
B_GN_EPS = 64e-5
C_HEADS, C_KV_HEADS, C_HD = 16, 4, 128
C_W = C_HEADS * C_HD
C_KV_W = C_KV_HEADS * C_HD
IDX_HEADS, IDX_HD = 16, 64
TOPK_MAX = 256
CONV_W = 3

LANES = 128
SUBLANES = 8
VMEM_LIMIT = 56 * 1024 * 1024

COL_A = 0
COL_B_RKV = 4096
COL_B_LORA = 7168
COL_C_Q = 8192
COL_C_K = 10240
COL_C_V = 10752
COL_C_QI = 11264
COL_C_KW = 12288
COL_G = 12800
NEG_BIG = -1e30
INT_MIN = -2147483648


def _sigmoid(x):
    return 1.0 / (1.0 + jnp.exp(-x))


def _dot(a, b):
    return jnp.dot(a, b, preferred_element_type=F32)


def _dot_nt(a, b):
    return lax.dot_general(a, b, (((1,), (1,)), ((), ())), preferred_element_type=F32)


def _dot_tn(a, b):
    return lax.dot_general(a, b, (((0,), (0,)), ((), ())), preferred_element_type=F32)


def _split3(x):
    hi = x.astype(BF16)
    r1 = x - hi.astype(F32)
    mid = r1.astype(BF16)
    lo = (r1 - mid.astype(F32)).astype(BF16)
    return hi, mid, lo


def _dot_exact_lhs(m_bf16, x):
    hi, mid, lo = _split3(x)
    return _dot(m_bf16, hi) + _dot(m_bf16, mid) + _dot(m_bf16, lo)


def _dot_exact_rhs(x, m_bf16):
    hi, mid, lo = _split3(x)
    return _dot(hi, m_bf16) + _dot(mid, m_bf16) + _dot(lo, m_bf16)


def _split2(x):
    hi = x.astype(BF16)
    lo = (x - hi.astype(F32)).astype(BF16)
    return hi, lo


def _dot_hp(a, b):
    ah, al = _split2(a)
    bh, bl = _split2(b)
    return _dot(ah, bh) + _dot(ah, bl) + _dot(al, bh)


def _dot_nt_hp(a, b):
    ah, al = _split2(a)
    bh, bl = _split2(b)
    return _dot_nt(ah, bh) + _dot_nt(ah, bl) + _dot_nt(al, bh)


def _dot_tn_hp(a, b):
    ah, al = _split2(a)
    bh, bl = _split2(b)
    return _dot_tn(ah, bh) + _dot_tn(ah, bl) + _dot_tn(al, bh)


def _cparams(sem):
    return pltpu.CompilerParams(dimension_semantics=sem, vmem_limit_bytes=VMEM_LIMIT)


def _rmsnorm_kernel(x_ref, g_ref, o_ref):
    x = x_ref[...]
    ms = jnp.mean(x * x, axis=-1, keepdims=True)
    o_ref[...] = (x * lax.rsqrt(ms + NORM_EPS) * g_ref[...]).astype(o_ref.dtype)


def rmsnorm(x, gain, tm=256):
    m, d = x.shape
    return pl.pallas_call(
        _rmsnorm_kernel,
        grid=(m // tm,),
        in_specs=[pl.BlockSpec((tm, d), lambda i: (i, 0)), pl.BlockSpec((1, d), lambda i: (0, 0))],
        out_specs=pl.BlockSpec((tm, d), lambda i: (i, 0)),
        out_shape=jax.ShapeDtypeStruct((m, d), BF16),
        compiler_params=_cparams(("parallel",)),
        name="rmsnorm",
    )(x, gain.reshape(1, d))


def _proj_in_kernel(h_ref, w_ref, o_ref, *, sig_from):
    acc = _dot(h_ref[...], w_ref[...])
    j = pl.program_id(1)

    @pl.when(j < sig_from)
    def _():
        o_ref[...] = acc

    @pl.when(j >= sig_from)
    def _():
        o_ref[...] = _sigmoid(acc)


def proj_in(h, w, sig_col, tm=1024, tn=512):
    m, k = h.shape
    n = w.shape[1]
    tm = min(tm, m)
    return pl.pallas_call(
        functools.partial(_proj_in_kernel, sig_from=sig_col // tn),
        grid=(m // tm, n // tn),
        in_specs=[pl.BlockSpec((tm, k), lambda i, j: (i, 0)), pl.BlockSpec((k, tn), lambda i, j: (0, j))],
        out_specs=pl.BlockSpec((tm, tn), lambda i, j: (i, j)),
        out_shape=jax.ShapeDtypeStruct((m, n), F32),
        compiler_params=_cparams(("parallel", "arbitrary")),
        name="proj_in",
    )(h, w)


def _mm_res_kernel(a_ref, b_ref, r_ref, o_ref):
    o_ref[...] = r_ref[...] + _dot(a_ref[...], b_ref[...])


def matmul_residual(a, b, res, tm=512, tn=512):
    m, k = a.shape
    n = b.shape[1]
    tm = min(tm, m)
    tn = min(tn, n)
    return pl.pallas_call(
        _mm_res_kernel,
        grid=(m // tm, n // tn),
        in_specs=[
            pl.BlockSpec((tm, k), lambda i, j: (i, 0)),
            pl.BlockSpec((k, tn), lambda i, j: (0, j)),
            pl.BlockSpec((tm, tn), lambda i, j: (i, j)),
        ],
        out_specs=pl.BlockSpec((tm, tn), lambda i, j: (i, j)),
        out_shape=jax.ShapeDtypeStruct((m, n), F32),
        compiler_params=_cparams(("parallel", "arbitrary")),
        name="matmul_residual",
    )(a, b, res)


def _merge_kernel(oa_ref, ob_ref, oc_ref, wa_ref, wb_ref, wc_ref, ga_ref, gb_ref, gc_ref, o_ref):
    acc = ga_ref[...] * _dot(oa_ref[...], wa_ref[...])
    acc += gb_ref[...] * _dot(ob_ref[...], wb_ref[...])
    acc += gc_ref[...] * _dot(oc_ref[...], wc_ref[...])
    o_ref[...] = acc.astype(o_ref.dtype)


def merge_branches(o_a, o_b, o_c, w_a, w_b, w_c, p, d_model, tm=512, tn=512):
    m = o_a.shape[0]
    tm = min(tm, m)
    g0 = COL_G // tn
    gstep = d_model // tn
    return pl.pallas_call(
        _merge_kernel,
        grid=(m // tm, d_model // tn),
        in_specs=[
            pl.BlockSpec((tm, o_a.shape[1]), lambda i, j: (i, 0)),
            pl.BlockSpec((tm, o_b.shape[1]), lambda i, j: (i, 0)),
            pl.BlockSpec((tm, o_c.shape[1]), lambda i, j: (i, 0)),
            pl.BlockSpec((w_a.shape[0], tn), lambda i, j: (0, j)),
            pl.BlockSpec((w_b.shape[0], tn), lambda i, j: (0, j)),
            pl.BlockSpec((w_c.shape[0], tn), lambda i, j: (0, j)),
            pl.BlockSpec((tm, tn), lambda i, j: (i, g0 + j)),
            pl.BlockSpec((tm, tn), lambda i, j: (i, g0 + gstep + j)),
            pl.BlockSpec((tm, tn), lambda i, j: (i, g0 + 2 * gstep + j)),
        ],
        out_specs=pl.BlockSpec((tm, tn), lambda i, j: (i, j)),
        out_shape=jax.ShapeDtypeStruct((m, d_model), BF16),
        compiler_params=_cparams(("parallel", "arbitrary")),
        name="merge_branches",
    )(o_a, o_b, o_c, w_a, w_b, w_c, p, p, p)


def _ffn_up_kernel(h_ref, wg_ref, wu_ref, cwg_ref, cwu_ref, cbg_ref, cbu_ref, o_ref, cg_ref, cu_ref, *,
                   tiles_per_seq, tm):
    i = pl.program_id(1)
    first = (i % tiles_per_seq) == 0
    h = h_ref[...]
    rows = lax.broadcasted_iota(I32, (tm, o_ref.shape[1]), 0)

    @pl.when(first)
    def _():
        cg_ref[...] = jnp.zeros_like(cg_ref)
        cu_ref[...] = jnp.zeros_like(cu_ref)

    def conv(w_ref, cw_ref, cb_ref, carry_ref):
        u = _dot(h, w_ref[...])
        prev = carry_ref[...]
        u1 = jnp.where(rows == 0, prev[7:8, :], pltpu.roll(u, 1, axis=0))
        u2 = pltpu.roll(u, 2, axis=0)
        u2 = jnp.where(rows == 0, prev[6:7, :], jnp.where(rows == 1, prev[7:8, :], u2))
        carry_ref[...] = u[tm - SUBLANES:tm, :]
        cw = cw_ref[...]
        return cw[0:1, :] * u2 + cw[1:2, :] * u1 + cw[2:3, :] * u + cb_ref[...]

    gate = conv(wg_ref, cwg_ref, cbg_ref, cg_ref)
    up = conv(wu_ref, cwu_ref, cbu_ref, cu_ref)
    o_ref[...] = (gate * _sigmoid(gate) * up).astype(o_ref.dtype)


def ffn_up(h, w_up, conv_w, conv_b, seq, tm=512, tn=256):
    m, k = h.shape
    d_ff = w_up.shape[1] // 2
    tm = min(tm, seq)
    nj = d_ff // tn
    return pl.pallas_call(
        functools.partial(_ffn_up_kernel, tiles_per_seq=seq // tm, tm=tm),
        grid=(nj, m // tm),
        in_specs=[
            pl.BlockSpec((tm, k), lambda j, i: (i, 0)),
            pl.BlockSpec((k, tn), lambda j, i: (0, j)),
            pl.BlockSpec((k, tn), lambda j, i: (0, nj + j)),
            pl.BlockSpec((CONV_W, tn), lambda j, i: (0, j)),
            pl.BlockSpec((CONV_W, tn), lambda j, i: (0, nj + j)),
            pl.BlockSpec((1, tn), lambda j, i: (0, j)),
            pl.BlockSpec((1, tn), lambda j, i: (0, nj + j)),
        ],
        out_specs=pl.BlockSpec((tm, tn), lambda j, i: (i, j)),
        out_shape=jax.ShapeDtypeStruct((m, d_ff), BF16),
        scratch_shapes=[pltpu.VMEM((SUBLANES, tn), F32), pltpu.VMEM((SUBLANES, tn), F32)],
        compiler_params=_cparams(("parallel", "arbitrary")),
        name="ffn_up",
    )(h, w_up, w_up, conv_w, conv_w, conv_b, conv_b)


A_SUB = 16


def _hgrn_kernel(q_ref, f_ref, i_ref, g_ref, lb_ref, gain_ref, o_ref, st_ref, *, n_chunks):
    @pl.when(pl.program_id(2) == 0)
    def _():
        st_ref[...] = jnp.zeros_like(st_ref)

    lb = lb_ref[...]
    gain = gain_ref[...]
    tri = (lax.broadcasted_iota(I32, (CHUNK, CHUNK), 0) >= lax.broadcasted_iota(I32, (CHUNK, CHUNK), 1)).astype(BF16)
    rows_sub = lax.broadcasted_iota(I32, (A_SUB, A_DK), 0)
    n_sub = CHUNK // A_SUB

    def chunk(c, carry):
        r0 = pl.multiple_of(c * CHUNK, CHUNK)
        z = f_ref[pl.ds(r0, CHUNK), :]
        qv = q_ref[pl.ds(r0, CHUNK), :]
        iv = i_ref[pl.ds(r0, CHUNK), :]
        gv = g_ref[pl.ds(r0, CHUNK), :]
        lf = jnp.log(lb + (1.0 - lb) * _sigmoid(z))
        kin = (1.0 - lb) * _sigmoid(-z)
        qf = qv * _sigmoid(qv)
        b = _dot_exact_lhs(tri, lf)
        st = st_ref[...]
        o_inter = _dot_nt((qf * jnp.exp(b)).astype(BF16), st.astype(BF16))
        iv_b = iv.astype(BF16)
        outs = []
        for si in range(n_sub):
            lo = si * A_SUB
            b_i = b[lo:lo + A_SUB]
            q_i = qf[lo:lo + A_SUB]
            k_i = kin[lo:lo + A_SUB]
            i_i = iv[lo:lo + A_SUB]
            o_i = o_inter[lo:lo + A_SUB]
            if si > 0:
                b_ref_row = b[lo - 1:lo]
                q_s = (q_i * jnp.exp(b_i - b_ref_row)).astype(BF16)
                k_s = (kin[0:lo] * jnp.exp(b_ref_row - b[0:lo])).astype(BF16)
                att = _dot_nt(q_s, k_s)
                o_i = o_i + _dot(att.astype(BF16), iv_b[0:lo])
            for s in range(A_SUB):
                d = jnp.exp(jnp.where(rows_sub >= s, b_i - b_i[s:s + 1], -jnp.inf))
                a = jnp.sum(q_i * d * k_i[s:s + 1], axis=1, keepdims=True)
                o_i = o_i + a * i_i[s:s + 1]
            outs.append(o_i)
        o = jnp.concatenate(outs, axis=0)
        b_last = b[CHUNK - 1:CHUNK]
        k_dec = (kin * jnp.exp(b_last - b)).astype(BF16)
        st_ref[...] = st * jnp.exp(b_last) + _dot_tn(iv_b, k_dec)
        ms = jnp.mean(o * o, axis=-1, keepdims=True)
        on = o * lax.rsqrt(ms + NORM_EPS) * gain
        o_ref[pl.ds(r0, CHUNK), :] = (on * (gv * _sigmoid(gv))).astype(o_ref.dtype)
        return carry

    lax.fori_loop(0, n_chunks, chunk, 0)


def hgrn2(p3, lb, out_gain, s_blk=512):
    bsz, seq, _ = p3.shape
    s_blk = min(s_blk, seq)
    c0 = COL_A // A_DK

    def col(off):
        return pl.BlockSpec((None, s_blk, A_DK), lambda b, h, s, off=off: (b, s, c0 + off + h))

    return pl.pallas_call(
        functools.partial(_hgrn_kernel, n_chunks=s_blk // CHUNK),
        grid=(bsz, A_HEADS, seq // s_blk),
        in_specs=[col(0), col(A_HEADS), col(2 * A_HEADS), col(3 * A_HEADS),
                  pl.BlockSpec((1, A_DK), lambda b, h, s: (0, h)),
                  pl.BlockSpec((1, A_DV), lambda b, h, s: (0, 0))],
        out_specs=pl.BlockSpec((None, s_blk, A_DV), lambda b, h, s: (b, s, h)),
        out_shape=jax.ShapeDtypeStruct((bsz, seq, A_W), BF16),
        scratch_shapes=[pltpu.VMEM((A_DV, A_DK), F32)],
        compiler_params=_cparams(("parallel", "parallel", "arbitrary")),
        name="hgrn2",
    )(p3, p3, p3, p3, lb.reshape(1, A_HEADS * A_DK), out_gain.reshape(1, A_DV))


def _rwkv_prep_kernel(r_ref, k_ref, v_ref, l_ref, rp_ref, kp_ref, vp_ref, lp_ref,
                      mu_ref, w0_ref, w2_ref, a0_ref, a2_ref, g2_ref, kk_ref, ka_ref,
                      ro_ref, ld_ref, k2_ref, vo_ref, kko_ref, kka_ref, go_ref, *, tiles_per_seq, tm):
    first = (pl.program_id(0) % tiles_per_seq) == 0
    rows = lax.broadcasted_iota(I32, (tm, B_W), 0)

    def shifted(cur_ref, prev_ref, part):
        cur = cur_ref[...]
        prev = jnp.where(first, 0.0, prev_ref[...])[SUBLANES - 1:SUBLANES, :]
        sh = jnp.where(rows == 0, prev, pltpu.roll(cur, 1, axis=0))
        return cur + (sh - cur) * mu_ref[part:part + 1, :]

    r = shifted(r_ref, rp_ref, 0)
    k = shifted(k_ref, kp_ref, 1)
    v = shifted(v_ref, vp_ref, 2)
    lo = shifted(l_ref, lp_ref, 3)
    lo_a = lo[:, 0:LANES]
    lo_g = lo[:, LANES:3 * LANES]
    wpre = w0_ref[...] + _dot(jnp.tanh(lo_a).astype(BF16), w2_ref[...])
    y = -wpre
    softplus = jnp.maximum(y, 0.0) + jnp.log(1.0 + jnp.exp(-jnp.abs(y)))
    w_log = -softplus - 0.5
    ld_ref[...] = -jnp.exp(w_log)
    a = _sigmoid(a0_ref[...] + _dot(lo_a.astype(BF16), a2_ref[...]))
    go_ref[...] = _dot(_sigmoid(lo_g).astype(BF16), g2_ref[...])
    kk = k * kk_ref[...]
    bd = (lax.broadcasted_iota(I32, (LANES, LANES), 0) // B_HD
          == lax.broadcasted_iota(I32, (LANES, LANES), 1) // B_HD).astype(BF16)
    sq = kk * kk
    ss = jnp.concatenate([_dot_exact_rhs(sq[:, j * LANES:(j + 1) * LANES], bd) for j in range(B_W // LANES)], axis=1)
    kk = kk / jnp.maximum(jnp.sqrt(ss), 1e-12)
    ro_ref[...] = r
    vo_ref[...] = v
    kko_ref[...] = kk
    kka_ref[...] = kk * a
    k2_ref[...] = k * (1.0 + (a - 1.0) * ka_ref[...])


def rwkv_prep(p, seq, mu4, w0, w2p, a0, a2p, g2p, k_k, k_a, tm=256):
    m = p.shape[0]
    tm = min(tm, seq)
    cb = COL_B_RKV // B_W
    pb = tm // SUBLANES

    def cur(j):
        return pl.BlockSpec((tm, B_W), lambda i, j=j: (i, cb + j))

    def prev(j):
        return pl.BlockSpec((SUBLANES, B_W), lambda i, j=j: (jnp.maximum(i * pb - 1, 0), cb + j))

    def full(a):
        return pl.BlockSpec(a.shape, lambda i: (0, 0))

    row = lambda a: a.reshape(1, B_W)
    params = [mu4, row(w0), w2p, row(a0), a2p, g2p, row(k_k), row(k_a)]
    out = jax.ShapeDtypeStruct((m, B_W), F32)
    return pl.pallas_call(
        functools.partial(_rwkv_prep_kernel, tiles_per_seq=seq // tm, tm=tm),
        grid=(m // tm,),
        in_specs=[cur(0), cur(1), cur(2), cur(3), prev(0), prev(1), prev(2), prev(3)] + [full(a) for a in params],
        out_specs=[pl.BlockSpec((tm, B_W), lambda i: (i, 0))] * 7,
        out_shape=[out] * 7,
        compiler_params=_cparams(("parallel",)),
        name="rwkv_prep",
    )(p, p, p, p, p, p, p, p, *params)


B_T = 64


def _rwkv_kernel(r_ref, ld_ref, k_ref, v_ref, kk_ref, kka_ref, g_ref, rk_ref, lnw_ref, lnb_ref, o_ref, st_ref, *,
                 n_chunks):
    @pl.when(pl.program_id(2) == 0)
    def _():
        st_ref[...] = jnp.zeros_like(st_ref)

    t = B_T
    ii = lax.broadcasted_iota(I32, (t, t), 0)
    jj = lax.broadcasted_iota(I32, (t, t), 1)
    tri = (ii >= jj).astype(BF16)
    i2 = lax.broadcasted_iota(I32, (2 * t, 2 * t), 0)
    j2 = lax.broadcasted_iota(I32, (2 * t, 2 * t), 1)
    same = (i2 // t) == (j2 // t)
    strict_bd = same & ((i2 % t) > (j2 % t))
    incl_bd = same & ((i2 % t) >= (j2 % t))
    head_bd = (i2 // B_HD) == (j2 // B_HD)
    head_bd_avg = head_bd.astype(BF16)
    lane = lax.broadcasted_iota(I32, (t, LANES), 1)
    h0 = lane < B_HD
    rk = rk_ref[...]
    lnw = lnw_ref[...]
    lnb = lnb_ref[...]

    def stack(x):
        return jnp.concatenate([jnp.where(h0, x, 0.0), jnp.where(h0, 0.0, x)], axis=0)

    def chunk(c, carry):
        r0 = pl.multiple_of(c * t, t)
        sl = pl.ds(r0, t)
        r = r_ref[sl, :]
        ld = ld_ref[sl, :]
        k = k_ref[sl, :]
        v = v_ref[sl, :]
        kk = kk_ref[sl, :]
        kka = kka_ref[sl, :]
        cs = _dot_exact_lhs(tri, ld)
        c_last = cs[t - 1:t]
        e_neg = jnp.exp(-cs)
        a_t = -kk * jnp.exp(cs - ld)
        r_t = r * jnp.exp(cs)
        b_t = kka * e_neg
        k_t = k * e_neg
        lhs = jnp.concatenate([stack(a_t), stack(r_t)], axis=0)
        rhs = jnp.concatenate([b_t, b_t, k_t, k_t], axis=0)
        sc = _dot_nt_hp(lhs, rhs)
        st = st_ref[...]
        proj = _dot_nt_hp(lhs, st)
        n = jnp.where(strict_bd, sc[0:2 * t, 0:2 * t], 0.0)
        a_ak = jnp.where(strict_bd, sc[0:2 * t, 2 * t:4 * t], 0.0)
        vs = stack(v)
        xs = proj[0:2 * t] + _dot_hp(a_ak, vs)
        for it in range(6):
            xs = xs + _dot_hp(n, xs)
            if it < 5:
                n = _dot_hp(n, n)
        m_r = jnp.concatenate([jnp.where(incl_bd, sc[2 * t:4 * t, 0:2 * t], 0.0),
                               jnp.where(incl_bd, sc[2 * t:4 * t, 2 * t:4 * t], 0.0)], axis=1)
        uv = jnp.concatenate([xs, vs], axis=0)
        os_ = proj[2 * t:4 * t] + _dot_hp(m_r, uv)
        o = os_[0:t] + os_[t:2 * t]
        u = xs[0:t] + xs[t:2 * t]
        dec = jnp.exp(c_last - cs)
        upd = _dot_tn_hp(jnp.concatenate([u, v], axis=0), jnp.concatenate([kka * dec, k * dec], axis=0))
        st_ref[...] = st * jnp.exp(c_last) + jnp.where(head_bd, upd, 0.0)
        inv = 1.0 / B_HD
        mean = _dot_exact_rhs(o, head_bd_avg) * inv
        d = o - mean
        var = _dot_exact_rhs(d * d, head_bd_avg) * inv
        on = d * lax.rsqrt(var + B_GN_EPS) * lnw + lnb
        bonus = _dot_exact_rhs(r * k * rk, head_bd_avg)
        o_ref[sl, :] = ((on + bonus * v) * g_ref[sl, :]).astype(o_ref.dtype)
        return carry

    lax.fori_loop(0, n_chunks, chunk, 0)


def rwkv_recurrence(r, ld, k2, v, kk, kka, g, r_k, ln_w, ln_b, bsz, seq, s_blk=512):
    s_blk = min(s_blk, seq)
    npair = B_W // LANES
    args = [a.reshape(bsz, seq, B_W) for a in (r, ld, k2, v, kk, kka, g)]
    blk = pl.BlockSpec((None, s_blk, LANES), lambda b, h, s: (b, s, h))
    par = pl.BlockSpec((1, LANES), lambda b, h, s: (0, h))
    return pl.pallas_call(
        functools.partial(_rwkv_kernel, n_chunks=s_blk // B_T),
        grid=(bsz, npair, seq // s_blk),
        in_specs=[blk] * 7 + [par] * 3,
        out_specs=blk,
        out_shape=jax.ShapeDtypeStruct((bsz, seq, B_W), BF16),
        scratch_shapes=[pltpu.VMEM((LANES, LANES), F32)],
        compiler_params=_cparams(("parallel", "parallel", "arbitrary")),
        name="rwkv_recurrence",
    )(*args, r_k.reshape(1, B_W), ln_w.reshape(1, B_W), ln_b.reshape(1, B_W))


def _rope(x, cos_t, sin_t, lane_in_head, half):
    partner = jnp.where(lane_in_head < half, pltpu.roll(x, LANES - half, axis=1), pltpu.roll(x, half, axis=1))
    return x * cos_t + partner * sin_t


def _attn_prep_kernel(q_ref, k_ref, v_ref, qi_ref, kw_ref, ca_ref, sa_ref, ci_ref, si_ref, qg_ref, kg_ref,
                      qo_ref, ko_ref, vo_ref, qio_ref, kio_ref, wo_ref, *, tm):
    ca, sa, ci, si = ca_ref[...], sa_ref[...], ci_ref[...], si_ref[...]
    lane = lax.broadcasted_iota(I32, (tm, LANES), 1)
    lane_i = lane % IDX_HD
    half_a = C_HD // ROPE_FRACTION // 2
    half_i = IDX_HD // ROPE_FRACTION // 2
    scale = C_HD ** -0.5

    def norm_rope(x, gain):
        ms = jnp.mean(x * x, axis=-1, keepdims=True)
        return _rope(x * lax.rsqrt(ms + NORM_EPS) * gain, ca, sa, lane, half_a)

    for h in range(C_HEADS):
        sl = slice(h * C_HD, (h + 1) * C_HD)
        qo_ref[:, sl] = (norm_rope(q_ref[:, sl], qg_ref[...]) * scale).astype(qo_ref.dtype)
    for h in range(C_KV_HEADS):
        sl = slice(h * C_HD, (h + 1) * C_HD)
        ko_ref[:, sl] = norm_rope(k_ref[:, sl], kg_ref[...]).astype(ko_ref.dtype)
    vo_ref[...] = v_ref[...].astype(vo_ref.dtype)
    for j in range(IDX_HEADS * IDX_HD // LANES):
        sl = slice(j * LANES, (j + 1) * LANES)
        qio_ref[:, sl] = _rope(qi_ref[:, sl], ci, si, lane_i, half_i).astype(qio_ref.dtype)
    kw = kw_ref[:, 0:LANES]
    kr = _rope(kw, ci, si, lane_i, half_i)
    kio_ref[...] = jnp.where(lane < IDX_HD, kr, pltpu.roll(kr, IDX_HD, axis=1)).astype(kio_ref.dtype)
    w = pltpu.roll(kw, LANES - IDX_HD, axis=1) * (IDX_HEADS ** -0.5 * IDX_HD ** -0.5)
    wo_ref[...] = jnp.where(lane < IDX_HEADS, w, 0.0)


def attn_prep(p, tabs, q_gain, k_gain, tm=256):
    m = p.shape[0]
    tm = min(tm, m)

    def colblk(width, off):
        return pl.BlockSpec((tm, width), lambda i: (i, off // width))

    tab = pl.BlockSpec((tm, LANES), lambda i: (i, 0))
    gain = pl.BlockSpec((1, C_HD), lambda i: (0, 0))
    kw_width = 512

    def out(width, dtype):
        return jax.ShapeDtypeStruct((m, width), dtype), pl.BlockSpec((tm, width), lambda i: (i, 0))

    outs = [out(C_W, BF16), out(C_KV_W, BF16), out(C_KV_W, BF16), out(IDX_HEADS * IDX_HD, BF16),
            out(LANES, BF16), out(LANES, F32)]
    return pl.pallas_call(
        functools.partial(_attn_prep_kernel, tm=tm),
        grid=(m // tm,),
        in_specs=[colblk(C_W, COL_C_Q), colblk(C_KV_W, COL_C_K), colblk(C_KV_W, COL_C_V),
                  colblk(IDX_HEADS * IDX_HD, COL_C_QI), colblk(kw_width, COL_C_KW), tab, tab, tab, tab, gain, gain],
        out_specs=[o[1] for o in outs],
        out_shape=[o[0] for o in outs],
        compiler_params=_cparams(("parallel",)),
        name="attn_prep",
    )(p, p, p, p, p, *tabs, q_gain.reshape(1, C_HD), k_gain.reshape(1, C_HD))


def _attn_kernel(q_ref, k_ref, v_ref, qi_ref, ki_ref, w_ref, o_ref, key_sc, bias_sc, *, tq, tk, topk):
    qb = pl.program_id(1)
    nkb = ((qb + 1) * tq + tk - 1) // tk
    lane = lax.broadcasted_iota(I32, (tq, LANES), 1)
    w = w_ref[...]
    row_chunk = (qb * tq + lax.broadcasted_iota(I32, (tq, tk), 0)) // CHUNK
    col_in_blk = lax.broadcasted_iota(I32, (tq, tk), 1)

    def score_block(kb, carry):
        c0 = pl.multiple_of(kb * tk, tk)
        ki2 = ki_ref[pl.ds(c0, tk), :]
        sc = jnp.zeros((tq, tk), F32)
        for hp in range(IDX_HEADS * IDX_HD // LANES):
            qp = qi_ref[:, hp * LANES:(hp + 1) * LANES]
            for e in range(2):
                qm = jnp.where((lane < IDX_HD) if e == 0 else (lane >= IDX_HD), qp, jnp.zeros_like(qp))
                rel = jnp.maximum(_dot_nt(qm, ki2), 0.0)
                h = 2 * hp + e
                sc = sc + w[:, h:h + 1] * rel
        sc = jnp.where(sc == 0.0, 0.0, sc)
        bits = lax.bitcast_convert_type(sc, I32)
        skey = bits ^ ((bits >> 31) & 0x7FFFFFFF)
        allowed = ((c0 + col_in_blk) // CHUNK) <= row_chunk
        key_sc[kb] = jnp.where(allowed, skey, INT_MIN)
        return carry

    lax.fori_loop(0, nkb, score_block, 0)

    def count_ge(cand):
        def body(kb, acc):
            return acc + jnp.sum(jnp.where(key_sc[kb] >= cand, 1.0, 0.0), axis=1, keepdims=True)
        return lax.fori_loop(0, nkb, body, jnp.zeros((tq, 1), F32))

    kf = float(topk)
    thr = jnp.where(count_ge(jnp.zeros((tq, 1), I32)) >= kf, 0, INT_MIN).astype(I32)

    def bit_step(i, thr):
        cand = thr | (jnp.int32(1) << (30 - i))
        return jnp.where(count_ge(cand) >= kf, cand, thr)

    thr = lax.fori_loop(0, 31, bit_step, thr)
    thr = jnp.maximum(thr, INT_MIN + 1)

    def bias_block(kb, carry):
        bias_sc[kb] = jnp.where(key_sc[kb] >= thr, 0.0, NEG_BIG)
        return carry

    lax.fori_loop(0, nkb, bias_block, 0)

    rep = C_HEADS // C_KV_HEADS
    for g in range(C_KV_HEADS):
        qs = jnp.concatenate([q_ref[:, (g * rep + r) * C_HD:(g * rep + r + 1) * C_HD] for r in range(rep)], axis=0)

        def body(kb, carry):
            m_prev, l_prev, acc = carry
            c0 = pl.multiple_of(kb * tk, tk)
            kblk = k_ref[pl.ds(c0, tk), g * C_HD:(g + 1) * C_HD]
            vblk = v_ref[pl.ds(c0, tk), g * C_HD:(g + 1) * C_HD]
            bias = bias_sc[kb]
            s = _dot_nt(qs, kblk)
            s = jnp.concatenate([s[r * tq:(r + 1) * tq] + bias for r in range(rep)], axis=0)
            m_new = jnp.maximum(m_prev, jnp.max(s, axis=1, keepdims=True))
            p = jnp.exp(s - m_new)
            alpha = jnp.exp(m_prev - m_new)
            l_new = alpha * l_prev + jnp.sum(p, axis=1, keepdims=True)
            acc = alpha * acc + _dot(p.astype(BF16), vblk)
            return m_new, l_new, acc

        init = (jnp.full((rep * tq, 1), NEG_BIG, F32), jnp.zeros((rep * tq, 1), F32), jnp.zeros((rep * tq, C_HD), F32))
        _, l_fin, acc = lax.fori_loop(0, nkb, body, init)
        out = acc / l_fin
        for r in range(rep):
            h = g * rep + r
            o_ref[:, h * C_HD:(h + 1) * C_HD] = out[r * tq:(r + 1) * tq].astype(o_ref.dtype)


def sparse_attention(qn, kn, vn, qi, ki2, w, bsz, seq, tq=128, tk=512):
    tk = min(tk, seq)
    topk = min(TOPK_MAX, seq // 4)
    r3 = lambda a: a.reshape(bsz, seq, a.shape[-1])
    qblk = lambda width: pl.BlockSpec((None, tq, width), lambda b, i: (b, i, 0))
    sblk = lambda width: pl.BlockSpec((None, seq, width), lambda b, i: (b, 0, 0))
    return pl.pallas_call(
        functools.partial(_attn_kernel, tq=tq, tk=tk, topk=topk),
        grid=(bsz, seq // tq),
        in_specs=[qblk(C_W), sblk(C_KV_W), sblk(C_KV_W), qblk(IDX_HEADS * IDX_HD), sblk(LANES), qblk(LANES)],
        out_specs=qblk(C_W),
        out_shape=jax.ShapeDtypeStruct((bsz, seq, C_W), BF16),
        scratch_shapes=[pltpu.VMEM((seq // tk, tq, tk), I32), pltpu.VMEM((seq // tk, tq, tk), F32)],
        compiler_params=_cparams(("parallel", "arbitrary")),
        name="sparse_attention",
    )(r3(qn), r3(kn), r3(vn), r3(qi), r3(ki2), r3(w))


def _pack_w_in(w):
    d = w.shape[0]
    a_cols = 4 * A_W
    b_cols = 3 * B_W + B_LORA_DECAY + B_LORA_A + B_LORA_G
    c_cols = C_W + 2 * C_KV_W + IDX_HEADS * IDX_HD + IDX_HD + IDX_HEADS
    e_b = a_cols + b_cols
    e_c = e_b + c_cols
    z = lambda n: jnp.zeros((d, n), w.dtype)
    packed = jnp.concatenate([w[:, :e_b], z(COL_C_Q - e_b), w[:, e_b:e_c], z(COL_G - COL_C_Q - c_cols), w[:, e_c:]],
                             axis=1)
    return packed.astype(BF16)


def _rope_tables(positions):
    pos = positions.astype(F32).reshape(-1, 1)

    def tables(head_dim):
        rot = head_dim // ROPE_FRACTION
        half = rot // 2
        inv_freq = ROPE_THETA ** (-jnp.arange(0, rot, 2, dtype=F32) / rot)
        ang = pos * inv_freq
        cos, sin = jnp.cos(ang), jnp.sin(ang)
        ones = jnp.ones((pos.shape[0], head_dim - rot), F32)
        c = jnp.concatenate([cos, cos, ones], axis=1)
        s = jnp.concatenate([-sin, sin, 0.0 * ones], axis=1)
        reps = LANES // head_dim
        return jnp.tile(c, (1, reps)), jnp.tile(s, (1, reps))

    ca, sa = tables(C_HD)
    ci, si = tables(IDX_HD)
    return ca, sa, ci, si


def _pad_rows(a, before, total):
    return jnp.zeros((total, a.shape[1]), a.dtype).at[before:before + a.shape[0]].set(a)


def kernel(x, positions, norm_mix, w_in, hgrn_lb_logits, hgrn_out_norm, rwkv_mu, rwkv_w0, rwkv_w2, rwkv_a0, rwkv_a2,
           rwkv_g2, rwkv_k_k, rwkv_k_a, rwkv_r_k, rwkv_ln_w, rwkv_ln_b, q_norm, k_norm, w_branch_a, w_branch_b,
           w_branch_c, w_out, norm_ffn, w_up, conv_w, conv_b, w_down):
    bsz, seq, d_model = x.shape
    depth = w_in.shape[0]
    m = bsz * seq
    tabs = _rope_tables(positions)
    lb_all = jnp.cumsum(jax.nn.softmax(hgrn_lb_logits.astype(F32), axis=0), axis=0)
    lb_all = lb_all - lb_all[0:1]
    n_lora = B_LORA_DECAY + B_LORA_A + B_LORA_G

    xf = x.reshape(m, d_model)
    for l in range(depth):
        h = rmsnorm(xf, norm_mix[l])
        p = proj_in(h, _pack_w_in(w_in[l]), COL_G)

        o_a = hgrn2(p.reshape(bsz, seq, -1), lb_all[l], hgrn_out_norm[l]).reshape(m, A_W)

        mu = rwkv_mu[l]
        mu4 = jnp.stack([mu[0:B_W], mu[B_W:2 * B_W], mu[2 * B_W:3 * B_W],
                         jnp.pad(mu[3 * B_W:], (0, B_W - n_lora))])
        w2p = _pad_rows(rwkv_w2[l], 0, LANES).astype(BF16)
        a2p = _pad_rows(rwkv_a2[l], B_LORA_DECAY, LANES).astype(BF16)
        g2p = _pad_rows(rwkv_g2[l], 0, 2 * LANES).astype(BF16)
        parts = rwkv_prep(p, seq, mu4, rwkv_w0[l], w2p, rwkv_a0[l], a2p, g2p, rwkv_k_k[l], rwkv_k_a[l])
        o_b = rwkv_recurrence(*parts, rwkv_r_k[l], rwkv_ln_w[l], rwkv_ln_b[l], bsz, seq).reshape(m, B_W)

        qn, kn, vn, qi, ki2, wi = attn_prep(p, tabs, q_norm[l], k_norm[l])
        o_c = sparse_attention(qn, kn, vn, qi, ki2, wi, bsz, seq).reshape(m, C_W)

        merged = merge_branches(o_a, o_b, o_c, w_branch_a[l].astype(BF16), w_branch_b[l].astype(BF16),
                                w_branch_c[l].astype(BF16), p, d_model)
        xf = matmul_residual(merged, w_out[l].astype(BF16), xf)

        h2 = rmsnorm(xf, norm_ffn[l])
        act = ffn_up(h2, w_up[l].astype(BF16), conv_w[l], conv_b[l].reshape(1, -1), seq)
        xf = matmul_residual(act, w_down[l].astype(BF16), xf, tm=512, tn=256)
    return xf.reshape(bsz, seq, d_model)
```

```python
import functools

import jax
import jax.numpy as jnp
import numpy as np
from jax import lax
from jax.experimental import pallas as pl
from jax.experimental.pallas import tpu as pltpu

F32 = jnp.float32
BF16 = jnp.bfloat16
I32 = jnp.int32

CHUNK = 64
ROPE_THETA = 500000.0
ROPE_FRACTION = 4
NORM_EPS = 1e-6
A_HEADS, A_DK, A_DV = 8, 128, 128
A_W = A_HEADS * A_DV
B_HEADS, B_HD = 16, 64
B_W = B_HEADS * B_HD
B_LORA_DECAY, B_LORA_A, B_LORA_G = 64, 64, 160
B_GN_EPS = 64e-5
C_HEADS, C_KV_HEADS, C_HD = 16, 4, 128
C_W = C_HEADS * C_HD
C_KV_W = C_KV_HEADS * C_HD
IDX_HEADS, IDX_HD = 16, 64
TOPK_MAX = 256
CONV_W = 3

LANES = 128
SUBLANES = 8
VMEM_LIMIT = 56 * 1024 * 1024

COL_A = 0
COL_B_RKV = 4096
COL_B_LORA = 7168
COL_C_Q = 8192
COL_C_K = 10240
COL_C_V = 10752
COL_C_QI = 11264
COL_C_KW = 12288
COL_G = 12800
LOG2_E = 1.4426950408889634
NEG_BIG = -1e30
INT_MIN = -2147483648


def _sigmoid(x):
    return 1.0 / (1.0 + jnp.exp(-x))


def _dot(a, b):
    return jnp.dot(a, b, preferred_element_type=F32)


def _dot_nt(a, b):
    return lax.dot_general(a, b, (((1,), (1,)), ((), ())), preferred_element_type=F32)


def _dot_tn(a, b):
    return lax.dot_general(a, b, (((0,), (0,)), ((), ())), preferred_element_type=F32)


def _split3(x):
    hi = x.astype(BF16)
    r1 = x - hi.astype(F32)
    mid = r1.astype(BF16)
    lo = (r1 - mid.astype(F32)).astype(BF16)
    return hi, mid, lo


def _dot_exact_lhs(m_bf16, x):
    hi, mid, lo = _split3(x)
    return _dot(m_bf16, hi) + _dot(m_bf16, mid) + _dot(m_bf16, lo)


def _dot_exact_rhs(x, m_bf16):
    hi, mid, lo = _split3(x)
    return _dot(hi, m_bf16) + _dot(mid, m_bf16) + _dot(lo, m_bf16)


def _split2(x):
    hi = x.astype(BF16)
    lo = (x - hi.astype(F32)).astype(BF16)
    return hi, lo


def _dot_hp(a, b):
    ah, al = _split2(a)
    bh, bl = _split2(b)
    return _dot(ah, bh) + _dot(ah, bl) + _dot(al, bh)


def _dot_nt_hp(a, b):
    ah, al = _split2(a)
    bh, bl = _split2(b)
    return _dot_nt(ah, bh) + _dot_nt(ah, bl) + _dot_nt(al, bh)


def _dot_tn_hp(a, b):
    ah, al = _split2(a)
    bh, bl = _split2(b)
    return _dot_tn(ah, bh) + _dot_tn(ah, bl) + _dot_tn(al, bh)


def _cparams(sem):
    return pltpu.CompilerParams(dimension_semantics=sem, vmem_limit_bytes=VMEM_LIMIT)


def _rmsnorm_kernel(x_ref, g_ref, o_ref):
    x = x_ref[...]
    ms = jnp.mean(x * x, axis=-1, keepdims=True)
    o_ref[...] = (x * lax.rsqrt(ms + NORM_EPS) * g_ref[...]).astype(o_ref.dtype)


def rmsnorm(x, gain, tm=256):
    m, d = x.shape
    return pl.pallas_call(
        _rmsnorm_kernel,
        grid=(m // tm,),
        in_specs=[pl.BlockSpec((tm, d), lambda i: (i, 0)), pl.BlockSpec((1, d), lambda i: (0, 0))],
        out_specs=pl.BlockSpec((tm, d), lambda i: (i, 0)),
        out_shape=jax.ShapeDtypeStruct((m, d), BF16),
        compiler_params=_cparams(("parallel",)),
        name="rmsnorm",
    )(x, gain.reshape(1, d))


def _proj_in_kernel(h_ref, w_ref, o_ref, *, sig_from):
    acc = _dot(h_ref[...], w_ref[...])
    j = pl.program_id(1)

    @pl.when(j < sig_from)
    def _():
        o_ref[...] = acc

    @pl.when(j >= sig_from)
    def _():
        o_ref[...] = _sigmoid(acc)


def proj_in(h, w, sig_col, tm=1024, tn=512):
    m, k = h.shape
    n = w.shape[1]
    tm = min(tm, m)
    return pl.pallas_call(
        functools.partial(_proj_in_kernel, sig_from=sig_col // tn),
        grid=(m // tm, n // tn),
        in_specs=[pl.BlockSpec((tm, k), lambda i, j: (i, 0)), pl.BlockSpec((k, tn), lambda i, j: (0, j))],
        out_specs=pl.BlockSpec((tm, tn), lambda i, j: (i, j)),
        out_shape=jax.ShapeDtypeStruct((m, n), F32),
        compiler_params=_cparams(("parallel", "arbitrary")),
        name="proj_in",
    )(h, w)


def _mm_res_kernel(a_ref, b_ref, r_ref, o_ref):
    o_ref[...] = r_ref[...] + _dot(a_ref[...], b_ref[...])


def matmul_residual(a, b, res, tm=512, tn=512):
    m, k = a.shape
    n = b.shape[1]
    tm = min(tm, m)
    tn = min(tn, n)
    return pl.pallas_call(
        _mm_res_kernel,
        grid=(m // tm, n // tn),
        in_specs=[
            pl.BlockSpec((tm, k), lambda i, j: (i, 0)),
            pl.BlockSpec((k, tn), lambda i, j: (0, j)),
            pl.BlockSpec((tm, tn), lambda i, j: (i, j)),
        ],
        out_specs=pl.BlockSpec((tm, tn), lambda i, j: (i, j)),
        out_shape=jax.ShapeDtypeStruct((m, n), F32),
        compiler_params=_cparams(("parallel", "arbitrary")),
        name="matmul_residual",
    )(a, b, res)


def _merge_kernel(oa_ref, ob_ref, oc_ref, wa_ref, wb_ref, wc_ref, ga_ref, gb_ref, gc_ref, o_ref):
    acc = ga_ref[...] * _dot(oa_ref[...], wa_ref[...])
    acc += gb_ref[...] * _dot(ob_ref[...], wb_ref[...])
    acc += gc_ref[...] * _dot(oc_ref[...], wc_ref[...])
    o_ref[...] = acc.astype(o_ref.dtype)


def merge_branches(o_a, o_b, o_c, w_a, w_b, w_c, p, d_model, tm=512, tn=512):
    m = o_a.shape[0]
    tm = min(tm, m)
    g0 = COL_G // tn
    gstep = d_model // tn
    return pl.pallas_call(
        _merge_kernel,
        grid=(m // tm, d_model // tn),
        in_specs=[
            pl.BlockSpec((tm, o_a.shape[1]), lambda i, j: (i, 0)),
            pl.BlockSpec((tm, o_b.shape[1]), lambda i, j: (i, 0)),
            pl.BlockSpec((tm, o_c.shape[1]), lambda i, j: (i, 0)),
            pl.BlockSpec((w_a.shape[0], tn), lambda i, j: (0, j)),
            pl.BlockSpec((w_b.shape[0], tn), lambda i, j: (0, j)),
            pl.BlockSpec((w_c.shape[0], tn), lambda i, j: (0, j)),
            pl.BlockSpec((tm, tn), lambda i, j: (i, g0 + j)),
            pl.BlockSpec((tm, tn), lambda i, j: (i, g0 + gstep + j)),
            pl.BlockSpec((tm, tn), lambda i, j: (i, g0 + 2 * gstep + j)),
        ],
        out_specs=pl.BlockSpec((tm, tn), lambda i, j: (i, j)),
        out_shape=jax.ShapeDtypeStruct((m, d_model), BF16),
        compiler_params=_cparams(("parallel", "arbitrary")),
        name="merge_branches",
    )(o_a, o_b, o_c, w_a, w_b, w_c, p, p, p)


def _ffn_up_kernel(h_ref, wg_ref, wu_ref, cwg_ref, cwu_ref, cbg_ref, cbu_ref, o_ref, cg_ref, cu_ref, *,
                   tiles_per_seq, tm, ts):
    i = pl.program_id(1)
    first = (i % tiles_per_seq) == 0
    rows = lax.broadcasted_iota(I32, (ts, o_ref.shape[1]), 0)

    @pl.when(first)
    def _():
        cg_ref[...] = jnp.zeros_like(cg_ref)
        cu_ref[...] = jnp.zeros_like(cu_ref)

    def conv(u, prev, cw_ref, cb_ref):
        u1 = jnp.where(rows == 0, prev[7:8, :], pltpu.roll(u, 1, axis=0))
        u2 = pltpu.roll(u, 2, axis=0)
        u2 = jnp.where(rows == 0, prev[6:7, :], jnp.where(rows == 1, prev[7:8, :], u2))
        cw = cw_ref[...]
        return cw[0:1, :] * u2 + cw[1:2, :] * u1 + cw[2:3, :] * u + cb_ref[...]

    prev_g = cg_ref[...]
    prev_u = cu_ref[...]
    for sb in range(tm // ts):
        h = h_ref[sb * ts:(sb + 1) * ts, :]
        ug = _dot(h, wg_ref[...])
        uu = _dot(h, wu_ref[...])
        gate = conv(ug, prev_g, cwg_ref, cbg_ref)
        up = conv(uu, prev_u, cwu_ref, cbu_ref)
        o_ref[sb * ts:(sb + 1) * ts, :] = (gate * _sigmoid(gate) * up).astype(o_ref.dtype)
        prev_g = ug[ts - SUBLANES:ts, :]
        prev_u = uu[ts - SUBLANES:ts, :]
    cg_ref[...] = prev_g
    cu_ref[...] = prev_u


def ffn_up(h, w_up, conv_w, conv_b, seq, tm=1024, tn=256, ts=256):
    m, k = h.shape
    d_ff = w_up.shape[1] // 2
    tm = min(tm, seq)
    ts = min(ts, tm)
    nj = d_ff // tn
    return pl.pallas_call(
        functools.partial(_ffn_up_kernel, tiles_per_seq=seq // tm, tm=tm, ts=ts),
        grid=(nj, m // tm),
        in_specs=[
            pl.BlockSpec((tm, k), lambda j, i: (i, 0)),
            pl.BlockSpec((k, tn), lambda j, i: (0, j)),
            pl.BlockSpec((k, tn), lambda j, i: (0, nj + j)),
            pl.BlockSpec((CONV_W, tn), lambda j, i: (0, j)),
            pl.BlockSpec((CONV_W, tn), lambda j, i: (0, nj + j)),
            pl.BlockSpec((1, tn), lambda j, i: (0, j)),
            pl.BlockSpec((1, tn), lambda j, i: (0, nj + j)),
        ],
        out_specs=pl.BlockSpec((tm, tn), lambda j, i: (i, j)),
        out_shape=jax.ShapeDtypeStruct((m, d_ff), BF16),
        scratch_shapes=[pltpu.VMEM((SUBLANES, tn), F32), pltpu.VMEM((SUBLANES, tn), F32)],
        compiler_params=_cparams(("parallel", "arbitrary")),
        name="ffn_up",
    )(h, w_up, w_up, conv_w, conv_w, conv_b, conv_b)


A_SUB = 16


def _hgrn_kernel(q_ref, f_ref, i_ref, g_ref, lb_ref, gain_ref, o_ref, st_ref, *, n_chunks, heads):
    @pl.when(pl.program_id(2) == 0)
    def _():
        st_ref[...] = jnp.zeros_like(st_ref)

    gain = gain_ref[...]
    tri = (lax.broadcasted_iota(I32, (CHUNK, CHUNK), 0) >= lax.broadcasted_iota(I32, (CHUNK, CHUNK), 1)).astype(BF16)
    rows_sub = lax.broadcasted_iota(I32, (A_SUB, A_DK), 0)
    n_sub = CHUNK // A_SUB

    def one_head(hh, r0):
        cols = slice(hh * A_DK, (hh + 1) * A_DK)
        rows = pl.ds(r0, CHUNK)
        lb = lb_ref[:, cols]
        z = f_ref[rows, cols]
        qv = q_ref[rows, cols]
        iv = i_ref[rows, cols]
        gv = g_ref[rows, cols]
        lf = jnp.log(lb + (1.0 - lb) * _sigmoid(z))
        kin = (1.0 - lb) * _sigmoid(-z)
        qf = qv * _sigmoid(qv)
        b = _dot_exact_lhs(tri, lf)
        st = st_ref[hh]
        o_inter = _dot_nt((qf * jnp.exp(b)).astype(BF16), st.astype(BF16))
        iv_b = iv.astype(BF16)
        outs = []
        for si in range(n_sub):
            lo = si * A_SUB
            b_i = b[lo:lo + A_SUB]
            q_i = qf[lo:lo + A_SUB]
            k_i = kin[lo:lo + A_SUB]
            i_i = iv[lo:lo + A_SUB]
            o_i = o_inter[lo:lo + A_SUB]
            if si > 0:
                b_ref_row = b[lo - 1:lo]
                q_s = (q_i * jnp.exp(b_i - b_ref_row)).astype(BF16)
                k_s = (kin[0:lo] * jnp.exp(b_ref_row - b[0:lo])).astype(BF16)
                att = _dot_nt(q_s, k_s)
                o_i = o_i + _dot(att.astype(BF16), iv_b[0:lo])
            for s in range(A_SUB):
                d = jnp.exp(jnp.where(rows_sub >= s, b_i - b_i[s:s + 1], -jnp.inf))
                a = jnp.sum(q_i * d * k_i[s:s + 1], axis=1, keepdims=True)
                o_i = o_i + a * i_i[s:s + 1]
            outs.append(o_i)
        o = jnp.concatenate(outs, axis=0)
        b_last = b[CHUNK - 1:CHUNK]
        k_dec = (kin * jnp.exp(b_last - b)).astype(BF16)
        st_ref[hh] = st * jnp.exp(b_last) + _dot_tn(iv_b, k_dec)
        ms = jnp.mean(o * o, axis=-1, keepdims=True)
        on = o * lax.rsqrt(ms + NORM_EPS) * gain
        o_ref[rows, cols] = (on * (gv * _sigmoid(gv))).astype(o_ref.dtype)

    def chunk(c, carry):
        r0 = pl.multiple_of(c * CHUNK, CHUNK)
        for hh in range(heads):
            one_head(hh, r0)
        return carry

    lax.fori_loop(0, n_chunks, chunk, 0)


def hgrn2(p3, lb, out_gain, s_blk=256, heads=4):
    bsz, seq, _ = p3.shape
    s_blk = min(s_blk, seq)
    width = heads * A_DK
    groups = A_HEADS // heads
    c0 = COL_A // width

    def col(part):
        return pl.BlockSpec((None, s_blk, width), lambda b, h, s, part=part: (b, s, c0 + part * groups + h))

    return pl.pallas_call(
        functools.partial(_hgrn_kernel, n_chunks=s_blk // CHUNK, heads=heads),
        grid=(bsz, groups, seq // s_blk),
        in_specs=[col(0), col(1), col(2), col(3),
                  pl.BlockSpec((1, width), lambda b, h, s: (0, h)),
                  pl.BlockSpec((1, A_DV), lambda b, h, s: (0, 0))],
        out_specs=pl.BlockSpec((None, s_blk, width), lambda b, h, s: (b, s, h)),
        out_shape=jax.ShapeDtypeStruct((bsz, seq, A_W), BF16),
        scratch_shapes=[pltpu.VMEM((heads, A_DV, A_DK), F32)],
        compiler_params=_cparams(("parallel", "parallel", "arbitrary")),
        name="hgrn2",
    )(p3, p3, p3, p3, lb.reshape(1, A_HEADS * A_DK), out_gain.reshape(1, A_DV))


def _rwkv_prep_kernel(r_ref, k_ref, v_ref, l_ref, rp_ref, kp_ref, vp_ref, lp_ref,
                      mu_ref, w0_ref, w2_ref, a0_ref, a2_ref, g2_ref, kk_ref, ka_ref,
                      ro_ref, ld_ref, k2_ref, vo_ref, kko_ref, kka_ref, go_ref, *, tiles_per_seq, tm):
    first = (pl.program_id(0) % tiles_per_seq) == 0
    rows = lax.broadcasted_iota(I32, (tm, B_W), 0)

    def shifted(cur_ref, prev_ref, part):
        cur = cur_ref[...]
        prev = jnp.where(first, 0.0, prev_ref[...])[SUBLANES - 1:SUBLANES, :]
        sh = jnp.where(rows == 0, prev, pltpu.roll(cur, 1, axis=0))
        return cur + (sh - cur) * mu_ref[part:part + 1, :]

    r = shifted(r_ref, rp_ref, 0)
    k = shifted(k_ref, kp_ref, 1)
    v = shifted(v_ref, vp_ref, 2)
    lo = shifted(l_ref, lp_ref, 3)
    lo_a = lo[:, 0:LANES]
    lo_g = lo[:, LANES:3 * LANES]
    wpre = w0_ref[...] + _dot(jnp.tanh(lo_a).astype(BF16), w2_ref[...])
    y = -wpre
    softplus = jnp.maximum(y, 0.0) + jnp.log(1.0 + jnp.exp(-jnp.abs(y)))
    w_log = -softplus - 0.5
    ld_ref[...] = -jnp.exp(w_log)
    a = _sigmoid(a0_ref[...] + _dot(lo_a.astype(BF16), a2_ref[...]))
    go_ref[...] = _dot(_sigmoid(lo_g).astype(BF16), g2_ref[...])
    kk = k * kk_ref[...]
    bd = (lax.broadcasted_iota(I32, (LANES, LANES), 0) // B_HD
          == lax.broadcasted_iota(I32, (LANES, LANES), 1) // B_HD).astype(BF16)
    sq = kk * kk
    ss = jnp.concatenate([_dot_exact_rhs(sq[:, j * LANES:(j + 1) * LANES], bd) for j in range(B_W // LANES)], axis=1)
    kk = kk / jnp.maximum(jnp.sqrt(ss), 1e-12)
    ro_ref[...] = r
    vo_ref[...] = v
    kko_ref[...] = kk
    kka_ref[...] = kk * a
    k2_ref[...] = k * (1.0 + (a - 1.0) * ka_ref[...])


def rwkv_prep(p, seq, mu4, w0, w2p, a0, a2p, g2p, k_k, k_a, tm=256):
    m = p.shape[0]
    tm = min(tm, seq)
    cb = COL_B_RKV // B_W
    pb = tm // SUBLANES

    def cur(j):
        return pl.BlockSpec((tm, B_W), lambda i, j=j: (i, cb + j))

    def prev(j):
        return pl.BlockSpec((SUBLANES, B_W), lambda i, j=j: (jnp.maximum(i * pb - 1, 0), cb + j))

    def full(a):
        return pl.BlockSpec(a.shape, lambda i: (0, 0))

    row = lambda a: a.reshape(1, B_W)
    params = [mu4, row(w0), w2p, row(a0), a2p, g2p, row(k_k), row(k_a)]
    out = jax.ShapeDtypeStruct((m, B_W), F32)
    return pl.pallas_call(
        functools.partial(_rwkv_prep_kernel, tiles_per_seq=seq // tm, tm=tm),
        grid=(m // tm,),
        in_specs=[cur(0), cur(1), cur(2), cur(3), prev(0), prev(1), prev(2), prev(3)] + [full(a) for a in params],
        out_specs=[pl.BlockSpec((tm, B_W), lambda i: (i, 0))] * 7,
        out_shape=[out] * 7,
        compiler_params=_cparams(("parallel",)),
        name="rwkv_prep",
    )(p, p, p, p, p, p, p, p, *params)


B_T = 64


def _rwkv_kernel(r_ref, ld_ref, k_ref, v_ref, kk_ref, kka_ref, g_ref, rk_ref, lnw_ref, lnb_ref, o_ref, st_ref, *,
                 n_chunks, pairs):
    @pl.when(pl.program_id(2) == 0)
    def _():
        st_ref[...] = jnp.zeros_like(st_ref)

    t = B_T
    ii = lax.broadcasted_iota(I32, (t, t), 0)
    jj = lax.broadcasted_iota(I32, (t, t), 1)
    tri = (ii >= jj).astype(BF16)
    i2 = lax.broadcasted_iota(I32, (2 * t, 2 * t), 0)
    j2 = lax.broadcasted_iota(I32, (2 * t, 2 * t), 1)
    same = (i2 // t) == (j2 // t)
    strict_bd = same & ((i2 % t) > (j2 % t))
    incl_bd = same & ((i2 % t) >= (j2 % t))
    head_bd = same.astype(BF16)
    lane = lax.broadcasted_iota(I32, (t, LANES), 1)
    h0 = lane < B_HD

    def stack(x):
        return jnp.concatenate([jnp.where(h0, x, jnp.zeros_like(x)), jnp.where(h0, jnp.zeros_like(x), x)], axis=0)

    def head_sum(x):
        hi, lo = _split2(x)
        return _dot(hi, head_bd) + _dot(lo, head_bd)

    def chunk(c, carry):
        sl = pl.ds(pl.multiple_of(c * t, t), t)
        prs = range(pairs)
        cols = [slice(pp * LANES, (pp + 1) * LANES) for pp in prs]
        r = [r_ref[sl, cl] for cl in cols]
        ld = [ld_ref[sl, cl] for cl in cols]
        k = [k_ref[sl, cl] for cl in cols]
        v = [v_ref[sl, cl] for cl in cols]
        cs = []
        for pp in prs:
            ld_hi, ld_lo = _split2(ld[pp])
            cs.append(_dot(tri, ld_hi) + _dot(tri, ld_lo))
        lhs, rhs, vs = [], [], []
        for pp in prs:
            e_neg = jnp.exp(-cs[pp])
            kka = kka_ref[sl, cols[pp]]
            a_t = (-kk_ref[sl, cols[pp]] * jnp.exp(cs[pp] - ld[pp])).astype(BF16)
            r_t = (r[pp] * jnp.exp(cs[pp])).astype(BF16)
            b_t = (kka * e_neg).astype(BF16)
            k_t = (k[pp] * e_neg).astype(BF16)
            lhs.append(jnp.concatenate([stack(a_t), stack(r_t)], axis=0))
            rhs.append(jnp.concatenate([b_t, b_t, k_t, k_t], axis=0))
            vs.append(stack(v[pp].astype(BF16)))
        sc = [_dot_nt(lhs[pp], rhs[pp]) for pp in prs]
        st = [st_ref[pp] for pp in prs]
        proj = [_dot_nt(lhs[pp], st[pp].astype(BF16)) for pp in prs]
        n = [jnp.where(strict_bd, sc[pp][0:2 * t, 0:2 * t], 0.0).astype(BF16) for pp in prs]
        xs = [proj[pp][0:2 * t] + _dot(jnp.where(strict_bd, sc[pp][0:2 * t, 2 * t:4 * t], 0.0).astype(BF16), vs[pp])
              for pp in prs]
        for it in range(6):
            xs = [xs[pp] + _dot(n[pp], xs[pp].astype(BF16)) for pp in prs]
            if it < 5:
                n = [_dot(n[pp], n[pp]).astype(BF16) for pp in prs]
        os_ = []
        for pp in prs:
            m_r = jnp.concatenate([jnp.where(incl_bd, sc[pp][2 * t:4 * t, 0:2 * t], 0.0),
                                   jnp.where(incl_bd, sc[pp][2 * t:4 * t, 2 * t:4 * t], 0.0)], axis=1)
            uv = jnp.concatenate([xs[pp].astype(BF16), vs[pp]], axis=0)
            os_.append(proj[pp][2 * t:4 * t] + _dot(m_r.astype(BF16), uv))
        upd = []
        for pp in prs:
            u = xs[pp][0:t] + xs[pp][t:2 * t]
            c_last = cs[pp][t - 1:t]
            dec = jnp.exp(c_last - cs[pp])
            upd.append(_dot_tn(jnp.concatenate([u, v[pp]], axis=0).astype(BF16),
                               jnp.concatenate([kka_ref[sl, cols[pp]] * dec, k[pp] * dec], axis=0).astype(BF16)))
        for pp in prs:
            st_ref[pp] = st[pp] * jnp.exp(cs[pp][t - 1:t]) + jnp.where(same, upd[pp], 0.0)
        inv = 1.0 / B_HD
        o = [os_[pp][0:t] + os_[pp][t:2 * t] for pp in prs]
        mean = [head_sum(o[pp]) * inv for pp in prs]
        d = [o[pp] - mean[pp] for pp in prs]
        var = [head_sum(d[pp] * d[pp]) * inv for pp in prs]
        bonus = [head_sum(r[pp] * k[pp] * rk_ref[:, cols[pp]]) for pp in prs]
        for pp in prs:
            on = d[pp] * lax.rsqrt(var[pp] + B_GN_EPS) * lnw_ref[:, cols[pp]] + lnb_ref[:, cols[pp]]
            o_ref[sl, cols[pp]] = ((on + bonus[pp] * v[pp]) * g_ref[sl, cols[pp]]).astype(o_ref.dtype)
        return carry

    lax.fori_loop(0, n_chunks, chunk, 0)


def rwkv_recurrence(r, ld, k2, v, kk, kka, g, r_k, ln_w, ln_b, bsz, seq, s_blk=256, pairs=4):
    s_blk = min(s_blk, seq)
    width = pairs * LANES
    groups = B_W // width
    args = [a.reshape(bsz, seq, B_W) for a in (r, ld, k2, v, kk, kka, g)]
    blk = pl.BlockSpec((None, s_blk, width), lambda b, h, s: (b, s, h))
    par = pl.BlockSpec((1, width), lambda b, h, s: (0, h))
    return pl.pallas_call(
        functools.partial(_rwkv_kernel, n_chunks=s_blk // B_T, pairs=pairs),
        grid=(bsz, groups, seq // s_blk),
        in_specs=[blk] * 7 + [par] * 3,
        out_specs=blk,
        out_shape=jax.ShapeDtypeStruct((bsz, seq, B_W), BF16),
        scratch_shapes=[pltpu.VMEM((pairs, LANES, LANES), F32)],
        compiler_params=_cparams(("parallel", "parallel", "arbitrary")),
        name="rwkv_recurrence",
    )(*args, r_k.reshape(1, B_W), ln_w.reshape(1, B_W), ln_b.reshape(1, B_W))


def _rope(x, cos_t, sin_t, lane_in_head, half):
    partner = jnp.where(lane_in_head < half, pltpu.roll(x, LANES - half, axis=1), pltpu.roll(x, half, axis=1))
    return x * cos_t + partner * sin_t


def _attn_prep_kernel(q_ref, k_ref, v_ref, qi_ref, kw_ref, ca_ref, sa_ref, ci_ref, si_ref, qg_ref, kg_ref,
                      qo_ref, ko_ref, vo_ref, qio_ref, kio_ref, wo_ref, *, tm):
    ca, sa, ci, si = ca_ref[...], sa_ref[...], ci_ref[...], si_ref[...]
    lane = lax.broadcasted_iota(I32, (tm, LANES), 1)
    lane_i = lane % IDX_HD
    half_a = C_HD // ROPE_FRACTION // 2
    half_i = IDX_HD // ROPE_FRACTION // 2
    scale = C_HD ** -0.5 * LOG2_E

    def norm_rope(x, gain):
        ms = jnp.mean(x * x, axis=-1, keepdims=True)
        return _rope(x * lax.rsqrt(ms + NORM_EPS) * gain, ca, sa, lane, half_a)

    for h in range(C_HEADS):
        sl = slice(h * C_HD, (h + 1) * C_HD)
        qo_ref[:, sl] = (norm_rope(q_ref[:, sl], qg_ref[...]) * scale).astype(qo_ref.dtype)
    for h in range(C_KV_HEADS):
        sl = slice(h * C_HD, (h + 1) * C_HD)
        ko_ref[:, sl] = norm_rope(k_ref[:, sl], kg_ref[...]).astype(ko_ref.dtype)
    vo_ref[...] = v_ref[...].astype(vo_ref.dtype)
    for j in range(IDX_HEADS * IDX_HD // LANES):
        sl = slice(j * LANES, (j + 1) * LANES)
        qio_ref[:, sl] = _rope(qi_ref[:, sl], ci, si, lane_i, half_i).astype(qio_ref.dtype)
    kw = kw_ref[:, 0:LANES]
    kr = _rope(kw, ci, si, lane_i, half_i)
    kio_ref[...] = jnp.where(lane < IDX_HD, kr, pltpu.roll(kr, IDX_HD, axis=1)).astype(kio_ref.dtype)
    w = pltpu.roll(kw, LANES - IDX_HD, axis=1) * (IDX_HEADS ** -0.5 * IDX_HD ** -0.5)
    wo_ref[...] = jnp.where(lane < IDX_HEADS, w, 0.0)


def attn_prep(p, tabs, q_gain, k_gain, tm=256):
    m = p.shape[0]
    tm = min(tm, m)

    def colblk(width, off):
        return pl.BlockSpec((tm, width), lambda i: (i, off // width))

    tab = pl.BlockSpec((tm, LANES), lambda i: (i, 0))
    gain = pl.BlockSpec((1, C_HD), lambda i: (0, 0))
    kw_width = 512

    def out(width, dtype):
        return jax.ShapeDtypeStruct((m, width), dtype), pl.BlockSpec((tm, width), lambda i: (i, 0))

    outs = [out(C_W, BF16), out(C_KV_W, BF16), out(C_KV_W, BF16), out(IDX_HEADS * IDX_HD, BF16),
            out(LANES, BF16), out(LANES, F32)]
    return pl.pallas_call(
        functools.partial(_attn_prep_kernel, tm=tm),
        grid=(m // tm,),
        in_specs=[colblk(C_W, COL_C_Q), colblk(C_KV_W, COL_C_K), colblk(C_KV_W, COL_C_V),
                  colblk(IDX_HEADS * IDX_HD, COL_C_QI), colblk(kw_width, COL_C_KW), tab, tab, tab, tab, gain, gain],
        out_specs=[o[1] for o in outs],
        out_shape=[o[0] for o in outs],
        compiler_params=_cparams(("parallel",)),
        name="attn_prep",
    )(p, p, p, p, p, *tabs, q_gain.reshape(1, C_HD), k_gain.reshape(1, C_HD))


V_ONES = 16


def _attn_kernel(q_ref, k_ref, vt_ref, qi_ref, ki_ref, w_ref, o_ref, key_sc, bias_sc, m_sc, acc_sc, *, tq, tk, topk):
    qb = pl.program_id(1)
    nkb = ((qb + 1) * tq + tk - 1) // tk
    lane = lax.broadcasted_iota(I32, (tq, LANES), 1)
    w_t = w_ref[...].T
    q_chunk = (qb * tq + lax.broadcasted_iota(I32, (tk, tq), 1)) // CHUNK
    key_in_blk = lax.broadcasted_iota(I32, (tk, tq), 0)
    n_pairs = IDX_HEADS * IDX_HD // LANES
    fold = 64

    q_pairs = []
    for hp in range(n_pairs):
        qp = qi_ref[:, hp * LANES:(hp + 1) * LANES]
        zero = jnp.zeros_like(qp)
        q_pairs.append(jnp.concatenate([jnp.where(lane < IDX_HD, qp, zero), jnp.where(lane < IDX_HD, zero, qp)], axis=0))

    def score_block(kb, carry):
        c0 = pl.multiple_of(kb * tk, tk)
        ki2 = ki_ref[pl.ds(c0, tk), :]
        sc = jnp.zeros((tk, tq), F32)
        for hp in range(n_pairs):
            rel = jnp.maximum(_dot_nt(ki2, q_pairs[hp]), 0.0)
            sc = sc + w_t[2 * hp:2 * hp + 1, :] * rel[:, 0:tq] + w_t[2 * hp + 1:2 * hp + 2, :] * rel[:, tq:2 * tq]
        sc = jnp.where(sc == 0.0, 0.0, sc)
        bits = lax.bitcast_convert_type(sc, I32)
        skey = bits ^ ((bits >> 31) & 0x7FFFFFFF)
        allowed = ((c0 + key_in_blk) // CHUNK) <= q_chunk
        key_sc[kb] = jnp.where(allowed, skey, INT_MIN)
        return carry

    lax.fori_loop(0, nkb, score_block, 0)

    def count_ge(cand):
        def body(kb, acc):
            hit = jnp.where(key_sc[kb] >= cand, 1.0, 0.0)
            for j in range(tk // fold):
                acc = acc + hit[j * fold:(j + 1) * fold]
            return acc
        acc = lax.fori_loop(0, nkb, body, jnp.zeros((fold, tq), F32))
        return jnp.sum(acc, axis=0, keepdims=True)

    kf = float(topk)
    thr = jnp.where(count_ge(jnp.zeros((1, tq), I32)) >= kf, 0, INT_MIN).astype(I32)

    def bit_step(i, thr):
        cand = thr | (jnp.int32(1) << (30 - i))
        return jnp.where(count_ge(cand) >= kf, cand, thr)

    thr = lax.fori_loop(0, 31, bit_step, thr)
    thr = jnp.maximum(thr, INT_MIN + 1)
    need = kf - count_ge(thr + 1)

    tri = (lax.broadcasted_iota(I32, (tk, tk), 0) >= lax.broadcasted_iota(I32, (tk, tk), 1)).astype(BF16)

    def bias_block(kb, seen):
        keys = key_sc[kb]
        eq = keys == thr
        eq_f = jnp.where(eq, 1.0, 0.0)
        rank = _dot(tri, eq_f.astype(BF16)) + seen
        take = (keys > thr) | (eq & (rank <= need))
        bias_sc[kb] = jnp.where(take, 0.0, NEG_BIG)
        return rank[tk - 1:tk]

    lax.fori_loop(0, nkb, bias_block, jnp.zeros((1, tq), F32))

    rep = C_HEADS // C_KV_HEADS
    ve = C_HD + V_ONES
    qs = [jnp.concatenate([q_ref[:, (g * rep + r) * C_HD:(g * rep + r + 1) * C_HD] for r in range(rep)], axis=0)
          for g in range(C_KV_HEADS)]
    m_sc[...] = jnp.full(m_sc.shape, NEG_BIG, F32)
    acc_sc[...] = jnp.zeros(acc_sc.shape, F32)

    def body(kb, carry):
        c0 = pl.multiple_of(kb * tk, tk)
        bias = bias_sc[kb]
        s_all = [_dot_nt(k_ref[pl.ds(c0, tk), g * C_HD:(g + 1) * C_HD], qs[g]) for g in range(C_KV_HEADS)]
        for g in range(C_KV_HEADS):
            s = jnp.concatenate([s_all[g][:, r * tq:(r + 1) * tq] + bias for r in range(rep)], axis=1)
            m_prev = m_sc[g]
            m_new = jnp.maximum(m_prev, jnp.max(s, axis=0, keepdims=True))
            p = jnp.exp2(s - m_new)
            acc_sc[g] = jnp.exp2(m_prev - m_new) * acc_sc[g] + _dot(vt_ref[kb, g * ve:(g + 1) * ve, :], p.astype(BF16))
            m_sc[g] = m_new
        return carry

    lax.fori_loop(0, nkb, body, 0)
    for g in range(C_KV_HEADS):
        acc = acc_sc[g]
        out_t = acc[0:C_HD] / acc[C_HD:C_HD + 1]
        for r in range(rep):
            h = g * rep + r
            o_ref[:, h * C_HD:(h + 1) * C_HD] = out_t[:, r * tq:(r + 1) * tq].T.astype(o_ref.dtype)


def sparse_attention(qn, kn, vn, qi, ki2, w, bsz, seq, tq=128, tk=512):
    tk = min(tk, seq)
    topk = min(TOPK_MAX, seq // 4)
    nkb = seq // tk
    rep = C_HEADS // C_KV_HEADS
    ve = C_HD + V_ONES
    r3 = lambda a: a.reshape(bsz, seq, a.shape[-1])
    v_t = vn.reshape(bsz, nkb, tk, C_KV_HEADS, C_HD).transpose(0, 1, 3, 4, 2)
    v_t = jnp.concatenate([v_t, jnp.ones((bsz, nkb, C_KV_HEADS, V_ONES, tk), v_t.dtype)], axis=3)
    v_t = v_t.reshape(bsz, nkb, C_KV_HEADS * ve, tk)
    qblk = lambda width: pl.BlockSpec((None, tq, width), lambda b, i: (b, i, 0))
    sblk = lambda width: pl.BlockSpec((None, seq, width), lambda b, i: (b, 0, 0))
    return pl.pallas_call(
        functools.partial(_attn_kernel, tq=tq, tk=tk, topk=topk),
        grid=(bsz, seq // tq),
        in_specs=[qblk(C_W), sblk(C_KV_W), pl.BlockSpec((None, nkb, C_KV_HEADS * ve, tk), lambda b, i: (b, 0, 0, 0)),
                  qblk(IDX_HEADS * IDX_HD), sblk(LANES), qblk(LANES)],
        out_specs=qblk(C_W),
        out_shape=jax.ShapeDtypeStruct((bsz, seq, C_W), BF16),
        scratch_shapes=[pltpu.VMEM((nkb, tk, tq), I32), pltpu.VMEM((nkb, tk, tq), F32),
                        pltpu.VMEM((C_KV_HEADS, 1, rep * tq), F32), pltpu.VMEM((C_KV_HEADS, ve, rep * tq), F32)],
        compiler_params=_cparams(("parallel", "arbitrary")),
        name="sparse_attention",
    )(r3(qn), r3(kn), v_t, r3(qi), r3(ki2), r3(w))


def _pack_w_in(w):
    d = w.shape[0]
    a_cols = 4 * A_W
    b_cols = 3 * B_W + B_LORA_DECAY + B_LORA_A + B_LORA_G
    c_cols = C_W + 2 * C_KV_W + IDX_HEADS * IDX_HD + IDX_HD + IDX_HEADS
    e_b = a_cols + b_cols
    e_c = e_b + c_cols
    z = lambda n: jnp.zeros((d, n), w.dtype)
    packed = jnp.concatenate([w[:, :e_b], z(COL_C_Q - e_b), w[:, e_b:e_c], z(COL_G - COL_C_Q - c_cols), w[:, e_c:]],
                             axis=1)
    return packed.astype(BF16)


def _rope_tables(positions):
    pos = positions.astype(F32).reshape(-1, 1)

    def tables(head_dim):
        rot = head_dim // ROPE_FRACTION
        half = rot // 2
        inv_freq = ROPE_THETA ** (-jnp.arange(0, rot, 2, dtype=F32) / rot)
        ang = pos * inv_freq
        cos, sin = jnp.cos(ang), jnp.sin(ang)
        ones = jnp.ones((pos.shape[0], head_dim - rot), F32)
        c = jnp.concatenate([cos, cos, ones], axis=1)
        s = jnp.concatenate([-sin, sin, 0.0 * ones], axis=1)
        reps = LANES // head_dim
        return jnp.tile(c, (1, reps)), jnp.tile(s, (1, reps))

    ca, sa = tables(C_HD)
    ci, si = tables(IDX_HD)
    return ca, sa, ci, si


def _pad_rows(a, before, total):
    return jnp.zeros((total, a.shape[1]), a.dtype).at[before:before + a.shape[0]].set(a)


def kernel(x, positions, norm_mix, w_in, hgrn_lb_logits, hgrn_out_norm, rwkv_mu, rwkv_w0, rwkv_w2, rwkv_a0, rwkv_a2,
           rwkv_g2, rwkv_k_k, rwkv_k_a, rwkv_r_k, rwkv_ln_w, rwkv_ln_b, q_norm, k_norm, w_branch_a, w_branch_b,
           w_branch_c, w_out, norm_ffn, w_up, conv_w, conv_b, w_down):
    bsz, seq, d_model = x.shape
    depth = w_in.shape[0]
    m = bsz * seq
    tabs = _rope_tables(positions)
    lb_all = jnp.cumsum(jax.nn.softmax(hgrn_lb_logits.astype(F32), axis=0), axis=0)
    lb_all = lb_all - lb_all[0:1]
    n_lora = B_LORA_DECAY + B_LORA_A + B_LORA_G

    xf = x.reshape(m, d_model)
    for l in range(depth):
        h = rmsnorm(xf, norm_mix[l])
        p = proj_in(h, _pack_w_in(w_in[l]), COL_G)

        o_a = hgrn2(p.reshape(bsz, seq, -1), lb_all[l], hgrn_out_norm[l]).reshape(m, A_W)

        mu = rwkv_mu[l]
        mu4 = jnp.stack([mu[0:B_W], mu[B_W:2 * B_W], mu[2 * B_W:3 * B_W],
                         jnp.pad(mu[3 * B_W:], (0, B_W - n_lora))])
        w2p = _pad_rows(rwkv_w2[l], 0, LANES).astype(BF16)
        a2p = _pad_rows(rwkv_a2[l], B_LORA_DECAY, LANES).astype(BF16)
        g2p = _pad_rows(rwkv_g2[l], 0, 2 * LANES).astype(BF16)
        parts = rwkv_prep(p, seq, mu4, rwkv_w0[l], w2p, rwkv_a0[l], a2p, g2p, rwkv_k_k[l], rwkv_k_a[l])
        o_b = rwkv_recurrence(*parts, rwkv_r_k[l], rwkv_ln_w[l], rwkv_ln_b[l], bsz, seq).reshape(m, B_W)

        qn, kn, vn, qi, ki2, wi = attn_prep(p, tabs, q_norm[l], k_norm[l])
        o_c = sparse_attention(qn, kn, vn, qi, ki2, wi, bsz, seq).reshape(m, C_W)

        merged = merge_branches(o_a, o_b, o_c, w_branch_a[l].astype(BF16), w_branch_b[l].astype(BF16),
                                w_branch_c[l].astype(BF16), p, d_model)
        xf = matmul_residual(merged, w_out[l].astype(BF16), xf)

        h2 = rmsnorm(xf, norm_ffn[l])
        act = ffn_up(h2, w_up[l].astype(BF16), conv_w[l], conv_b[l].reshape(1, -1), seq)
        xf = matmul_residual(act, w_down[l].astype(BF16), xf, tm=512, tn=256)
    return xf.reshape(bsz, seq, d_model)
```

```python
import functools

import jax
import jax.numpy as jnp
import numpy as np
from jax import lax
from jax.experimental import pallas as pl
from jax.experimental.pallas import tpu as pltpu

F32 = jnp.float32
BF16 = jnp.bfloat16
I32 = jnp.int32

CHUNK = 64
ROPE_THETA = 500000.0
ROPE_FRACTION = 4
NORM_EPS = 1e-6
A_HEADS, A_DK, A_DV = 8, 128, 128
A_W = A_HEADS * A_DV
B_HEADS, B_HD = 16, 64
B_W = B_HEADS * B_HD
B_LORA_DECAY, B_LORA_A, B_LORA_G = 64, 64, 160
B_GN_EPS = 64e-5
C_HEADS, C_KV_HEADS, C_HD = 16, 4, 128
C_W = C_HEADS * C_HD
C_KV_W = C_KV_HEADS * C_HD
IDX_HEADS, IDX_HD = 16, 64
TOPK_MAX = 256
CONV_W = 3

LANES = 128
SUBLANES = 8
VMEM_LIMIT = 56 * 1024 * 1024

COL_A = 0
COL_B_RKV = 4096
COL_B_LORA = 7168
COL_C_Q = 8192
COL_C_K = 10240
COL_C_V = 10752
COL_C_QI = 11264
COL_C_KW = 12288
COL_G = 12800
LOG2_E = 1.4426950408889634
NEG_BIG = -1e30
INT_MIN = -2147483648


def _sigmoid(x):
    return 1.0 / (1.0 + jnp.exp(-x))


def _dot(a, b):
    return jnp.dot(a, b, preferred_element_type=F32)


def _dot_nt(a, b):
    return lax.dot_general(a, b, (((1,), (1,)), ((), ())), preferred_element_type=F32)


def _dot_tn(a, b):
    return lax.dot_general(a, b, (((0,), (0,)), ((), ())), preferred_element_type=F32)


def _split3(x):
    hi = x.astype(BF16)
    r1 = x - hi.astype(F32)
    mid = r1.astype(BF16)
    lo = (r1 - mid.astype(F32)).astype(BF16)
    return hi, mid, lo


def _dot_exact_lhs(m_bf16, x):
    hi, mid, lo = _split3(x)
    return _dot(m_bf16, hi) + _dot(m_bf16, mid) + _dot(m_bf16, lo)


def _dot_exact_rhs(x, m_bf16):
    hi, mid, lo = _split3(x)
    return _dot(hi, m_bf16) + _dot(mid, m_bf16) + _dot(lo, m_bf16)


def _split2(x):
    hi = x.astype(BF16)
    lo = (x - hi.astype(F32)).astype(BF16)
    return hi, lo


def _dot_hp(a, b):
    ah, al = _split2(a)
    bh, bl = _split2(b)
    return _dot(ah, bh) + _dot(ah, bl) + _dot(al, bh)


def _dot_nt_hp(a, b):
    ah, al = _split2(a)
    bh, bl = _split2(b)
    return _dot_nt(ah, bh) + _dot_nt(ah, bl) + _dot_nt(al, bh)


def _dot_tn_hp(a, b):
    ah, al = _split2(a)
    bh, bl = _split2(b)
    return _dot_tn(ah, bh) + _dot_tn(ah, bl) + _dot_tn(al, bh)


def _cparams(sem):
    return pltpu.CompilerParams(dimension_semantics=sem, vmem_limit_bytes=VMEM_LIMIT)


def _rmsnorm_kernel(x_ref, g_ref, o_ref):
    x = x_ref[...]
    ms = jnp.mean(x * x, axis=-1, keepdims=True)
    o_ref[...] = (x * lax.rsqrt(ms + NORM_EPS) * g_ref[...]).astype(o_ref.dtype)


def rmsnorm(x, gain, tm=256):
    m, d = x.shape
    return pl.pallas_call(
        _rmsnorm_kernel,
        grid=(m // tm,),
        in_specs=[pl.BlockSpec((tm, d), lambda i: (i, 0)), pl.BlockSpec((1, d), lambda i: (0, 0))],
        out_specs=pl.BlockSpec((tm, d), lambda i: (i, 0)),
        out_shape=jax.ShapeDtypeStruct((m, d), BF16),
        compiler_params=_cparams(("parallel",)),
        name="rmsnorm",
    )(x, gain.reshape(1, d))


def _wspec(w, layer, tn, jmap):
    return pl.BlockSpec((None, w.shape[1], tn), lambda *g: (layer, 0, jmap(*g)))


def _proj_in_kernel(h_ref, w_ref, o_ref):
    o_ref[...] = _dot(h_ref[...], w_ref[...])


def proj_in(h, w, layer, tm=1024, tn=512):
    m, k = h.shape
    n = w.shape[2]
    tm = min(tm, m)
    return pl.pallas_call(
        _proj_in_kernel,
        grid=(m // tm, n // tn),
        in_specs=[pl.BlockSpec((tm, k), lambda i, j: (i, 0)), _wspec(w, layer, tn, lambda i, j: j)],
        out_specs=pl.BlockSpec((tm, tn), lambda i, j: (i, j)),
        out_shape=jax.ShapeDtypeStruct((m, n), F32),
        compiler_params=_cparams(("parallel", "arbitrary")),
        name="proj_in",
    )(h, w)


def _mm_res_kernel(a_ref, b_ref, r_ref, o_ref):
    o_ref[...] = r_ref[...] + _dot(a_ref[...], b_ref[...])


def matmul_residual(a, b, layer, res, tm=512, tn=512):
    m, k = a.shape
    n = b.shape[2]
    tm = min(tm, m)
    tn = min(tn, n)
    return pl.pallas_call(
        _mm_res_kernel,
        grid=(m // tm, n // tn),
        in_specs=[
            pl.BlockSpec((tm, k), lambda i, j: (i, 0)),
            _wspec(b, layer, tn, lambda i, j: j),
            pl.BlockSpec((tm, tn), lambda i, j: (i, j)),
        ],
        out_specs=pl.BlockSpec((tm, tn), lambda i, j: (i, j)),
        out_shape=jax.ShapeDtypeStruct((m, n), F32),
        compiler_params=_cparams(("parallel", "arbitrary")),
        name="matmul_residual",
    )(a, b, res)


def _merge_kernel(oa_ref, ob_ref, oc_ref, wa_ref, wb_ref, wc_ref, ga_ref, gb_ref, gc_ref, o_ref):
    acc = _sigmoid(ga_ref[...]) * _dot(oa_ref[...], wa_ref[...])
    acc += _sigmoid(gb_ref[...]) * _dot(ob_ref[...], wb_ref[...])
    acc += _sigmoid(gc_ref[...]) * _dot(oc_ref[...], wc_ref[...])
    o_ref[...] = acc.astype(o_ref.dtype)


def merge_branches(o_a, o_b, o_c, w_a, w_b, w_c, layer, p, d_model, tm=512, tn=512):
    m = o_a.shape[0]
    tm = min(tm, m)
    g0 = COL_G // tn
    gstep = d_model // tn
    return pl.pallas_call(
        _merge_kernel,
        grid=(m // tm, d_model // tn),
        in_specs=[
            pl.BlockSpec((tm, o_a.shape[1]), lambda i, j: (i, 0)),
            pl.BlockSpec((tm, o_b.shape[1]), lambda i, j: (i, 0)),
            pl.BlockSpec((tm, o_c.shape[1]), lambda i, j: (i, 0)),
            _wspec(w_a, layer, tn, lambda i, j: j),
            _wspec(w_b, layer, tn, lambda i, j: j),
            _wspec(w_c, layer, tn, lambda i, j: j),
            pl.BlockSpec((tm, tn), lambda i, j: (i, g0 + j)),
            pl.BlockSpec((tm, tn), lambda i, j: (i, g0 + gstep + j)),
            pl.BlockSpec((tm, tn), lambda i, j: (i, g0 + 2 * gstep + j)),
        ],
        out_specs=pl.BlockSpec((tm, tn), lambda i, j: (i, j)),
        out_shape=jax.ShapeDtypeStruct((m, d_model), BF16),
        compiler_params=_cparams(("parallel", "arbitrary")),
        name="merge_branches",
    )(o_a, o_b, o_c, w_a, w_b, w_c, p, p, p)


def _ffn_up_kernel(h_ref, wg_ref, wu_ref, cwg_ref, cwu_ref, cbg_ref, cbu_ref, o_ref, w_sc, cg_ref, cu_ref, *,
                   tiles_per_seq, tm, ts):
    i = pl.program_id(1)
    first = (i % tiles_per_seq) == 0
    tn = o_ref.shape[1]
    rows = lax.broadcasted_iota(I32, (ts, tn), 0)

    @pl.when(i == 0)
    def _():
        w_sc[:, 0:tn] = wg_ref[...].astype(BF16)
        w_sc[:, tn:2 * tn] = wu_ref[...].astype(BF16)

    @pl.when(first)
    def _():
        cg_ref[...] = jnp.zeros_like(cg_ref)
        cu_ref[...] = jnp.zeros_like(cu_ref)

    def conv(u, prev, cw_ref, cb_ref):
        u1 = jnp.where(rows == 0, prev[7:8, :], pltpu.roll(u, 1, axis=0))
        u2 = pltpu.roll(u, 2, axis=0)
        u2 = jnp.where(rows == 0, prev[6:7, :], jnp.where(rows == 1, prev[7:8, :], u2))
        cw = cw_ref[...]
        return cw[0:1, :] * u2 + cw[1:2, :] * u1 + cw[2:3, :] * u + cb_ref[...]

    prev_g = cg_ref[...]
    prev_u = cu_ref[...]
    for sb in range(tm // ts):
        u = _dot(h_ref[sb * ts:(sb + 1) * ts, :], w_sc[...])
        ug = u[:, 0:tn]
        uu = u[:, tn:2 * tn]
        gate = conv(ug, prev_g, cwg_ref, cbg_ref)
        up = conv(uu, prev_u, cwu_ref, cbu_ref)
        o_ref[sb * ts:(sb + 1) * ts, :] = (gate * _sigmoid(gate) * up).astype(o_ref.dtype)
        prev_g = ug[ts - SUBLANES:ts, :]
        prev_u = uu[ts - SUBLANES:ts, :]
    cg_ref[...] = prev_g
    cu_ref[...] = prev_u


def ffn_up(h, w_up, layer, conv_w, conv_b, seq, tm=1024, tn=256, ts=256):
    m, k = h.shape
    d_ff = w_up.shape[2] // 2
    tm = min(tm, seq)
    ts = min(ts, tm)
    nj = d_ff // tn
    return pl.pallas_call(
        functools.partial(_ffn_up_kernel, tiles_per_seq=seq // tm, tm=tm, ts=ts),
        grid=(nj, m // tm),
        in_specs=[
            pl.BlockSpec((tm, k), lambda j, i: (i, 0)),
            _wspec(w_up, layer, tn, lambda j, i: j),
            _wspec(w_up, layer, tn, lambda j, i: nj + j),
            pl.BlockSpec((CONV_W, tn), lambda j, i: (0, j)),
            pl.BlockSpec((CONV_W, tn), lambda j, i: (0, nj + j)),
            pl.BlockSpec((1, tn), lambda j, i: (0, j)),
            pl.BlockSpec((1, tn), lambda j, i: (0, nj + j)),
        ],
        out_specs=pl.BlockSpec((tm, tn), lambda j, i: (i, j)),
        out_shape=jax.ShapeDtypeStruct((m, d_ff), BF16),
        scratch_shapes=[pltpu.VMEM((k, 2 * tn), BF16), pltpu.VMEM((SUBLANES, tn), F32),
                        pltpu.VMEM((SUBLANES, tn), F32)],
        compiler_params=_cparams(("parallel", "arbitrary")),
        name="ffn_up",
    )(h, w_up, w_up, conv_w, conv_w, conv_b, conv_b)


A_SUB = 16


def _hgrn_kernel(q_ref, f_ref, i_ref, g_ref, lb_ref, gain_ref, o_ref, st_ref, *, n_chunks, heads):
    @pl.when(pl.program_id(2) == 0)
    def _():
        st_ref[...] = jnp.zeros_like(st_ref)

    gain = gain_ref[...]
    tri = (lax.broadcasted_iota(I32, (CHUNK, CHUNK), 0) >= lax.broadcasted_iota(I32, (CHUNK, CHUNK), 1)).astype(BF16)
    rows_sub = lax.broadcasted_iota(I32, (A_SUB, A_DK), 0)
    n_sub = CHUNK // A_SUB

    def one_head(hh, r0):
        cols = slice(hh * A_DK, (hh + 1) * A_DK)
        rows = pl.ds(r0, CHUNK)
        lb = lb_ref[:, cols]
        z = f_ref[rows, cols]
        qv = q_ref[rows, cols]
        iv = i_ref[rows, cols]
        gv = g_ref[rows, cols]
        lf = jnp.log(lb + (1.0 - lb) * _sigmoid(z))
        kin = (1.0 - lb) * _sigmoid(-z)
        qf = qv * _sigmoid(qv)
        b = _dot_exact_lhs(tri, lf)
        st = st_ref[hh]
        o_inter = _dot_nt((qf * jnp.exp(b)).astype(BF16), st.astype(BF16))
        iv_b = iv.astype(BF16)
        outs = []
        for si in range(n_sub):
            lo = si * A_SUB
            b_i = b[lo:lo + A_SUB]
            q_i = qf[lo:lo + A_SUB]
            k_i = kin[lo:lo + A_SUB]
            i_i = iv[lo:lo + A_SUB]
            o_i = o_inter[lo:lo + A_SUB]
            if si > 0:
                b_ref_row = b[lo - 1:lo]
                q_s = (q_i * jnp.exp(b_i - b_ref_row)).astype(BF16)
                k_s = (kin[0:lo] * jnp.exp(b_ref_row - b[0:lo])).astype(BF16)
                att = _dot_nt(q_s, k_s)
                o_i = o_i + _dot(att.astype(BF16), iv_b[0:lo])
            for s in range(A_SUB):
                d = jnp.exp(jnp.where(rows_sub >= s, b_i - b_i[s:s + 1], -jnp.inf))
                a = jnp.sum(q_i * d * k_i[s:s + 1], axis=1, keepdims=True)
                o_i = o_i + a * i_i[s:s + 1]
            outs.append(o_i)
        o = jnp.concatenate(outs, axis=0)
        b_last = b[CHUNK - 1:CHUNK]
        k_dec = (kin * jnp.exp(b_last - b)).astype(BF16)
        st_ref[hh] = st * jnp.exp(b_last) + _dot_tn(iv_b, k_dec)
        ms = jnp.mean(o * o, axis=-1, keepdims=True)
        on = o * lax.rsqrt(ms + NORM_EPS) * gain
        o_ref[rows, cols] = (on * (gv * _sigmoid(gv))).astype(o_ref.dtype)

    def chunk(c, carry):
        r0 = pl.multiple_of(c * CHUNK, CHUNK)
        for hh in range(heads):
            one_head(hh, r0)
        return carry

    lax.fori_loop(0, n_chunks, chunk, 0)


def hgrn2(p3, lb, out_gain, s_blk=256, heads=4):
    bsz, seq, _ = p3.shape
    s_blk = min(s_blk, seq)
    width = heads * A_DK
    groups = A_HEADS // heads
    c0 = COL_A // width

    def col(part):
        return pl.BlockSpec((None, s_blk, width), lambda b, h, s, part=part: (b, s, c0 + part * groups + h))

    return pl.pallas_call(
        functools.partial(_hgrn_kernel, n_chunks=s_blk // CHUNK, heads=heads),
        grid=(bsz, groups, seq // s_blk),
        in_specs=[col(0), col(1), col(2), col(3),
                  pl.BlockSpec((1, width), lambda b, h, s: (0, h)),
                  pl.BlockSpec((1, A_DV), lambda b, h, s: (0, 0))],
        out_specs=pl.BlockSpec((None, s_blk, width), lambda b, h, s: (b, s, h)),
        out_shape=jax.ShapeDtypeStruct((bsz, seq, A_W), BF16),
        scratch_shapes=[pltpu.VMEM((heads, A_DV, A_DK), F32)],
        compiler_params=_cparams(("parallel", "parallel", "arbitrary")),
        name="hgrn2",
    )(p3, p3, p3, p3, lb.reshape(1, A_HEADS * A_DK), out_gain.reshape(1, A_DV))


def _rwkv_prep_kernel(r_ref, k_ref, v_ref, l_ref, rp_ref, kp_ref, vp_ref, lp_ref,
                      mu_ref, w0_ref, w2_ref, a0_ref, a2_ref, g2_ref, kk_ref, ka_ref,
                      ro_ref, ld_ref, k2_ref, vo_ref, kko_ref, kka_ref, go_ref, *, tiles_per_seq, tm):
    first = (pl.program_id(0) % tiles_per_seq) == 0
    rows = lax.broadcasted_iota(I32, (tm, B_W), 0)

    def shifted(cur_ref, prev_ref, part):
        cur = cur_ref[...]
        prev = jnp.where(first, 0.0, prev_ref[...])[SUBLANES - 1:SUBLANES, :]
        sh = jnp.where(rows == 0, prev, pltpu.roll(cur, 1, axis=0))
        return cur + (sh - cur) * mu_ref[part:part + 1, :]

    r = shifted(r_ref, rp_ref, 0)
    k = shifted(k_ref, kp_ref, 1)
    v = shifted(v_ref, vp_ref, 2)
    lo = shifted(l_ref, lp_ref, 3)
    lo_a = lo[:, 0:LANES]
    lo_g = lo[:, LANES:3 * LANES]
    wpre = w0_ref[...] + _dot(jnp.tanh(lo_a).astype(BF16), w2_ref[...])
    y = -wpre
    softplus = jnp.maximum(y, 0.0) + jnp.log(1.0 + jnp.exp(-jnp.abs(y)))
    w_log = -softplus - 0.5
    ld_ref[...] = -jnp.exp(w_log)
    a = _sigmoid(a0_ref[...] + _dot(lo_a.astype(BF16), a2_ref[...]))
    go_ref[...] = _dot(_sigmoid(lo_g).astype(BF16), g2_ref[...])
    kk = k * kk_ref[...]
    bd = (lax.broadcasted_iota(I32, (LANES, LANES), 0) // B_HD
          == lax.broadcasted_iota(I32, (LANES, LANES), 1) // B_HD).astype(BF16)
    sq = kk * kk
    ss = jnp.concatenate([_dot_exact_rhs(sq[:, j * LANES:(j + 1) * LANES], bd) for j in range(B_W // LANES)], axis=1)
    kk = kk / jnp.maximum(jnp.sqrt(ss), 1e-12)
    ro_ref[...] = r
    vo_ref[...] = v
    kko_ref[...] = kk
    kka_ref[...] = kk * a
    k2_ref[...] = k * (1.0 + (a - 1.0) * ka_ref[...])


def rwkv_prep(p, seq, mu4, w0, w2p, a0, a2p, g2p, k_k, k_a, tm=256):
    m = p.shape[0]
    tm = min(tm, seq)
    cb = COL_B_RKV // B_W
    pb = tm // SUBLANES

    def cur(j):
        return pl.BlockSpec((tm, B_W), lambda i, j=j: (i, cb + j))

    def prev(j):
        return pl.BlockSpec((SUBLANES, B_W), lambda i, j=j: (jnp.maximum(i * pb - 1, 0), cb + j))

    def full(a):
        return pl.BlockSpec(a.shape, lambda i: (0, 0))

    row = lambda a: a.reshape(1, B_W)
    params = [mu4, row(w0), w2p, row(a0), a2p, g2p, row(k_k), row(k_a)]
    out = jax.ShapeDtypeStruct((m, B_W), F32)
    return pl.pallas_call(
        functools.partial(_rwkv_prep_kernel, tiles_per_seq=seq // tm, tm=tm),
        grid=(m // tm,),
        in_specs=[cur(0), cur(1), cur(2), cur(3), prev(0), prev(1), prev(2), prev(3)] + [full(a) for a in params],
        out_specs=[pl.BlockSpec((tm, B_W), lambda i: (i, 0))] * 7,
        out_shape=[out] * 7,
        compiler_params=_cparams(("parallel",)),
        name="rwkv_prep",
    )(p, p, p, p, p, p, p, p, *params)


B_T = 64


def _rwkv_kernel(r_ref, ld_ref, k_ref, v_ref, kk_ref, kka_ref, g_ref, rk_ref, lnw_ref, lnb_ref, o_ref, st_ref, *,
                 n_chunks, pairs):
    @pl.when(pl.program_id(2) == 0)
    def _():
        st_ref[...] = jnp.zeros_like(st_ref)

    t = B_T
    ii = lax.broadcasted_iota(I32, (t, t), 0)
    jj = lax.broadcasted_iota(I32, (t, t), 1)
    tri = (ii >= jj).astype(BF16)
    i2 = lax.broadcasted_iota(I32, (2 * t, 2 * t), 0)
    j2 = lax.broadcasted_iota(I32, (2 * t, 2 * t), 1)
    same = (i2 // t) == (j2 // t)
    strict_bd = same & ((i2 % t) > (j2 % t))
    incl_bd = same & ((i2 % t) >= (j2 % t))
    head_bd = same.astype(BF16)
    lane = lax.broadcasted_iota(I32, (t, LANES), 1)
    h0 = lane < B_HD

    def stack(x):
        return jnp.concatenate([jnp.where(h0, x, jnp.zeros_like(x)), jnp.where(h0, jnp.zeros_like(x), x)], axis=0)

    def head_sum(x):
        hi, lo = _split2(x)
        return _dot(hi, head_bd) + _dot(lo, head_bd)

    def chunk(c, carry):
        sl = pl.ds(pl.multiple_of(c * t, t), t)
        prs = range(pairs)
        cols = [slice(pp * LANES, (pp + 1) * LANES) for pp in prs]
        r = [r_ref[sl, cl] for cl in cols]
        ld = [ld_ref[sl, cl] for cl in cols]
        k = [k_ref[sl, cl] for cl in cols]
        v = [v_ref[sl, cl] for cl in cols]
        cs = []
        for pp in prs:
            ld_hi, ld_lo = _split2(ld[pp])
            cs.append(_dot(tri, ld_hi) + _dot(tri, ld_lo))
        lhs, rhs, vs = [], [], []
        for pp in prs:
            e_neg = jnp.exp(-cs[pp])
            kka = kka_ref[sl, cols[pp]]
            a_t = (-kk_ref[sl, cols[pp]] * jnp.exp(cs[pp] - ld[pp])).astype(BF16)
            r_t = (r[pp] * jnp.exp(cs[pp])).astype(BF16)
            b_t = (kka * e_neg).astype(BF16)
            k_t = (k[pp] * e_neg).astype(BF16)
            lhs.append(jnp.concatenate([stack(a_t), stack(r_t)], axis=0))
            rhs.append(jnp.concatenate([b_t, b_t, k_t, k_t], axis=0))
            vs.append(stack(v[pp].astype(BF16)))
        sc = [_dot_nt(lhs[pp], rhs[pp]) for pp in prs]
        st = [st_ref[pp] for pp in prs]
        proj = [_dot_nt(lhs[pp], st[pp].astype(BF16)) for pp in prs]
        n = [jnp.where(strict_bd, sc[pp][0:2 * t, 0:2 * t], 0.0).astype(BF16) for pp in prs]
        xs = [proj[pp][0:2 * t] + _dot(jnp.where(strict_bd, sc[pp][0:2 * t, 2 * t:4 * t], 0.0).astype(BF16), vs[pp])
              for pp in prs]
        for it in range(6):
            xs = [xs[pp] + _dot(n[pp], xs[pp].astype(BF16)) for pp in prs]
            if it < 5:
                n = [_dot(n[pp], n[pp]).astype(BF16) for pp in prs]
        os_ = []
        for pp in prs:
            m_r = jnp.concatenate([jnp.where(incl_bd, sc[pp][2 * t:4 * t, 0:2 * t], 0.0),
                                   jnp.where(incl_bd, sc[pp][2 * t:4 * t, 2 * t:4 * t], 0.0)], axis=1)
            uv = jnp.concatenate([xs[pp].astype(BF16), vs[pp]], axis=0)
            os_.append(proj[pp][2 * t:4 * t] + _dot(m_r.astype(BF16), uv))
        upd = []
        for pp in prs:
            u = xs[pp][0:t] + xs[pp][t:2 * t]
            c_last = cs[pp][t - 1:t]
            dec = jnp.exp(c_last - cs[pp])
            upd.append(_dot_tn(jnp.concatenate([u, v[pp]], axis=0).astype(BF16),
                               jnp.concatenate([kka_ref[sl, cols[pp]] * dec, k[pp] * dec], axis=0).astype(BF16)))
        for pp in prs:
            st_ref[pp] = st[pp] * jnp.exp(cs[pp][t - 1:t]) + jnp.where(same, upd[pp], 0.0)
        inv = 1.0 / B_HD
        o = [os_[pp][0:t] + os_[pp][t:2 * t] for pp in prs]
        mean = [head_sum(o[pp]) * inv for pp in prs]
        d = [o[pp] - mean[pp] for pp in prs]
        var = [head_sum(d[pp] * d[pp]) * inv for pp in prs]
        bonus = [head_sum(r[pp] * k[pp] * rk_ref[:, cols[pp]]) for pp in prs]
        for pp in prs:
            on = d[pp] * lax.rsqrt(var[pp] + B_GN_EPS) * lnw_ref[:, cols[pp]] + lnb_ref[:, cols[pp]]
            o_ref[sl, cols[pp]] = ((on + bonus[pp] * v[pp]) * g_ref[sl, cols[pp]]).astype(o_ref.dtype)
        return carry

    lax.fori_loop(0, n_chunks, chunk, 0)


def rwkv_recurrence(r, ld, k2, v, kk, kka, g, r_k, ln_w, ln_b, bsz, seq, s_blk=256, pairs=4):
    s_blk = min(s_blk, seq)
    width = pairs * LANES
    groups = B_W // width
    args = [a.reshape(bsz, seq, B_W) for a in (r, ld, k2, v, kk, kka, g)]
    blk = pl.BlockSpec((None, s_blk, width), lambda b, h, s: (b, s, h))
    par = pl.BlockSpec((1, width), lambda b, h, s: (0, h))
    return pl.pallas_call(
        functools.partial(_rwkv_kernel, n_chunks=s_blk // B_T, pairs=pairs),
        grid=(bsz, groups, seq // s_blk),
        in_specs=[blk] * 7 + [par] * 3,
        out_specs=blk,
        out_shape=jax.ShapeDtypeStruct((bsz, seq, B_W), BF16),
        scratch_shapes=[pltpu.VMEM((pairs, LANES, LANES), F32)],
        compiler_params=_cparams(("parallel", "parallel", "arbitrary")),
        name="rwkv_recurrence",
    )(*args, r_k.reshape(1, B_W), ln_w.reshape(1, B_W), ln_b.reshape(1, B_W))


def _rope(x, cos_t, sin_t, lane_in_head, half):
    partner = jnp.where(lane_in_head < half, pltpu.roll(x, LANES - half, axis=1), pltpu.roll(x, half, axis=1))
    return x * cos_t + partner * sin_t


def _attn_prep_kernel(q_ref, k_ref, v_ref, qi_ref, kw_ref, ca_ref, sa_ref, ci_ref, si_ref, qg_ref, kg_ref,
                      qo_ref, ko_ref, vo_ref, qio_ref, kio_ref, wo_ref, *, tm):
    ca, sa, ci, si = ca_ref[...], sa_ref[...], ci_ref[...], si_ref[...]
    lane = lax.broadcasted_iota(I32, (tm, LANES), 1)
    lane_i = lane % IDX_HD
    half_a = C_HD // ROPE_FRACTION // 2
    half_i = IDX_HD // ROPE_FRACTION // 2
    scale = C_HD ** -0.5 * LOG2_E

    def norm_rope(x, gain):
        ms = jnp.mean(x * x, axis=-1, keepdims=True)
        return _rope(x * lax.rsqrt(ms + NORM_EPS) * gain, ca, sa, lane, half_a)

    for h in range(C_HEADS):
        sl = slice(h * C_HD, (h + 1) * C_HD)
        qo_ref[:, sl] = (norm_rope(q_ref[:, sl], qg_ref[...]) * scale).astype(qo_ref.dtype)
    for h in range(C_KV_HEADS):
        sl = slice(h * C_HD, (h + 1) * C_HD)
        ko_ref[:, sl] = norm_rope(k_ref[:, sl], kg_ref[...]).astype(ko_ref.dtype)
    vo_ref[...] = v_ref[...].astype(vo_ref.dtype)
    for j in range(IDX_HEADS * IDX_HD // LANES):
        sl = slice(j * LANES, (j + 1) * LANES)
        qio_ref[:, sl] = _rope(qi_ref[:, sl], ci, si, lane_i, half_i).astype(qio_ref.dtype)
    kw = kw_ref[:, 0:LANES]
    kr = _rope(kw, ci, si, lane_i, half_i)
    kio_ref[...] = jnp.where(lane < IDX_HD, kr, pltpu.roll(kr, IDX_HD, axis=1)).astype(kio_ref.dtype)
    w = pltpu.roll(kw, LANES - IDX_HD, axis=1) * (IDX_HEADS ** -0.5 * IDX_HD ** -0.5)
    wo_ref[...] = jnp.where(lane < IDX_HEADS, w, 0.0)


def attn_prep(p, tabs, q_gain, k_gain, tm=256):
    m = p.shape[0]
    tm = min(tm, m)

    def colblk(width, off):
        return pl.BlockSpec((tm, width), lambda i: (i, off // width))

    tab = pl.BlockSpec((tm, LANES), lambda i: (i, 0))
    gain = pl.BlockSpec((1, C_HD), lambda i: (0, 0))
    kw_width = 512

    def out(width, dtype):
        return jax.ShapeDtypeStruct((m, width), dtype), pl.BlockSpec((tm, width), lambda i: (i, 0))

    outs = [out(C_W, BF16), out(C_KV_W, BF16), out(C_KV_W, BF16), out(IDX_HEADS * IDX_HD, BF16),
            out(LANES, BF16), out(LANES, F32)]
    return pl.pallas_call(
        functools.partial(_attn_prep_kernel, tm=tm),
        grid=(m // tm,),
        in_specs=[colblk(C_W, COL_C_Q), colblk(C_KV_W, COL_C_K), colblk(C_KV_W, COL_C_V),
                  colblk(IDX_HEADS * IDX_HD, COL_C_QI), colblk(kw_width, COL_C_KW), tab, tab, tab, tab, gain, gain],
        out_specs=[o[1] for o in outs],
        out_shape=[o[0] for o in outs],
        compiler_params=_cparams(("parallel",)),
        name="attn_prep",
    )(p, p, p, p, p, *tabs, q_gain.reshape(1, C_HD), k_gain.reshape(1, C_HD))


V_ONES = 16


def _attn_kernel(q_ref, k_ref, vt_ref, qi_ref, ki_ref, w_ref, o_ref, key_sc, bias_sc, m_sc, acc_sc, *, tq, tk, topk):
    qb = pl.program_id(1)
    nkb = ((qb + 1) * tq + tk - 1) // tk
    lane = lax.broadcasted_iota(I32, (tq, LANES), 1)
    w_t = w_ref[...].T
    q_chunk = (qb * tq + lax.broadcasted_iota(I32, (tk, tq), 1)) // CHUNK
    key_in_blk = lax.broadcasted_iota(I32, (tk, tq), 0)
    n_pairs = IDX_HEADS * IDX_HD // LANES
    fold = 64

    q_pairs = []
    for hp in range(n_pairs):
        qp = qi_ref[:, hp * LANES:(hp + 1) * LANES]
        zero = jnp.zeros_like(qp)
        q_pairs.append(jnp.concatenate([jnp.where(lane < IDX_HD, qp, zero), jnp.where(lane < IDX_HD, zero, qp)], axis=0))

    def score_block(kb, carry):
        c0 = pl.multiple_of(kb * tk, tk)
        ki2 = ki_ref[pl.ds(c0, tk), :]
        sc = jnp.zeros((tk, tq), F32)
        for hp in range(n_pairs):
            rel = jnp.maximum(_dot_nt(ki2, q_pairs[hp]), 0.0)
            sc = sc + w_t[2 * hp:2 * hp + 1, :] * rel[:, 0:tq] + w_t[2 * hp + 1:2 * hp + 2, :] * rel[:, tq:2 * tq]
        sc = jnp.where(sc == 0.0, 0.0, sc)
        bits = lax.bitcast_convert_type(sc, I32)
        skey = bits ^ ((bits >> 31) & 0x7FFFFFFF)
        allowed = ((c0 + key_in_blk) // CHUNK) <= q_chunk
        key_sc[kb] = jnp.where(allowed, skey, INT_MIN)
        return carry

    lax.fori_loop(0, nkb, score_block, 0)

    def count_ge(cand):
        def body(kb, acc):
            hit = jnp.where(key_sc[kb] >= cand, 1.0, 0.0)
            for j in range(tk // fold):
                acc = acc + hit[j * fold:(j + 1) * fold]
            return acc
        acc = lax.fori_loop(0, nkb, body, jnp.zeros((fold, tq), F32))
        return jnp.sum(acc, axis=0, keepdims=True)

    kf = float(topk)
    thr = jnp.where(count_ge(jnp.zeros((1, tq), I32)) >= kf, 0, INT_MIN).astype(I32)

    def bit_step(i, thr):
        cand = thr | (jnp.int32(1) << (30 - i))
        return jnp.where(count_ge(cand) >= kf, cand, thr)

    thr = lax.fori_loop(0, 31, bit_step, thr)
    thr = jnp.maximum(thr, INT_MIN + 1)
    need = kf - count_ge(thr + 1)

    tri = (lax.broadcasted_iota(I32, (tk, tk), 0) >= lax.broadcasted_iota(I32, (tk, tk), 1)).astype(BF16)

    def bias_block(kb, seen):
        keys = key_sc[kb]
        eq = keys == thr
        eq_f = jnp.where(eq, 1.0, 0.0)
        rank = _dot(tri, eq_f.astype(BF16)) + seen
        take = (keys > thr) | (eq & (rank <= need))
        bias_sc[kb] = jnp.where(take, 0.0, NEG_BIG)
        return rank[tk - 1:tk]

    lax.fori_loop(0, nkb, bias_block, jnp.zeros((1, tq), F32))

    rep = C_HEADS // C_KV_HEADS
    ve = C_HD + V_ONES
    qs = [jnp.concatenate([q_ref[:, (g * rep + r) * C_HD:(g * rep + r + 1) * C_HD] for r in range(rep)], axis=0)
          for g in range(C_KV_HEADS)]
    m_sc[...] = jnp.full(m_sc.shape, NEG_BIG, F32)
    acc_sc[...] = jnp.zeros(acc_sc.shape, F32)

    def body(kb, carry):
        c0 = pl.multiple_of(kb * tk, tk)
        bias = bias_sc[kb]
        s_all = [_dot_nt(k_ref[pl.ds(c0, tk), g * C_HD:(g + 1) * C_HD], qs[g]) for g in range(C_KV_HEADS)]
        for g in range(C_KV_HEADS):
            s = jnp.concatenate([s_all[g][:, r * tq:(r + 1) * tq] + bias for r in range(rep)], axis=1)
            m_prev = m_sc[g]
            m_new = jnp.maximum(m_prev, jnp.max(s, axis=0, keepdims=True))
            p = jnp.exp2(s - m_new)
            acc_sc[g] = jnp.exp2(m_prev - m_new) * acc_sc[g] + _dot(vt_ref[kb, g * ve:(g + 1) * ve, :], p.astype(BF16))
            m_sc[g] = m_new
        return carry

    lax.fori_loop(0, nkb, body, 0)
    for g in range(C_KV_HEADS):
        acc = acc_sc[g]
        out_t = acc[0:C_HD] / acc[C_HD:C_HD + 1]
        for r in range(rep):
            h = g * rep + r
            o_ref[:, h * C_HD:(h + 1) * C_HD] = out_t[:, r * tq:(r + 1) * tq].T.astype(o_ref.dtype)


def sparse_attention(qn, kn, vn, qi, ki2, w, bsz, seq, tq=128, tk=512):
    tk = min(tk, seq)
    topk = min(TOPK_MAX, seq // 4)
    nkb = seq // tk
    rep = C_HEADS // C_KV_HEADS
    ve = C_HD + V_ONES
    r3 = lambda a: a.reshape(bsz, seq, a.shape[-1])
    v_t = vn.reshape(bsz, nkb, tk, C_KV_HEADS, C_HD).transpose(0, 1, 3, 4, 2)
    v_t = jnp.concatenate([v_t, jnp.ones((bsz, nkb, C_KV_HEADS, V_ONES, tk), v_t.dtype)], axis=3)
    v_t = v_t.reshape(bsz, nkb, C_KV_HEADS * ve, tk)
    qblk = lambda width: pl.BlockSpec((None, tq, width), lambda b, i: (b, i, 0))
    sblk = lambda width: pl.BlockSpec((None, seq, width), lambda b, i: (b, 0, 0))
    return pl.pallas_call(
        functools.partial(_attn_kernel, tq=tq, tk=tk, topk=topk),
        grid=(bsz, seq // tq),
        in_specs=[qblk(C_W), sblk(C_KV_W), pl.BlockSpec((None, nkb, C_KV_HEADS * ve, tk), lambda b, i: (b, 0, 0, 0)),
                  qblk(IDX_HEADS * IDX_HD), sblk(LANES), qblk(LANES)],
        out_specs=qblk(C_W),
        out_shape=jax.ShapeDtypeStruct((bsz, seq, C_W), BF16),
        scratch_shapes=[pltpu.VMEM((nkb, tk, tq), I32), pltpu.VMEM((nkb, tk, tq), F32),
                        pltpu.VMEM((C_KV_HEADS, 1, rep * tq), F32), pltpu.VMEM((C_KV_HEADS, ve, rep * tq), F32)],
        compiler_params=_cparams(("parallel", "arbitrary")),
        name="sparse_attention",
    )(r3(qn), r3(kn), v_t, r3(qi), r3(ki2), r3(w))


def _pack_w_in(w):
    depth, d, _ = w.shape
    a_cols = 4 * A_W
    b_cols = 3 * B_W + B_LORA_DECAY + B_LORA_A + B_LORA_G
    c_cols = C_W + 2 * C_KV_W + IDX_HEADS * IDX_HD + IDX_HD + IDX_HEADS
    e_b = a_cols + b_cols
    e_c = e_b + c_cols
    z = lambda n: jnp.zeros((depth, d, n), BF16)
    c = lambda a: a.astype(BF16)
    return jnp.concatenate([c(w[:, :, :e_b]), z(COL_C_Q - e_b), c(w[:, :, e_b:e_c]), z(COL_G - COL_C_Q - c_cols),
                            c(w[:, :, e_c:])], axis=2)


def _rope_tables(positions):
    pos = positions.astype(F32).reshape(-1, 1)

    def tables(head_dim):
        rot = head_dim // ROPE_FRACTION
        half = rot // 2
        inv_freq = ROPE_THETA ** (-jnp.arange(0, rot, 2, dtype=F32) / rot)
        ang = pos * inv_freq
        cos, sin = jnp.cos(ang), jnp.sin(ang)
        ones = jnp.ones((pos.shape[0], head_dim - rot), F32)
        c = jnp.concatenate([cos, cos, ones], axis=1)
        s = jnp.concatenate([-sin, sin, 0.0 * ones], axis=1)
        reps = LANES // head_dim
        return jnp.tile(c, (1, reps)), jnp.tile(s, (1, reps))

    ca, sa = tables(C_HD)
    ci, si = tables(IDX_HD)
    return ca, sa, ci, si


def _pad_rows(a, before, total):
    return jnp.zeros((total, a.shape[1]), a.dtype).at[before:before + a.shape[0]].set(a)


def kernel(x, positions, norm_mix, w_in, hgrn_lb_logits, hgrn_out_norm, rwkv_mu, rwkv_w0, rwkv_w2, rwkv_a0, rwkv_a2,
           rwkv_g2, rwkv_k_k, rwkv_k_a, rwkv_r_k, rwkv_ln_w, rwkv_ln_b, q_norm, k_norm, w_branch_a, w_branch_b,
           w_branch_c, w_out, norm_ffn, w_up, conv_w, conv_b, w_down):
    bsz, seq, d_model = x.shape
    depth = w_in.shape[0]
    m = bsz * seq
    tabs = _rope_tables(positions)
    lb_all = jnp.cumsum(jax.nn.softmax(hgrn_lb_logits.astype(F32), axis=0), axis=0)
    lb_all = lb_all - lb_all[0:1]
    n_lora = B_LORA_DECAY + B_LORA_A + B_LORA_G
    w_in_p = _pack_w_in(w_in)
    w_a, w_b, w_c = w_branch_a.astype(BF16), w_branch_b.astype(BF16), w_branch_c.astype(BF16)
    w_o, w_d = w_out.astype(BF16), w_down.astype(BF16)

    xf = x.reshape(m, d_model)
    for l in range(depth):
        h = rmsnorm(xf, norm_mix[l])
        p = proj_in(h, w_in_p, l)

        o_a = hgrn2(p.reshape(bsz, seq, -1), lb_all[l], hgrn_out_norm[l]).reshape(m, A_W)

        mu = rwkv_mu[l]
        mu4 = jnp.stack([mu[0:B_W], mu[B_W:2 * B_W], mu[2 * B_W:3 * B_W],
                         jnp.pad(mu[3 * B_W:], (0, B_W - n_lora))])
        w2p = _pad_rows(rwkv_w2[l], 0, LANES).astype(BF16)
        a2p = _pad_rows(rwkv_a2[l], B_LORA_DECAY, LANES).astype(BF16)
        g2p = _pad_rows(rwkv_g2[l], 0, 2 * LANES).astype(BF16)
        parts = rwkv_prep(p, seq, mu4, rwkv_w0[l], w2p, rwkv_a0[l], a2p, g2p, rwkv_k_k[l], rwkv_k_a[l])
        o_b = rwkv_recurrence(*parts, rwkv_r_k[l], rwkv_ln_w[l], rwkv_ln_b[l], bsz, seq).reshape(m, B_W)

        qn, kn, vn, qi, ki2, wi = attn_prep(p, tabs, q_norm[l], k_norm[l])
        o_c = sparse_attention(qn, kn, vn, qi, ki2, wi, bsz, seq).reshape(m, C_W)

        merged = merge_branches(o_a, o_b, o_c, w_a, w_b, w_c, l, p, d_model)
        xf = matmul_residual(merged, w_o, l, xf)

        h2 = rmsnorm(xf, norm_ffn[l])
        act = ffn_up(h2, w_up, l, conv_w[l], conv_b[l].reshape(1, -1), seq)
        xf = matmul_residual(act, w_d, l, xf, tm=512, tn=256)
    return xf.reshape(bsz, seq, d_model)
```

```python
import functools

import jax
import jax.numpy as jnp
import numpy as np
from jax import lax
from jax.experimental import pallas as pl
from jax.experimental.pallas import tpu as pltpu

F32 = jnp.float32
BF16 = jnp.bfloat16
I32 = jnp.int32

CHUNK = 64
ROPE_THETA = 500000.0
ROPE_FRACTION = 4
NORM_EPS = 1e-6
A_HEADS, A_DK, A_DV = 8, 128, 128
A_W = A_HEADS * A_DV
B_HEADS, B_HD = 16, 64
B_W = B_HEADS * B_HD
B_LORA_DECAY, B_LORA_A, B_LORA_G = 64, 64, 160
B_GN_EPS = 64e-5
C_HEADS, C_KV_HEADS, C_HD = 16, 4, 128
C_W = C_HEADS * C_HD
C_KV_W = C_KV_HEADS * C_HD
IDX_HEADS, IDX_HD = 16, 64
TOPK_MAX = 256
CONV_W = 3

LANES = 128
SUBLANES = 8
VMEM_LIMIT = 56 * 1024 * 1024

COL_A = 0
COL_B_RKV = 4096
COL_B_LORA = 7168
W_AB = 8192
COL_C_Q = 0
COL_C_K = 2048
COL_C_V = 2560
COL_C_QI = 3072
COL_C_KW = 4096
W_C = 4608
LOG2_E = 1.4426950408889634
NEG_BIG = -1e30
INT_MIN = -2147483648


def _sigmoid(x):
    return 1.0 / (1.0 + jnp.exp(-x))


def _dot(a, b):
    return jnp.dot(a, b, preferred_element_type=F32)


def _dot_nt(a, b):
    return lax.dot_general(a, b, (((1,), (1,)), ((), ())), preferred_element_type=F32)


def _dot_tn(a, b):
    return lax.dot_general(a, b, (((0,), (0,)), ((), ())), preferred_element_type=F32)


def _split3(x):
    hi = x.astype(BF16)
    r1 = x - hi.astype(F32)
    mid = r1.astype(BF16)
    lo = (r1 - mid.astype(F32)).astype(BF16)
    return hi, mid, lo


def _dot_exact_lhs(m_bf16, x):
    hi, mid, lo = _split3(x)
    return _dot(m_bf16, hi) + _dot(m_bf16, mid) + _dot(m_bf16, lo)


def _dot_exact_rhs(x, m_bf16):
    hi, mid, lo = _split3(x)
    return _dot(hi, m_bf16) + _dot(mid, m_bf16) + _dot(lo, m_bf16)


def _split2(x):
    hi = x.astype(BF16)
    lo = (x - hi.astype(F32)).astype(BF16)
    return hi, lo


def _dot_hp(a, b):
    ah, al = _split2(a)
    bh, bl = _split2(b)
    return _dot(ah, bh) + _dot(ah, bl) + _dot(al, bh)


def _dot_nt_hp(a, b):
    ah, al = _split2(a)
    bh, bl = _split2(b)
    return _dot_nt(ah, bh) + _dot_nt(ah, bl) + _dot_nt(al, bh)


def _dot_tn_hp(a, b):
    ah, al = _split2(a)
    bh, bl = _split2(b)
    return _dot_tn(ah, bh) + _dot_tn(ah, bl) + _dot_tn(al, bh)


def _cparams(sem):
    return pltpu.CompilerParams(dimension_semantics=sem, vmem_limit_bytes=VMEM_LIMIT)


def _rmsnorm_kernel(x_ref, g_ref, o_ref):
    x = x_ref[...]
    ms = jnp.mean(x * x, axis=-1, keepdims=True)
    o_ref[...] = (x * lax.rsqrt(ms + NORM_EPS) * g_ref[...]).astype(o_ref.dtype)


def rmsnorm(x, gain, tm=256):
    m, d = x.shape
    return pl.pallas_call(
        _rmsnorm_kernel,
        grid=(m // tm,),
        in_specs=[pl.BlockSpec((tm, d), lambda i: (i, 0)), pl.BlockSpec((1, d), lambda i: (0, 0))],
        out_specs=pl.BlockSpec((tm, d), lambda i: (i, 0)),
        out_shape=jax.ShapeDtypeStruct((m, d), BF16),
        compiler_params=_cparams(("parallel",)),
        name="rmsnorm",
    )(x, gain.reshape(1, d))


def _wspec(w, layer, tn, jmap):
    return pl.BlockSpec((None, w.shape[1], tn), lambda *g: (layer, 0, jmap(*g)))


def _proj_in_kernel(h_ref, w_ref, o_ref):
    o_ref[...] = _dot(h_ref[...], w_ref[...]).astype(o_ref.dtype)


def proj_in(h, w, layer, out_dtype, tm=1024, tn=512):
    m, k = h.shape
    n = w.shape[2]
    tm = min(tm, m)
    return pl.pallas_call(
        _proj_in_kernel,
        grid=(m // tm, n // tn),
        in_specs=[pl.BlockSpec((tm, k), lambda i, j: (i, 0)), _wspec(w, layer, tn, lambda i, j: j)],
        out_specs=pl.BlockSpec((tm, tn), lambda i, j: (i, j)),
        out_shape=jax.ShapeDtypeStruct((m, n), out_dtype),
        compiler_params=_cparams(("parallel", "arbitrary")),
        name="proj_in",
    )(h, w)


def _pack_cols_kernel(*refs, shift, width):
    a_ref, o_ref = refs[0], refs[-1]
    t = pl.program_id(1)
    lane = lax.broadcasted_iota(I32, o_ref.shape, 1)
    a = a_ref[...]
    if shift:
        keep = LANES - shift
        a = jnp.where(lane < keep, pltpu.roll(a, keep, axis=1), pltpu.roll(refs[1][...], keep, axis=1))
    o_ref[...] = jnp.where(t * LANES + lane < width, a, 0.0).astype(o_ref.dtype)


def pack_cols(w, src_start, width, dst_width):
    depth, k, n_src = w.shape
    q0, shift = divmod(src_start, LANES)
    last = (n_src - 1) // LANES

    def src(extra):
        return pl.BlockSpec((None, k, LANES), lambda l, t: (l, 0, jnp.minimum(q0 + t + extra, last)))

    srcs = [src(0), src(1)] if shift else [src(0)]
    return pl.pallas_call(
        functools.partial(_pack_cols_kernel, shift=shift, width=width),
        grid=(depth, dst_width // LANES),
        in_specs=srcs,
        out_specs=pl.BlockSpec((None, k, LANES), lambda l, t: (l, 0, t)),
        out_shape=jax.ShapeDtypeStruct((depth, k, dst_width), BF16),
        compiler_params=_cparams(("parallel", "arbitrary")),
        name="pack_cols",
    )(*([w] * len(srcs)))


def _mm_res_kernel(a_ref, b_ref, r_ref, o_ref):
    o_ref[...] = r_ref[...] + _dot(a_ref[...], b_ref[...])


def matmul_residual(a, b, layer, res, tm=512, tn=512):
    m, k = a.shape
    n = b.shape[2]
    tm = min(tm, m)
    tn = min(tn, n)
    return pl.pallas_call(
        _mm_res_kernel,
        grid=(m // tm, n // tn),
        in_specs=[
            pl.BlockSpec((tm, k), lambda i, j: (i, 0)),
            _wspec(b, layer, tn, lambda i, j: j),
            pl.BlockSpec((tm, tn), lambda i, j: (i, j)),
        ],
        out_specs=pl.BlockSpec((tm, tn), lambda i, j: (i, j)),
        out_shape=jax.ShapeDtypeStruct((m, n), F32),
        compiler_params=_cparams(("parallel", "arbitrary")),
        name="matmul_residual",
    )(a, b, res)


def _merge_kernel(oa_ref, ob_ref, oc_ref, wa_ref, wb_ref, wc_ref, ga_ref, gb_ref, gc_ref, o_ref):
    acc = _sigmoid(ga_ref[...].astype(F32)) * _dot(oa_ref[...], wa_ref[...])
    acc += _sigmoid(gb_ref[...].astype(F32)) * _dot(ob_ref[...], wb_ref[...])
    acc += _sigmoid(gc_ref[...].astype(F32)) * _dot(oc_ref[...], wc_ref[...])
    o_ref[...] = acc.astype(o_ref.dtype)


def merge_branches(o_a, o_b, o_c, w_a, w_b, w_c, layer, p, d_model, tm=1024, tn=512):
    m = o_a.shape[0]
    tm = min(tm, m)
    g0 = 0
    gstep = d_model // tn
    return pl.pallas_call(
        _merge_kernel,
        grid=(m // tm, d_model // tn),
        in_specs=[
            pl.BlockSpec((tm, o_a.shape[1]), lambda i, j: (i, 0)),
            pl.BlockSpec((tm, o_b.shape[1]), lambda i, j: (i, 0)),
            pl.BlockSpec((tm, o_c.shape[1]), lambda i, j: (i, 0)),
            _wspec(w_a, layer, tn, lambda i, j: j),
            _wspec(w_b, layer, tn, lambda i, j: j),
            _wspec(w_c, layer, tn, lambda i, j: j),
            pl.BlockSpec((tm, tn), lambda i, j: (i, g0 + j)),
            pl.BlockSpec((tm, tn), lambda i, j: (i, g0 + gstep + j)),
            pl.BlockSpec((tm, tn), lambda i, j: (i, g0 + 2 * gstep + j)),
        ],
        out_specs=pl.BlockSpec((tm, tn), lambda i, j: (i, j)),
        out_shape=jax.ShapeDtypeStruct((m, d_model), BF16),
        compiler_params=_cparams(("parallel", "arbitrary")),
        name="merge_branches",
    )(o_a, o_b, o_c, w_a, w_b, w_c, p, p, p)


def _ffn_up_kernel(h_ref, wg_ref, wu_ref, cwg_ref, cwu_ref, cbg_ref, cbu_ref, o_ref, w_sc, cg_ref, cu_ref, *,
                   tiles_per_seq, tm, ts):
    i = pl.program_id(1)
    first = (i % tiles_per_seq) == 0
    tn = o_ref.shape[1]
    rows = lax.broadcasted_iota(I32, (ts, tn), 0)

    @pl.when(i == 0)
    def _():
        w_sc[:, 0:tn] = wg_ref[...].astype(BF16)
        w_sc[:, tn:2 * tn] = wu_ref[...].astype(BF16)

    @pl.when(first)
    def _():
        cg_ref[...] = jnp.zeros_like(cg_ref)
        cu_ref[...] = jnp.zeros_like(cu_ref)

    def conv(u, prev, cw_ref, cb_ref):
        u1 = jnp.where(rows == 0, prev[7:8, :], pltpu.roll(u, 1, axis=0))
        u2 = pltpu.roll(u, 2, axis=0)
        u2 = jnp.where(rows == 0, prev[6:7, :], jnp.where(rows == 1, prev[7:8, :], u2))
        cw = cw_ref[...]
        return cw[0:1, :] * u2 + cw[1:2, :] * u1 + cw[2:3, :] * u + cb_ref[...]

    prev_g = cg_ref[...]
    prev_u = cu_ref[...]
    for sb in range(tm // ts):
        u = _dot(h_ref[sb * ts:(sb + 1) * ts, :], w_sc[...])
        ug = u[:, 0:tn]
        uu = u[:, tn:2 * tn]
        gate = conv(ug, prev_g, cwg_ref, cbg_ref)
        up = conv(uu, prev_u, cwu_ref, cbu_ref)
        o_ref[sb * ts:(sb + 1) * ts, :] = (gate * _sigmoid(gate) * up).astype(o_ref.dtype)
        prev_g = ug[ts - SUBLANES:ts, :]
        prev_u = uu[ts - SUBLANES:ts, :]
    cg_ref[...] = prev_g
    cu_ref[...] = prev_u


def ffn_up(h, w_up, layer, conv_w, conv_b, seq, tm=1024, tn=256, ts=256):
    m, k = h.shape
    d_ff = w_up.shape[2] // 2
    tm = min(tm, seq)
    ts = min(ts, tm)
    nj = d_ff // tn
    return pl.pallas_call(
        functools.partial(_ffn_up_kernel, tiles_per_seq=seq // tm, tm=tm, ts=ts),
        grid=(nj, m // tm),
        in_specs=[
            pl.BlockSpec((tm, k), lambda j, i: (i, 0)),
            _wspec(w_up, layer, tn, lambda j, i: j),
            _wspec(w_up, layer, tn, lambda j, i: nj + j),
            pl.BlockSpec((CONV_W, tn), lambda j, i: (0, j)),
            pl.BlockSpec((CONV_W, tn), lambda j, i: (0, nj + j)),
            pl.BlockSpec((1, tn), lambda j, i: (0, j)),
            pl.BlockSpec((1, tn), lambda j, i: (0, nj + j)),
        ],
        out_specs=pl.BlockSpec((tm, tn), lambda j, i: (i, j)),
        out_shape=jax.ShapeDtypeStruct((m, d_ff), BF16),
        scratch_shapes=[pltpu.VMEM((k, 2 * tn), BF16), pltpu.VMEM((SUBLANES, tn), F32),
                        pltpu.VMEM((SUBLANES, tn), F32)],
        compiler_params=_cparams(("parallel", "arbitrary")),
        name="ffn_up",
    )(h, w_up, w_up, conv_w, conv_w, conv_b, conv_b)


A_SUB = 16


def _hgrn_kernel(q_ref, f_ref, i_ref, g_ref, lb_ref, gain_ref, o_ref, st_ref, *, n_chunks, heads):
    @pl.when(pl.program_id(2) == 0)
    def _():
        st_ref[...] = jnp.zeros_like(st_ref)

    gain = gain_ref[...]
    tri = (lax.broadcasted_iota(I32, (CHUNK, CHUNK), 0) >= lax.broadcasted_iota(I32, (CHUNK, CHUNK), 1)).astype(BF16)
    rows_sub = lax.broadcasted_iota(I32, (A_SUB, A_DK), 0)
    n_sub = CHUNK // A_SUB

    def one_head(hh, r0):
        cols = slice(hh * A_DK, (hh + 1) * A_DK)
        rows = pl.ds(r0, CHUNK)
        lb = lb_ref[:, cols]
        z = f_ref[rows, cols]
        qv = q_ref[rows, cols]
        iv = i_ref[rows, cols]
        gv = g_ref[rows, cols]
        lf = jnp.log(lb + (1.0 - lb) * _sigmoid(z))
        kin = (1.0 - lb) * _sigmoid(-z)
        qf = qv * _sigmoid(qv)
        b = _dot_exact_lhs(tri, lf)
        st = st_ref[hh]
        o_inter = _dot_nt((qf * jnp.exp(b)).astype(BF16), st.astype(BF16))
        iv_b = iv.astype(BF16)
        outs = []
        for si in range(n_sub):
            lo = si * A_SUB
            b_i = b[lo:lo + A_SUB]
            q_i = qf[lo:lo + A_SUB]
            k_i = kin[lo:lo + A_SUB]
            i_i = iv[lo:lo + A_SUB]
            o_i = o_inter[lo:lo + A_SUB]
            if si > 0:
                b_ref_row = b[lo - 1:lo]
                q_s = (q_i * jnp.exp(b_i - b_ref_row)).astype(BF16)
                k_s = (kin[0:lo] * jnp.exp(b_ref_row - b[0:lo])).astype(BF16)
                att = _dot_nt(q_s, k_s)
                o_i = o_i + _dot(att.astype(BF16), iv_b[0:lo])
            for s in range(A_SUB):
                d = jnp.exp(jnp.where(rows_sub >= s, b_i - b_i[s:s + 1], -jnp.inf))
                a = jnp.sum(q_i * d * k_i[s:s + 1], axis=1, keepdims=True)
                o_i = o_i + a * i_i[s:s + 1]
            outs.append(o_i)
        o = jnp.concatenate(outs, axis=0)
        b_last = b[CHUNK - 1:CHUNK]
        k_dec = (kin * jnp.exp(b_last - b)).astype(BF16)
        st_ref[hh] = st * jnp.exp(b_last) + _dot_tn(iv_b, k_dec)
        ms = jnp.mean(o * o, axis=-1, keepdims=True)
        on = o * lax.rsqrt(ms + NORM_EPS) * gain
        o_ref[rows, cols] = (on * (gv * _sigmoid(gv))).astype(o_ref.dtype)

    def chunk(c, carry):
        r0 = pl.multiple_of(c * CHUNK, CHUNK)
        for hh in range(heads):
            one_head(hh, r0)
        return carry

    lax.fori_loop(0, n_chunks, chunk, 0)


def hgrn2(p3, lb, out_gain, s_blk=256, heads=4):
    bsz, seq, _ = p3.shape
    s_blk = min(s_blk, seq)
    width = heads * A_DK
    groups = A_HEADS // heads
    c0 = COL_A // width

    def col(part):
        return pl.BlockSpec((None, s_blk, width), lambda b, h, s, part=part: (b, s, c0 + part * groups + h))

    return pl.pallas_call(
        functools.partial(_hgrn_kernel, n_chunks=s_blk // CHUNK, heads=heads),
        grid=(bsz, groups, seq // s_blk),
        in_specs=[col(0), col(1), col(2), col(3),
                  pl.BlockSpec((1, width), lambda b, h, s: (0, h)),
                  pl.BlockSpec((1, A_DV), lambda b, h, s: (0, 0))],
        out_specs=pl.BlockSpec((None, s_blk, width), lambda b, h, s: (b, s, h)),
        out_shape=jax.ShapeDtypeStruct((bsz, seq, A_W), BF16),
        scratch_shapes=[pltpu.VMEM((heads, A_DV, A_DK), F32)],
        compiler_params=_cparams(("parallel", "parallel", "arbitrary")),
        name="hgrn2",
    )(p3, p3, p3, p3, lb.reshape(1, A_HEADS * A_DK), out_gain.reshape(1, A_DV))


def _rwkv_prep_kernel(r_ref, k_ref, v_ref, l_ref, rp_ref, kp_ref, vp_ref, lp_ref,
                      mu_ref, w0_ref, w2_ref, a0_ref, a2_ref, g2_ref, kk_ref, ka_ref,
                      ro_ref, ld_ref, k2_ref, vo_ref, kko_ref, kka_ref, go_ref, *, tiles_per_seq, tm):
    first = (pl.program_id(0) % tiles_per_seq) == 0
    rows = lax.broadcasted_iota(I32, (tm, B_W), 0)

    def shifted(cur_ref, prev_ref, part):
        cur = cur_ref[...]
        prev = jnp.where(first, 0.0, prev_ref[...])[SUBLANES - 1:SUBLANES, :]
        sh = jnp.where(rows == 0, prev, pltpu.roll(cur, 1, axis=0))
        return cur + (sh - cur) * mu_ref[part:part + 1, :]

    r = shifted(r_ref, rp_ref, 0)
    k = shifted(k_ref, kp_ref, 1)
    v = shifted(v_ref, vp_ref, 2)
    lo = shifted(l_ref, lp_ref, 3)
    lo_a = lo[:, 0:LANES]
    lo_g = lo[:, LANES:3 * LANES]
    wpre = w0_ref[...] + _dot(jnp.tanh(lo_a).astype(BF16), w2_ref[...])
    y = -wpre
    softplus = jnp.maximum(y, 0.0) + jnp.log(1.0 + jnp.exp(-jnp.abs(y)))
    w_log = -softplus - 0.5
    ld_ref[...] = -jnp.exp(w_log)
    a = _sigmoid(a0_ref[...] + _dot(lo_a.astype(BF16), a2_ref[...]))
    go_ref[...] = _dot(_sigmoid(lo_g).astype(BF16), g2_ref[...])
    kk = k * kk_ref[...]
    bd = (lax.broadcasted_iota(I32, (LANES, LANES), 0) // B_HD
          == lax.broadcasted_iota(I32, (LANES, LANES), 1) // B_HD).astype(BF16)
    sq = kk * kk
    ss = jnp.concatenate([_dot_exact_rhs(sq[:, j * LANES:(j + 1) * LANES], bd) for j in range(B_W // LANES)], axis=1)
    kk = kk / jnp.maximum(jnp.sqrt(ss), 1e-12)
    ro_ref[...] = r
    vo_ref[...] = v
    kko_ref[...] = kk
    kka_ref[...] = kk * a
    k2_ref[...] = k * (1.0 + (a - 1.0) * ka_ref[...])


def rwkv_prep(p, seq, mu4, w0, w2p, a0, a2p, g2p, k_k, k_a, tm=256):
    m = p.shape[0]
    tm = min(tm, seq)
    cb = COL_B_RKV // B_W
    pb = tm // SUBLANES

    def cur(j):
        return pl.BlockSpec((tm, B_W), lambda i, j=j: (i, cb + j))

    def prev(j):
        return pl.BlockSpec((SUBLANES, B_W), lambda i, j=j: (jnp.maximum(i * pb - 1, 0), cb + j))

    def full(a):
        return pl.BlockSpec(a.shape, lambda i: (0, 0))

    row = lambda a: a.reshape(1, B_W)
    params = [mu4, row(w0), w2p, row(a0), a2p, g2p, row(k_k), row(k_a)]
    out = jax.ShapeDtypeStruct((m, B_W), F32)
    return pl.pallas_call(
        functools.partial(_rwkv_prep_kernel, tiles_per_seq=seq // tm, tm=tm),
        grid=(m // tm,),
        in_specs=[cur(0), cur(1), cur(2), cur(3), prev(0), prev(1), prev(2), prev(3)] + [full(a) for a in params],
        out_specs=[pl.BlockSpec((tm, B_W), lambda i: (i, 0))] * 7,
        out_shape=[out] * 7,
        compiler_params=_cparams(("parallel",)),
        name="rwkv_prep",
    )(p, p, p, p, p, p, p, p, *params)


B_T = 64


def _rwkv_kernel(r_ref, ld_ref, k_ref, v_ref, kk_ref, kka_ref, g_ref, rk_ref, lnw_ref, lnb_ref, o_ref, st_ref, *,
                 n_chunks, pairs):
    @pl.when(pl.program_id(2) == 0)
    def _():
        st_ref[...] = jnp.zeros_like(st_ref)

    t = B_T
    ii = lax.broadcasted_iota(I32, (t, t), 0)
    jj = lax.broadcasted_iota(I32, (t, t), 1)
    tri = (ii >= jj).astype(BF16)
    i2 = lax.broadcasted_iota(I32, (2 * t, 2 * t), 0)
    j2 = lax.broadcasted_iota(I32, (2 * t, 2 * t), 1)
    same = (i2 // t) == (j2 // t)
    strict_bd = same & ((i2 % t) > (j2 % t))
    incl_bd = same & ((i2 % t) >= (j2 % t))
    head_bd = same.astype(BF16)
    lane = lax.broadcasted_iota(I32, (t, LANES), 1)
    h0 = lane < B_HD

    def stack(x):
        return jnp.concatenate([jnp.where(h0, x, jnp.zeros_like(x)), jnp.where(h0, jnp.zeros_like(x), x)], axis=0)

    def head_sum(x):
        hi, lo = _split2(x)
        return _dot(hi, head_bd) + _dot(lo, head_bd)

    def chunk(c, carry):
        sl = pl.ds(pl.multiple_of(c * t, t), t)
        prs = range(pairs)
        cols = [slice(pp * LANES, (pp + 1) * LANES) for pp in prs]
        r = [r_ref[sl, cl] for cl in cols]
        ld = [ld_ref[sl, cl] for cl in cols]
        k = [k_ref[sl, cl] for cl in cols]
        v = [v_ref[sl, cl] for cl in cols]
        cs = []
        for pp in prs:
            ld_hi, ld_lo = _split2(ld[pp])
            cs.append(_dot(tri, ld_hi) + _dot(tri, ld_lo))
        lhs, rhs, vs = [], [], []
        for pp in prs:
            e_neg = jnp.exp(-cs[pp])
            kka = kka_ref[sl, cols[pp]]
            a_t = (-kk_ref[sl, cols[pp]] * jnp.exp(cs[pp] - ld[pp])).astype(BF16)
            r_t = (r[pp] * jnp.exp(cs[pp])).astype(BF16)
            b_t = (kka * e_neg).astype(BF16)
            k_t = (k[pp] * e_neg).astype(BF16)
            lhs.append(jnp.concatenate([stack(a_t), stack(r_t)], axis=0))
            rhs.append(jnp.concatenate([b_t, b_t, k_t, k_t], axis=0))
            vs.append(stack(v[pp].astype(BF16)))
        sc = [_dot_nt(lhs[pp], rhs[pp]) for pp in prs]
        st = [st_ref[pp] for pp in prs]
        proj = [_dot_nt(lhs[pp], st[pp].astype(BF16)) for pp in prs]
        n = [jnp.where(strict_bd, sc[pp][0:2 * t, 0:2 * t], 0.0).astype(BF16) for pp in prs]
        xs = [proj[pp][0:2 * t] + _dot(jnp.where(strict_bd, sc[pp][0:2 * t, 2 * t:4 * t], 0.0).astype(BF16), vs[pp])
              for pp in prs]
        for it in range(6):
            xs = [xs[pp] + _dot(n[pp], xs[pp].astype(BF16)) for pp in prs]
            if it < 5:
                n = [_dot(n[pp], n[pp]).astype(BF16) for pp in prs]
        os_ = []
        for pp in prs:
            m_r = jnp.concatenate([jnp.where(incl_bd, sc[pp][2 * t:4 * t, 0:2 * t], 0.0),
                                   jnp.where(incl_bd, sc[pp][2 * t:4 * t, 2 * t:4 * t], 0.0)], axis=1)
            uv = jnp.concatenate([xs[pp].astype(BF16), vs[pp]], axis=0)
            os_.append(proj[pp][2 * t:4 * t] + _dot(m_r.astype(BF16), uv))
        upd = []
        for pp in prs:
            u = xs[pp][0:t] + xs[pp][t:2 * t]
            c_last = cs[pp][t - 1:t]
            dec = jnp.exp(c_last - cs[pp])
            upd.append(_dot_tn(jnp.concatenate([u, v[pp]], axis=0).astype(BF16),
                               jnp.concatenate([kka_ref[sl, cols[pp]] * dec, k[pp] * dec], axis=0).astype(BF16)))
        for pp in prs:
            st_ref[pp] = st[pp] * jnp.exp(cs[pp][t - 1:t]) + jnp.where(same, upd[pp], 0.0)
        inv = 1.0 / B_HD
        o = [os_[pp][0:t] + os_[pp][t:2 * t] for pp in prs]
        mean = [head_sum(o[pp]) * inv for pp in prs]
        d = [o[pp] - mean[pp] for pp in prs]
        var = [head_sum(d[pp] * d[pp]) * inv for pp in prs]
        bonus = [head_sum(r[pp] * k[pp] * rk_ref[:, cols[pp]]) for pp in prs]
        for pp in prs:
            on = d[pp] * lax.rsqrt(var[pp] + B_GN_EPS) * lnw_ref[:, cols[pp]] + lnb_ref[:, cols[pp]]
            o_ref[sl, cols[pp]] = ((on + bonus[pp] * v[pp]) * g_ref[sl, cols[pp]]).astype(o_ref.dtype)
        return carry

    lax.fori_loop(0, n_chunks, chunk, 0)


def rwkv_recurrence(r, ld, k2, v, kk, kka, g, r_k, ln_w, ln_b, bsz, seq, s_blk=256, pairs=4):
    s_blk = min(s_blk, seq)
    width = pairs * LANES
    groups = B_W // width
    args = [a.reshape(bsz, seq, B_W) for a in (r, ld, k2, v, kk, kka, g)]
    blk = pl.BlockSpec((None, s_blk, width), lambda b, h, s: (b, s, h))
    par = pl.BlockSpec((1, width), lambda b, h, s: (0, h))
    return pl.pallas_call(
        functools.partial(_rwkv_kernel, n_chunks=s_blk // B_T, pairs=pairs),
        grid=(bsz, groups, seq // s_blk),
        in_specs=[blk] * 7 + [par] * 3,
        out_specs=blk,
        out_shape=jax.ShapeDtypeStruct((bsz, seq, B_W), BF16),
        scratch_shapes=[pltpu.VMEM((pairs, LANES, LANES), F32)],
        compiler_params=_cparams(("parallel", "parallel", "arbitrary")),
        name="rwkv_recurrence",
    )(*args, r_k.reshape(1, B_W), ln_w.reshape(1, B_W), ln_b.reshape(1, B_W))


def _rope(x, cos_t, sin_t, lane_in_head, half):
    partner = jnp.where(lane_in_head < half, pltpu.roll(x, LANES - half, axis=1), pltpu.roll(x, half, axis=1))
    return x * cos_t + partner * sin_t


def _attn_prep_kernel(q_ref, k_ref, v_ref, qi_ref, kw_ref, ca_ref, sa_ref, ci_ref, si_ref, qg_ref, kg_ref,
                      qo_ref, ko_ref, vo_ref, qio_ref, kio_ref, wo_ref, *, tm):
    ca, sa, ci, si = ca_ref[...], sa_ref[...], ci_ref[...], si_ref[...]
    lane = lax.broadcasted_iota(I32, (tm, LANES), 1)
    lane_i = lane % IDX_HD
    half_a = C_HD // ROPE_FRACTION // 2
    half_i = IDX_HD // ROPE_FRACTION // 2
    scale = C_HD ** -0.5 * LOG2_E

    def norm_rope(x, gain):
        ms = jnp.mean(x * x, axis=-1, keepdims=True)
        return _rope(x * lax.rsqrt(ms + NORM_EPS) * gain, ca, sa, lane, half_a)

    for h in range(C_HEADS):
        sl = slice(h * C_HD, (h + 1) * C_HD)
        qo_ref[:, sl] = (norm_rope(q_ref[:, sl], qg_ref[...]) * scale).astype(qo_ref.dtype)
    for h in range(C_KV_HEADS):
        sl = slice(h * C_HD, (h + 1) * C_HD)
        ko_ref[:, sl] = norm_rope(k_ref[:, sl], kg_ref[...]).astype(ko_ref.dtype)
    vo_ref[...] = v_ref[...].astype(vo_ref.dtype)
    for j in range(IDX_HEADS * IDX_HD // LANES):
        sl = slice(j * LANES, (j + 1) * LANES)
        qio_ref[:, sl] = _rope(qi_ref[:, sl], ci, si, lane_i, half_i).astype(qio_ref.dtype)
    kw = kw_ref[:, 0:LANES]
    kr = _rope(kw, ci, si, lane_i, half_i)
    kio_ref[...] = jnp.where(lane < IDX_HD, kr, pltpu.roll(kr, IDX_HD, axis=1)).astype(kio_ref.dtype)
    w = pltpu.roll(kw, LANES - IDX_HD, axis=1) * (IDX_HEADS ** -0.5 * IDX_HD ** -0.5)
    wo_ref[...] = jnp.where(lane < IDX_HEADS, w, 0.0)


def attn_prep(p, tabs, q_gain, k_gain, tm=256):
    m = p.shape[0]
    tm = min(tm, m)

    def colblk(width, off):
        return pl.BlockSpec((tm, width), lambda i: (i, off // width))

    tab = pl.BlockSpec((tm, LANES), lambda i: (i, 0))
    gain = pl.BlockSpec((1, C_HD), lambda i: (0, 0))
    kw_width = 512

    def out(width, dtype):
        return jax.ShapeDtypeStruct((m, width), dtype), pl.BlockSpec((tm, width), lambda i: (i, 0))

    outs = [out(C_W, BF16), out(C_KV_W, BF16), out(C_KV_W, BF16), out(IDX_HEADS * IDX_HD, BF16),
            out(LANES, BF16), out(LANES, F32)]
    return pl.pallas_call(
        functools.partial(_attn_prep_kernel, tm=tm),
        grid=(m // tm,),
        in_specs=[colblk(C_W, COL_C_Q), colblk(C_KV_W, COL_C_K), colblk(C_KV_W, COL_C_V),
                  colblk(IDX_HEADS * IDX_HD, COL_C_QI), colblk(kw_width, COL_C_KW), tab, tab, tab, tab, gain, gain],
        out_specs=[o[1] for o in outs],
        out_shape=[o[0] for o in outs],
        compiler_params=_cparams(("parallel",)),
        name="attn_prep",
    )(p, p, p, p, p, *tabs, q_gain.reshape(1, C_HD), k_gain.reshape(1, C_HD))


V_ONES = 16


def _attn_kernel(q_ref, k_ref, vt_ref, qi_ref, ki_ref, w_ref, o_ref, key_sc, bias_sc, m_sc, acc_sc, *, tq, tk, topk):
    qb = pl.program_id(1)
    nkb = ((qb + 1) * tq + tk - 1) // tk
    lane = lax.broadcasted_iota(I32, (tq, LANES), 1)
    w_t = w_ref[...].T
    q_chunk = (qb * tq + lax.broadcasted_iota(I32, (tk, tq), 1)) // CHUNK
    key_in_blk = lax.broadcasted_iota(I32, (tk, tq), 0)
    n_pairs = IDX_HEADS * IDX_HD // LANES
    fold = 64

    q_pairs = []
    for hp in range(n_pairs):
        qp = qi_ref[:, hp * LANES:(hp + 1) * LANES]
        zero = jnp.zeros_like(qp)
        q_pairs.append(jnp.concatenate([jnp.where(lane < IDX_HD, qp, zero), jnp.where(lane < IDX_HD, zero, qp)], axis=0))

    def score_block(kb, carry):
        c0 = pl.multiple_of(kb * tk, tk)
        ki2 = ki_ref[pl.ds(c0, tk), :]
        sc = jnp.zeros((tk, tq), F32)
        for hp in range(n_pairs):
            rel = jnp.maximum(_dot_nt(ki2, q_pairs[hp]), 0.0)
            sc = sc + w_t[2 * hp:2 * hp + 1, :] * rel[:, 0:tq] + w_t[2 * hp + 1:2 * hp + 2, :] * rel[:, tq:2 * tq]
        sc = jnp.where(sc == 0.0, 0.0, sc)
        bits = lax.bitcast_convert_type(sc, I32)
        skey = bits ^ ((bits >> 31) & 0x7FFFFFFF)
        allowed = ((c0 + key_in_blk) // CHUNK) <= q_chunk
        key_sc[kb] = jnp.where(allowed, skey, INT_MIN)
        return carry

    lax.fori_loop(0, nkb, score_block, 0)

    def count_ge(cand):
        def body(kb, acc):
            hit = jnp.where(key_sc[kb] >= cand, 1.0, 0.0)
            for j in range(tk // fold):
                acc = acc + hit[j * fold:(j + 1) * fold]
            return acc
        acc = lax.fori_loop(0, nkb, body, jnp.zeros((fold, tq), F32))
        return jnp.sum(acc, axis=0, keepdims=True)

    kf = float(topk)
    thr = jnp.where(count_ge(jnp.zeros((1, tq), I32)) >= kf, 0, INT_MIN).astype(I32)

    def bit_step(i, thr):
        cand = thr | (jnp.int32(1) << (30 - i))
        return jnp.where(count_ge(cand) >= kf, cand, thr)

    thr = lax.fori_loop(0, 31, bit_step, thr)
    thr = jnp.maximum(thr, INT_MIN + 1)
    need = kf - count_ge(thr + 1)

    tri = (lax.broadcasted_iota(I32, (tk, tk), 0) >= lax.broadcasted_iota(I32, (tk, tk), 1)).astype(BF16)

    def bias_block(kb, seen):
        keys = key_sc[kb]
        eq = keys == thr
        eq_f = jnp.where(eq, 1.0, 0.0)
        rank = _dot(tri, eq_f.astype(BF16)) + seen
        take = (keys > thr) | (eq & (rank <= need))
        bias_sc[kb] = jnp.where(take, 0.0, NEG_BIG)
        return rank[tk - 1:tk]

    lax.fori_loop(0, nkb, bias_block, jnp.zeros((1, tq), F32))

    rep = C_HEADS // C_KV_HEADS
    ve = C_HD + V_ONES
    qs = [jnp.concatenate([q_ref[:, (g * rep + r) * C_HD:(g * rep + r + 1) * C_HD] for r in range(rep)], axis=0)
          for g in range(C_KV_HEADS)]
    m_sc[...] = jnp.full(m_sc.shape, NEG_BIG, F32)
    acc_sc[...] = jnp.zeros(acc_sc.shape, F32)

    def body(kb, carry):
        c0 = pl.multiple_of(kb * tk, tk)
        bias = bias_sc[kb]
        s_all = [_dot_nt(k_ref[pl.ds(c0, tk), g * C_HD:(g + 1) * C_HD], qs[g]) for g in range(C_KV_HEADS)]
        for g in range(C_KV_HEADS):
            s = jnp.concatenate([s_all[g][:, r * tq:(r + 1) * tq] + bias for r in range(rep)], axis=1)
            m_prev = m_sc[g]
            m_new = jnp.maximum(m_prev, jnp.max(s, axis=0, keepdims=True))
            p = jnp.exp2(s - m_new)
            acc_sc[g] = jnp.exp2(m_prev - m_new) * acc_sc[g] + _dot(vt_ref[kb, g * ve:(g + 1) * ve, :], p.astype(BF16))
            m_sc[g] = m_new
        return carry

    lax.fori_loop(0, nkb, body, 0)
    for g in range(C_KV_HEADS):
        acc = acc_sc[g]
        out_t = acc[0:C_HD] / acc[C_HD:C_HD + 1]
        for r in range(rep):
            h = g * rep + r
            o_ref[:, h * C_HD:(h + 1) * C_HD] = out_t[:, r * tq:(r + 1) * tq].T.astype(o_ref.dtype)


def sparse_attention(qn, kn, vn, qi, ki2, w, bsz, seq, tq=128, tk=512):
    tk = min(tk, seq)
    topk = min(TOPK_MAX, seq // 4)
    nkb = seq // tk
    rep = C_HEADS // C_KV_HEADS
    ve = C_HD + V_ONES
    r3 = lambda a: a.reshape(bsz, seq, a.shape[-1])
    v_t = vn.reshape(bsz, nkb, tk, C_KV_HEADS, C_HD).transpose(0, 1, 3, 4, 2)
    v_t = jnp.concatenate([v_t, jnp.ones((bsz, nkb, C_KV_HEADS, V_ONES, tk), v_t.dtype)], axis=3)
    v_t = v_t.reshape(bsz, nkb, C_KV_HEADS * ve, tk)
    qblk = lambda width: pl.BlockSpec((None, tq, width), lambda b, i: (b, i, 0))
    sblk = lambda width: pl.BlockSpec((None, seq, width), lambda b, i: (b, 0, 0))
    return pl.pallas_call(
        functools.partial(_attn_kernel, tq=tq, tk=tk, topk=topk),
        grid=(bsz, seq // tq),
        in_specs=[qblk(C_W), sblk(C_KV_W), pl.BlockSpec((None, nkb, C_KV_HEADS * ve, tk), lambda b, i: (b, 0, 0, 0)),
                  qblk(IDX_HEADS * IDX_HD), sblk(LANES), qblk(LANES)],
        out_specs=qblk(C_W),
        out_shape=jax.ShapeDtypeStruct((bsz, seq, C_W), BF16),
        scratch_shapes=[pltpu.VMEM((nkb, tk, tq), I32), pltpu.VMEM((nkb, tk, tq), F32),
                        pltpu.VMEM((C_KV_HEADS, 1, rep * tq), F32), pltpu.VMEM((C_KV_HEADS, ve, rep * tq), F32)],
        compiler_params=_cparams(("parallel", "arbitrary")),
        name="sparse_attention",
    )(r3(qn), r3(kn), v_t, r3(qi), r3(ki2), r3(w))


def _rope_tables(positions):
    pos = positions.astype(F32).reshape(-1, 1)

    def tables(head_dim):
        rot = head_dim // ROPE_FRACTION
        half = rot // 2
        inv_freq = ROPE_THETA ** (-jnp.arange(0, rot, 2, dtype=F32) / rot)
        ang = pos * inv_freq
        cos, sin = jnp.cos(ang), jnp.sin(ang)
        ones = jnp.ones((pos.shape[0], head_dim - rot), F32)
        c = jnp.concatenate([cos, cos, ones], axis=1)
        s = jnp.concatenate([-sin, sin, 0.0 * ones], axis=1)
        reps = LANES // head_dim
        return jnp.tile(c, (1, reps)), jnp.tile(s, (1, reps))

    ca, sa = tables(C_HD)
    ci, si = tables(IDX_HD)
    return ca, sa, ci, si


def _pad_rows(a, before, total):
    return jnp.zeros((total, a.shape[1]), a.dtype).at[before:before + a.shape[0]].set(a)


def kernel(x, positions, norm_mix, w_in, hgrn_lb_logits, hgrn_out_norm, rwkv_mu, rwkv_w0, rwkv_w2, rwkv_a0, rwkv_a2,
           rwkv_g2, rwkv_k_k, rwkv_k_a, rwkv_r_k, rwkv_ln_w, rwkv_ln_b, q_norm, k_norm, w_branch_a, w_branch_b,
           w_branch_c, w_out, norm_ffn, w_up, conv_w, conv_b, w_down):
    bsz, seq, d_model = x.shape
    depth = w_in.shape[0]
    m = bsz * seq
    tabs = _rope_tables(positions)
    lb_all = jnp.cumsum(jax.nn.softmax(hgrn_lb_logits.astype(F32), axis=0), axis=0)
    lb_all = lb_all - lb_all[0:1]
    n_lora = B_LORA_DECAY + B_LORA_A + B_LORA_G
    ab_cols = 4 * A_W + 3 * B_W + n_lora
    c_cols = C_W + 2 * C_KV_W + IDX_HEADS * IDX_HD + IDX_HD + IDX_HEADS
    w_in_ab = pack_cols(w_in, 0, ab_cols, W_AB)
    w_in_c = pack_cols(w_in, ab_cols, c_cols, W_C)
    w_in_g = pack_cols(w_in, ab_cols + c_cols, 3 * d_model, 3 * d_model)
    w_a, w_b, w_c = w_branch_a.astype(BF16), w_branch_b.astype(BF16), w_branch_c.astype(BF16)
    w_o, w_d = w_out.astype(BF16), w_down.astype(BF16)

    xf = x.reshape(m, d_model)
    for l in range(depth):
        h = rmsnorm(xf, norm_mix[l])
        p = proj_in(h, w_in_ab, l, F32)
        p_c = proj_in(h, w_in_c, l, F32)
        p_g = proj_in(h, w_in_g, l, BF16)

        o_a = hgrn2(p.reshape(bsz, seq, -1), lb_all[l], hgrn_out_norm[l]).reshape(m, A_W)

        mu = rwkv_mu[l]
        mu4 = jnp.stack([mu[0:B_W], mu[B_W:2 * B_W], mu[2 * B_W:3 * B_W],
                         jnp.pad(mu[3 * B_W:], (0, B_W - n_lora))])
        w2p = _pad_rows(rwkv_w2[l], 0, LANES).astype(BF16)
        a2p = _pad_rows(rwkv_a2[l], B_LORA_DECAY, LANES).astype(BF16)
        g2p = _pad_rows(rwkv_g2[l], 0, 2 * LANES).astype(BF16)
        parts = rwkv_prep(p, seq, mu4, rwkv_w0[l], w2p, rwkv_a0[l], a2p, g2p, rwkv_k_k[l], rwkv_k_a[l])
        o_b = rwkv_recurrence(*parts, rwkv_r_k[l], rwkv_ln_w[l], rwkv_ln_b[l], bsz, seq).reshape(m, B_W)

        qn, kn, vn, qi, ki2, wi = attn_prep(p_c, tabs, q_norm[l], k_norm[l])
        o_c = sparse_attention(qn, kn, vn, qi, ki2, wi, bsz, seq).reshape(m, C_W)

        merged = merge_branches(o_a, o_b, o_c, w_a, w_b, w_c, l, p_g, d_model)
        xf = matmul_residual(merged, w_o, l, xf, tm=1024, tn=512)

        h2 = rmsnorm(xf, norm_ffn[l])
        act = ffn_up(h2, w_up, l, conv_w[l], conv_b[l].reshape(1, -1), seq)
        xf = matmul_residual(act, w_d, l, xf, tm=512, tn=512)
    return xf.reshape(bsz, seq, d_model)
```

```python
import functools

import jax
import jax.numpy as jnp
import numpy as np
from jax import lax
from jax.experimental import pallas as pl
from jax.experimental.pallas import tpu as pltpu

F32 = jnp.float32
BF16 = jnp.bfloat16
I32 = jnp.int32

CHUNK = 64
ROPE_THETA = 500000.0
ROPE_FRACTION = 4
NORM_EPS = 1e-6
A_HEADS, A_DK, A_DV = 8, 128, 128
A_W = A_HEADS * A_DV
B_HEADS, B_HD = 16, 64
B_W = B_HEADS * B_HD
B_LORA_DECAY, B_LORA_A, B_LORA_G = 64, 64, 160
B_GN_EPS = 64e-5
C_HEADS, C_KV_HEADS, C_HD = 16, 4, 128
C_W = C_HEADS * C_HD
C_KV_W = C_KV_HEADS * C_HD
IDX_HEADS, IDX_HD = 16, 64
TOPK_MAX = 256
CONV_W = 3

LANES = 128
SUBLANES = 8
VMEM_LIMIT = 56 * 1024 * 1024

COL_A = 0
COL_B_RKV = 4096
COL_B_LORA = 7168
W_AB = 8192
COL_C_Q = 0
COL_C_K = 2048
COL_C_V = 2560
COL_C_QI = 3072
COL_C_KW = 4096
W_C = 4608
LOG2_E = 1.4426950408889634
NEG_BIG = -1e30
INT_MIN = -2147483648


def _sigmoid(x):
    return 1.0 / (1.0 + jnp.exp(-x))


def _dot(a, b):
    return jnp.dot(a, b, preferred_element_type=F32)


def _dot_nt(a, b):
    return lax.dot_general(a, b, (((1,), (1,)), ((), ())), preferred_element_type=F32)


def _dot_tn(a, b):
    return lax.dot_general(a, b, (((0,), (0,)), ((), ())), preferred_element_type=F32)


def _split3(x):
    hi = x.astype(BF16)
    r1 = x - hi.astype(F32)
    mid = r1.astype(BF16)
    lo = (r1 - mid.astype(F32)).astype(BF16)
    return hi, mid, lo


def _dot_exact_lhs(m_bf16, x):
    hi, mid, lo = _split3(x)
    return _dot(m_bf16, hi) + _dot(m_bf16, mid) + _dot(m_bf16, lo)


def _dot_exact_rhs(x, m_bf16):
    hi, mid, lo = _split3(x)
    return _dot(hi, m_bf16) + _dot(mid, m_bf16) + _dot(lo, m_bf16)


def _split2(x):
    hi = x.astype(BF16)
    lo = (x - hi.astype(F32)).astype(BF16)
    return hi, lo


def _dot_hp(a, b):
    ah, al = _split2(a)
    bh, bl = _split2(b)
    return _dot(ah, bh) + _dot(ah, bl) + _dot(al, bh)


def _dot_nt_hp(a, b):
    ah, al = _split2(a)
    bh, bl = _split2(b)
    return _dot_nt(ah, bh) + _dot_nt(ah, bl) + _dot_nt(al, bh)


def _dot_tn_hp(a, b):
    ah, al = _split2(a)
    bh, bl = _split2(b)
    return _dot_tn(ah, bh) + _dot_tn(ah, bl) + _dot_tn(al, bh)


def _cparams(sem):
    return pltpu.CompilerParams(dimension_semantics=sem, vmem_limit_bytes=VMEM_LIMIT)


def _rmsnorm_kernel(x_ref, g_ref, o_ref):
    x = x_ref[...]
    ms = jnp.mean(x * x, axis=-1, keepdims=True)
    o_ref[...] = (x * lax.rsqrt(ms + NORM_EPS) * g_ref[...]).astype(o_ref.dtype)


def rmsnorm(x, gain, tm=256):
    m, d = x.shape
    return pl.pallas_call(
        _rmsnorm_kernel,
        grid=(m // tm,),
        in_specs=[pl.BlockSpec((tm, d), lambda i: (i, 0)), pl.BlockSpec((1, d), lambda i: (0, 0))],
        out_specs=pl.BlockSpec((tm, d), lambda i: (i, 0)),
        out_shape=jax.ShapeDtypeStruct((m, d), BF16),
        compiler_params=_cparams(("parallel",)),
        name="rmsnorm",
    )(x, gain.reshape(1, d))


def _wspec(w, layer, tn, jmap):
    return pl.BlockSpec((None, w.shape[1], tn), lambda *g: (layer, 0, jmap(*g)))


def _proj_in_kernel(h_ref, w_ref, o_ref):
    o_ref[...] = _dot(h_ref[...], w_ref[...]).astype(o_ref.dtype)


def proj_in(h, w, layer, out_dtype, tm=1024, tn=512):
    m, k = h.shape
    n = w.shape[2]
    tm = min(tm, m)
    return pl.pallas_call(
        _proj_in_kernel,
        grid=(m // tm, n // tn),
        in_specs=[pl.BlockSpec((tm, k), lambda i, j: (i, 0)), _wspec(w, layer, tn, lambda i, j: j)],
        out_specs=pl.BlockSpec((tm, tn), lambda i, j: (i, j)),
        out_shape=jax.ShapeDtypeStruct((m, n), out_dtype),
        compiler_params=_cparams(("parallel", "arbitrary")),
        name="proj_in",
    )(h, w)


def _pack_cols_kernel(*refs, shift, width):
    a_ref, o_ref = refs[0], refs[-1]
    t = pl.program_id(1)
    a = a_ref[...]
    if shift:
        a = jnp.concatenate([a[shift:], refs[1][0:shift]], axis=0)
    col = t * LANES + lax.broadcasted_iota(I32, a.shape, 0)
    o_ref[...] = jnp.where(col < width, a, 0.0).T.astype(o_ref.dtype)


def pack_cols(w_t, src_start, width, dst_width):
    depth, n_src, k = w_t.shape
    q0, shift = divmod(src_start, LANES)
    assert shift % SUBLANES == 0
    last = (n_src - 1) // LANES

    def src(extra):
        return pl.BlockSpec((None, LANES, k), lambda l, t: (l, jnp.minimum(q0 + t + extra, last), 0))

    w = w_t
    srcs = [src(0), src(1)] if shift else [src(0)]
    return pl.pallas_call(
        functools.partial(_pack_cols_kernel, shift=shift, width=width),
        grid=(depth, dst_width // LANES),
        in_specs=srcs,
        out_specs=pl.BlockSpec((None, k, LANES), lambda l, t: (l, 0, t)),
        out_shape=jax.ShapeDtypeStruct((depth, k, dst_width), BF16),
        compiler_params=_cparams(("parallel", "arbitrary")),
        name="pack_cols",
    )(*([w] * len(srcs)))


def _mm_res_kernel(a_ref, b_ref, r_ref, o_ref):
    o_ref[...] = r_ref[...] + _dot(a_ref[...], b_ref[...])


def matmul_residual(a, b, layer, res, tm=512, tn=512):
    m, k = a.shape
    n = b.shape[2]
    tm = min(tm, m)
    tn = min(tn, n)
    return pl.pallas_call(
        _mm_res_kernel,
        grid=(m // tm, n // tn),
        in_specs=[
            pl.BlockSpec((tm, k), lambda i, j: (i, 0)),
            _wspec(b, layer, tn, lambda i, j: j),
            pl.BlockSpec((tm, tn), lambda i, j: (i, j)),
        ],
        out_specs=pl.BlockSpec((tm, tn), lambda i, j: (i, j)),
        out_shape=jax.ShapeDtypeStruct((m, n), F32),
        compiler_params=_cparams(("parallel", "arbitrary")),
        name="matmul_residual",
    )(a, b, res)


def _merge_kernel(oa_ref, ob_ref, oc_ref, wa_ref, wb_ref, wc_ref, ga_ref, gb_ref, gc_ref, o_ref):
    acc = _sigmoid(ga_ref[...].astype(F32)) * _dot(oa_ref[...], wa_ref[...])
    acc += _sigmoid(gb_ref[...].astype(F32)) * _dot(ob_ref[...], wb_ref[...])
    acc += _sigmoid(gc_ref[...].astype(F32)) * _dot(oc_ref[...], wc_ref[...])
    o_ref[...] = acc.astype(o_ref.dtype)


def merge_branches(o_a, o_b, o_c, w_a, w_b, w_c, layer, p, d_model, tm=1024, tn=512):
    m = o_a.shape[0]
    tm = min(tm, m)
    g0 = 0
    gstep = d_model // tn
    return pl.pallas_call(
        _merge_kernel,
        grid=(m // tm, d_model // tn),
        in_specs=[
            pl.BlockSpec((tm, o_a.shape[1]), lambda i, j: (i, 0)),
            pl.BlockSpec((tm, o_b.shape[1]), lambda i, j: (i, 0)),
            pl.BlockSpec((tm, o_c.shape[1]), lambda i, j: (i, 0)),
            _wspec(w_a, layer, tn, lambda i, j: j),
            _wspec(w_b, layer, tn, lambda i, j: j),
            _wspec(w_c, layer, tn, lambda i, j: j),
            pl.BlockSpec((tm, tn), lambda i, j: (i, g0 + j)),
            pl.BlockSpec((tm, tn), lambda i, j: (i, g0 + gstep + j)),
            pl.BlockSpec((tm, tn), lambda i, j: (i, g0 + 2 * gstep + j)),
        ],
        out_specs=pl.BlockSpec((tm, tn), lambda i, j: (i, j)),
        out_shape=jax.ShapeDtypeStruct((m, d_model), BF16),
        compiler_params=_cparams(("parallel", "arbitrary")),
        name="merge_branches",
    )(o_a, o_b, o_c, w_a, w_b, w_c, p, p, p)


def _ffn_up_kernel(h_ref, wg_ref, wu_ref, cwg_ref, cwu_ref, cbg_ref, cbu_ref, o_ref, w_sc, cg_ref, cu_ref, *,
                   tiles_per_seq, tm, ts):
    i = pl.program_id(1)
    first = (i % tiles_per_seq) == 0
    tn = o_ref.shape[1]
    rows = lax.broadcasted_iota(I32, (ts, tn), 0)

    @pl.when(i == 0)
    def _():
        w_sc[:, 0:tn] = wg_ref[...].astype(BF16)
        w_sc[:, tn:2 * tn] = wu_ref[...].astype(BF16)

    @pl.when(first)
    def _():
        cg_ref[...] = jnp.zeros_like(cg_ref)
        cu_ref[...] = jnp.zeros_like(cu_ref)

    def conv(u, prev, cw_ref, cb_ref):
        u1 = jnp.where(rows == 0, prev[7:8, :], pltpu.roll(u, 1, axis=0))
        u2 = pltpu.roll(u, 2, axis=0)
        u2 = jnp.where(rows == 0, prev[6:7, :], jnp.where(rows == 1, prev[7:8, :], u2))
        cw = cw_ref[...]
        return cw[0:1, :] * u2 + cw[1:2, :] * u1 + cw[2:3, :] * u + cb_ref[...]

    prev_g = cg_ref[...]
    prev_u = cu_ref[...]
    for sb in range(tm // ts):
        u = _dot(h_ref[sb * ts:(sb + 1) * ts, :], w_sc[...])
        ug = u[:, 0:tn]
        uu = u[:, tn:2 * tn]
        gate = conv(ug, prev_g, cwg_ref, cbg_ref)
        up = conv(uu, prev_u, cwu_ref, cbu_ref)
        o_ref[sb * ts:(sb + 1) * ts, :] = (gate * _sigmoid(gate) * up).astype(o_ref.dtype)
        prev_g = ug[ts - SUBLANES:ts, :]
        prev_u = uu[ts - SUBLANES:ts, :]
    cg_ref[...] = prev_g
    cu_ref[...] = prev_u


def ffn_up(h, w_up, layer, conv_w, conv_b, seq, tm=1024, tn=256, ts=256):
    m, k = h.shape
    d_ff = w_up.shape[2] // 2
    tm = min(tm, seq)
    ts = min(ts, tm)
    nj = d_ff // tn
    return pl.pallas_call(
        functools.partial(_ffn_up_kernel, tiles_per_seq=seq // tm, tm=tm, ts=ts),
        grid=(nj, m // tm),
        in_specs=[
            pl.BlockSpec((tm, k), lambda j, i: (i, 0)),
            _wspec(w_up, layer, tn, lambda j, i: j),
            _wspec(w_up, layer, tn, lambda j, i: nj + j),
            pl.BlockSpec((CONV_W, tn), lambda j, i: (0, j)),
            pl.BlockSpec((CONV_W, tn), lambda j, i: (0, nj + j)),
            pl.BlockSpec((1, tn), lambda j, i: (0, j)),
            pl.BlockSpec((1, tn), lambda j, i: (0, nj + j)),
        ],
        out_specs=pl.BlockSpec((tm, tn), lambda j, i: (i, j)),
        out_shape=jax.ShapeDtypeStruct((m, d_ff), BF16),
        scratch_shapes=[pltpu.VMEM((k, 2 * tn), BF16), pltpu.VMEM((SUBLANES, tn), F32),
                        pltpu.VMEM((SUBLANES, tn), F32)],
        compiler_params=_cparams(("parallel", "arbitrary")),
        name="ffn_up",
    )(h, w_up, w_up, conv_w, conv_w, conv_b, conv_b)


A_SUB = 16


def _hgrn_kernel(q_ref, f_ref, i_ref, g_ref, lb_ref, gain_ref, o_ref, st_ref, *, n_chunks, heads):
    @pl.when(pl.program_id(2) == 0)
    def _():
        st_ref[...] = jnp.zeros_like(st_ref)

    gain = gain_ref[...]
    tri = (lax.broadcasted_iota(I32, (CHUNK, CHUNK), 0) >= lax.broadcasted_iota(I32, (CHUNK, CHUNK), 1)).astype(BF16)
    rows_sub = lax.broadcasted_iota(I32, (A_SUB, A_DK), 0)
    n_sub = CHUNK // A_SUB

    def chunk(c, carry):
        rows = pl.ds(pl.multiple_of(c * CHUNK, CHUNK), CHUNK)
        hs = range(heads)
        cols = [slice(hh * A_DK, (hh + 1) * A_DK) for hh in hs]
        iv = [i_ref[rows, cl] for cl in cols]
        iv_b = [x.astype(BF16) for x in iv]
        st = [st_ref[hh] for hh in hs]
        qf, kin, b = [], [], []
        for hh in hs:
            lb = lb_ref[:, cols[hh]]
            z = f_ref[rows, cols[hh]]
            qv = q_ref[rows, cols[hh]]
            qf.append(qv * _sigmoid(qv))
            kin.append((1.0 - lb) * _sigmoid(-z))
            b.append(_dot_exact_lhs(tri, jnp.log(lb + (1.0 - lb) * _sigmoid(z))))
        o_inter = [_dot_nt((qf[hh] * jnp.exp(b[hh])).astype(BF16), st[hh].astype(BF16)) for hh in hs]
        outs = [[] for _ in hs]
        for si in range(n_sub):
            lo = si * A_SUB
            sub = slice(lo, lo + A_SUB)
            o_i = [o_inter[hh][sub] for hh in hs]
            if si > 0:
                att = []
                for hh in hs:
                    b_ref_row = b[hh][lo - 1:lo]
                    q_s = (qf[hh][sub] * jnp.exp(b[hh][sub] - b_ref_row)).astype(BF16)
                    k_s = (kin[hh][0:lo] * jnp.exp(b_ref_row - b[hh][0:lo])).astype(BF16)
                    att.append(_dot_nt(q_s, k_s))
                o_i = [o_i[hh] + _dot(att[hh].astype(BF16), iv_b[hh][0:lo]) for hh in hs]
            for s in range(A_SUB):
                for hh in hs:
                    b_i = b[hh][sub]
                    d = jnp.exp(jnp.where(rows_sub >= s, b_i - b_i[s:s + 1], -jnp.inf))
                    a = jnp.sum(qf[hh][sub] * d * kin[hh][lo + s:lo + s + 1], axis=1, keepdims=True)
                    o_i[hh] = o_i[hh] + a * iv[hh][lo + s:lo + s + 1]
            for hh in hs:
                outs[hh].append(o_i[hh])
        for hh in hs:
            b_last = b[hh][CHUNK - 1:CHUNK]
            k_dec = (kin[hh] * jnp.exp(b_last - b[hh])).astype(BF16)
            st_ref[hh] = st[hh] * jnp.exp(b_last) + _dot_tn(iv_b[hh], k_dec)
        for hh in hs:
            o = jnp.concatenate(outs[hh], axis=0)
            gv = g_ref[rows, cols[hh]]
            ms = jnp.mean(o * o, axis=-1, keepdims=True)
            on = o * lax.rsqrt(ms + NORM_EPS) * gain
            o_ref[rows, cols[hh]] = (on * (gv * _sigmoid(gv))).astype(o_ref.dtype)
        return carry

    lax.fori_loop(0, n_chunks, chunk, 0)


def hgrn2(p3, lb, out_gain, s_blk=256, heads=4):
    bsz, seq, _ = p3.shape
    s_blk = min(s_blk, seq)
    width = heads * A_DK
    groups = A_HEADS // heads
    c0 = COL_A // width

    def col(part):
        return pl.BlockSpec((None, s_blk, width), lambda b, h, s, part=part: (b, s, c0 + part * groups + h))

    return pl.pallas_call(
        functools.partial(_hgrn_kernel, n_chunks=s_blk // CHUNK, heads=heads),
        grid=(bsz, groups, seq // s_blk),
        in_specs=[col(0), col(1), col(2), col(3),
                  pl.BlockSpec((1, width), lambda b, h, s: (0, h)),
                  pl.BlockSpec((1, A_DV), lambda b, h, s: (0, 0))],
        out_specs=pl.BlockSpec((None, s_blk, width), lambda b, h, s: (b, s, h)),
        out_shape=jax.ShapeDtypeStruct((bsz, seq, A_W), BF16),
        scratch_shapes=[pltpu.VMEM((heads, A_DV, A_DK), F32)],
        compiler_params=_cparams(("parallel", "parallel", "arbitrary")),
        name="hgrn2",
    )(p3, p3, p3, p3, lb.reshape(1, A_HEADS * A_DK), out_gain.reshape(1, A_DV))


def _rwkv_prep_kernel(r_ref, k_ref, v_ref, l_ref, rp_ref, kp_ref, vp_ref, lp_ref,
                      mu_ref, w0_ref, w2_ref, a0_ref, a2_ref, g2_ref, kk_ref, ka_ref,
                      ro_ref, ld_ref, k2_ref, vo_ref, kko_ref, kka_ref, go_ref, *, tiles_per_seq, tm):
    first = (pl.program_id(0) % tiles_per_seq) == 0
    rows = lax.broadcasted_iota(I32, (tm, B_W), 0)

    def shifted(cur_ref, prev_ref, part):
        cur = cur_ref[...]
        prev = jnp.where(first, 0.0, prev_ref[...])[SUBLANES - 1:SUBLANES, :]
        sh = jnp.where(rows == 0, prev, pltpu.roll(cur, 1, axis=0))
        return cur + (sh - cur) * mu_ref[part:part + 1, :]

    r = shifted(r_ref, rp_ref, 0)
    k = shifted(k_ref, kp_ref, 1)
    v = shifted(v_ref, vp_ref, 2)
    lo = shifted(l_ref, lp_ref, 3)
    lo_a = lo[:, 0:LANES]
    lo_g = lo[:, LANES:3 * LANES]
    wpre = w0_ref[...] + _dot(jnp.tanh(lo_a).astype(BF16), w2_ref[...])
    y = -wpre
    softplus = jnp.maximum(y, 0.0) + jnp.log(1.0 + jnp.exp(-jnp.abs(y)))
    w_log = -softplus - 0.5
    ld_ref[...] = -jnp.exp(w_log)
    a = _sigmoid(a0_ref[...] + _dot(lo_a.astype(BF16), a2_ref[...]))
    go_ref[...] = _dot(_sigmoid(lo_g).astype(BF16), g2_ref[...])
    kk = k * kk_ref[...]
    bd = (lax.broadcasted_iota(I32, (LANES, LANES), 0) // B_HD
          == lax.broadcasted_iota(I32, (LANES, LANES), 1) // B_HD).astype(BF16)
    sq = kk * kk
    ss = jnp.concatenate([_dot_exact_rhs(sq[:, j * LANES:(j + 1) * LANES], bd) for j in range(B_W // LANES)], axis=1)
    kk = kk / jnp.maximum(jnp.sqrt(ss), 1e-12)
    ro_ref[...] = r
    vo_ref[...] = v
    kko_ref[...] = kk
    kka_ref[...] = kk * a
    k2_ref[...] = k * (1.0 + (a - 1.0) * ka_ref[...])


def rwkv_prep(p, seq, mu4, w0, w2p, a0, a2p, g2p, k_k, k_a, tm=256):
    m = p.shape[0]
    tm = min(tm, seq)
    cb = COL_B_RKV // B_W
    pb = tm // SUBLANES

    def cur(j):
        return pl.BlockSpec((tm, B_W), lambda i, j=j: (i, cb + j))

    def prev(j):
        return pl.BlockSpec((SUBLANES, B_W), lambda i, j=j: (jnp.maximum(i * pb - 1, 0), cb + j))

    def full(a):
        return pl.BlockSpec(a.shape, lambda i: (0, 0))

    row = lambda a: a.reshape(1, B_W)
    params = [mu4, row(w0), w2p, row(a0), a2p, g2p, row(k_k), row(k_a)]
    out = jax.ShapeDtypeStruct((m, B_W), F32)
    return pl.pallas_call(
        functools.partial(_rwkv_prep_kernel, tiles_per_seq=seq // tm, tm=tm),
        grid=(m // tm,),
        in_specs=[cur(0), cur(1), cur(2), cur(3), prev(0), prev(1), prev(2), prev(3)] + [full(a) for a in params],
        out_specs=[pl.BlockSpec((tm, B_W), lambda i: (i, 0))] * 7,
        out_shape=[out] * 7,
        compiler_params=_cparams(("parallel",)),
        name="rwkv_prep",
    )(p, p, p, p, p, p, p, p, *params)


B_T = 64


def _rwkv_kernel(r_ref, ld_ref, k_ref, v_ref, kk_ref, kka_ref, g_ref, rk_ref, lnw_ref, lnb_ref, o_ref, st_ref, *,
                 n_chunks, pairs):
    @pl.when(pl.program_id(2) == 0)
    def _():
        st_ref[...] = jnp.zeros_like(st_ref)

    t = B_T
    ii = lax.broadcasted_iota(I32, (t, t), 0)
    jj = lax.broadcasted_iota(I32, (t, t), 1)
    tri = (ii >= jj).astype(BF16)
    i2 = lax.broadcasted_iota(I32, (2 * t, 2 * t), 0)
    j2 = lax.broadcasted_iota(I32, (2 * t, 2 * t), 1)
    same = (i2 // t) == (j2 // t)
    strict_bd = same & ((i2 % t) > (j2 % t))
    incl_bd = same & ((i2 % t) >= (j2 % t))
    head_bd = same.astype(BF16)
    lane = lax.broadcasted_iota(I32, (t, LANES), 1)
    h0 = lane < B_HD

    def stack(x):
        return jnp.concatenate([jnp.where(h0, x, jnp.zeros_like(x)), jnp.where(h0, jnp.zeros_like(x), x)], axis=0)

    def head_sum(x):
        hi, lo = _split2(x)
        return _dot(hi, head_bd) + _dot(lo, head_bd)

    def chunk(c, carry):
        sl = pl.ds(pl.multiple_of(c * t, t), t)
        prs = range(pairs)
        cols = [slice(pp * LANES, (pp + 1) * LANES) for pp in prs]
        r = [r_ref[sl, cl] for cl in cols]
        ld = [ld_ref[sl, cl] for cl in cols]
        k = [k_ref[sl, cl] for cl in cols]
        v = [v_ref[sl, cl] for cl in cols]
        cs = []
        for pp in prs:
            ld_hi, ld_lo = _split2(ld[pp])
            cs.append(_dot(tri, ld_hi) + _dot(tri, ld_lo))
        lhs, rhs, vs = [], [], []
        for pp in prs:
            e_neg = jnp.exp(-cs[pp])
            kka = kka_ref[sl, cols[pp]]
            a_t = (-kk_ref[sl, cols[pp]] * jnp.exp(cs[pp] - ld[pp])).astype(BF16)
            r_t = (r[pp] * jnp.exp(cs[pp])).astype(BF16)
            b_t = (kka * e_neg).astype(BF16)
            k_t = (k[pp] * e_neg).astype(BF16)
            lhs.append(jnp.concatenate([stack(a_t), stack(r_t)], axis=0))
            rhs.append(jnp.concatenate([b_t, b_t, k_t, k_t], axis=0))
            vs.append(stack(v[pp].astype(BF16)))
        sc = [_dot_nt(lhs[pp], rhs[pp]) for pp in prs]
        st = [st_ref[pp] for pp in prs]
        proj = [_dot_nt(lhs[pp], st[pp].astype(BF16)) for pp in prs]
        n = [jnp.where(strict_bd, sc[pp][0:2 * t, 0:2 * t], 0.0).astype(BF16) for pp in prs]
        xs = [proj[pp][0:2 * t] + _dot(jnp.where(strict_bd, sc[pp][0:2 * t, 2 * t:4 * t], 0.0).astype(BF16), vs[pp])
              for pp in prs]
        for it in range(6):
            xs = [xs[pp] + _dot(n[pp], xs[pp].astype(BF16)) for pp in prs]
            if it < 5:
                n = [_dot(n[pp], n[pp]).astype(BF16) for pp in prs]
        os_ = []
        for pp in prs:
            m_r = jnp.concatenate([jnp.where(incl_bd, sc[pp][2 * t:4 * t, 0:2 * t], 0.0),
                                   jnp.where(incl_bd, sc[pp][2 * t:4 * t, 2 * t:4 * t], 0.0)], axis=1)
            uv = jnp.concatenate([xs[pp].astype(BF16), vs[pp]], axis=0)
            os_.append(proj[pp][2 * t:4 * t] + _dot(m_r.astype(BF16), uv))
        upd = []
        for pp in prs:
            u = xs[pp][0:t] + xs[pp][t:2 * t]
            c_last = cs[pp][t - 1:t]
            dec = jnp.exp(c_last - cs[pp])
            upd.append(_dot_tn(jnp.concatenate([u, v[pp]], axis=0).astype(BF16),
                               jnp.concatenate([kka_ref[sl, cols[pp]] * dec, k[pp] * dec], axis=0).astype(BF16)))
        for pp in prs:
            st_ref[pp] = st[pp] * jnp.exp(cs[pp][t - 1:t]) + jnp.where(same, upd[pp], 0.0)
        inv = 1.0 / B_HD
        o = [os_[pp][0:t] + os_[pp][t:2 * t] for pp in prs]
        mean = [head_sum(o[pp]) * inv for pp in prs]
        d = [o[pp] - mean[pp] for pp in prs]
        var = [head_sum(d[pp] * d[pp]) * inv for pp in prs]
        bonus = [head_sum(r[pp] * k[pp] * rk_ref[:, cols[pp]]) for pp in prs]
        for pp in prs:
            on = d[pp] * lax.rsqrt(var[pp] + B_GN_EPS) * lnw_ref[:, cols[pp]] + lnb_ref[:, cols[pp]]
            o_ref[sl, cols[pp]] = ((on + bonus[pp] * v[pp]) * g_ref[sl, cols[pp]]).astype(o_ref.dtype)
        return carry

    lax.fori_loop(0, n_chunks, chunk, 0)


def rwkv_recurrence(r, ld, k2, v, kk, kka, g, r_k, ln_w, ln_b, bsz, seq, s_blk=256, pairs=4):
    s_blk = min(s_blk, seq)
    width = pairs * LANES
    groups = B_W // width
    args = [a.reshape(bsz, seq, B_W) for a in (r, ld, k2, v, kk, kka, g)]
    blk = pl.BlockSpec((None, s_blk, width), lambda b, h, s: (b, s, h))
    par = pl.BlockSpec((1, width), lambda b, h, s: (0, h))
    return pl.pallas_call(
        functools.partial(_rwkv_kernel, n_chunks=s_blk // B_T, pairs=pairs),
        grid=(bsz, groups, seq // s_blk),
        in_specs=[blk] * 7 + [par] * 3,
        out_specs=blk,
        out_shape=jax.ShapeDtypeStruct((bsz, seq, B_W), BF16),
        scratch_shapes=[pltpu.VMEM((pairs, LANES, LANES), F32)],
        compiler_params=_cparams(("parallel", "parallel", "arbitrary")),
        name="rwkv_recurrence",
    )(*args, r_k.reshape(1, B_W), ln_w.reshape(1, B_W), ln_b.reshape(1, B_W))


def _rope(x, cos_t, sin_t, lane_in_head, half):
    partner = jnp.where(lane_in_head < half, pltpu.roll(x, LANES - half, axis=1), pltpu.roll(x, half, axis=1))
    return x * cos_t + partner * sin_t


def _attn_prep_kernel(q_ref, k_ref, v_ref, qi_ref, kw_ref, ca_ref, sa_ref, ci_ref, si_ref, qg_ref, kg_ref,
                      qo_ref, ko_ref, vo_ref, qio_ref, kio_ref, wo_ref, *, tm):
    ca, sa, ci, si = ca_ref[...], sa_ref[...], ci_ref[...], si_ref[...]
    lane = lax.broadcasted_iota(I32, (tm, LANES), 1)
    lane_i = lane % IDX_HD
    half_a = C_HD // ROPE_FRACTION // 2
    half_i = IDX_HD // ROPE_FRACTION // 2
    scale = C_HD ** -0.5 * LOG2_E

    def norm_rope(x, gain):
        ms = jnp.mean(x * x, axis=-1, keepdims=True)
        return _rope(x * lax.rsqrt(ms + NORM_EPS) * gain, ca, sa, lane, half_a)

    for h in range(C_HEADS):
        sl = slice(h * C_HD, (h + 1) * C_HD)
        qo_ref[:, sl] = (norm_rope(q_ref[:, sl], qg_ref[...]) * scale).astype(qo_ref.dtype)
    for h in range(C_KV_HEADS):
        sl = slice(h * C_HD, (h + 1) * C_HD)
        ko_ref[:, sl] = norm_rope(k_ref[:, sl], kg_ref[...]).astype(ko_ref.dtype)
    vo_ref[...] = v_ref[...].astype(vo_ref.dtype)
    for j in range(IDX_HEADS * IDX_HD // LANES):
        sl = slice(j * LANES, (j + 1) * LANES)
        qio_ref[:, sl] = _rope(qi_ref[:, sl], ci, si, lane_i, half_i).astype(qio_ref.dtype)
    kw = kw_ref[:, 0:LANES]
    kr = _rope(kw, ci, si, lane_i, half_i)
    kio_ref[...] = jnp.where(lane < IDX_HD, kr, pltpu.roll(kr, IDX_HD, axis=1)).astype(kio_ref.dtype)
    w = pltpu.roll(kw, LANES - IDX_HD, axis=1) * (IDX_HEADS ** -0.5 * IDX_HD ** -0.5)
    wo_ref[...] = jnp.where(lane < IDX_HEADS, w, 0.0)


def attn_prep(p, tabs, q_gain, k_gain, tm=256):
    m = p.shape[0]
    tm = min(tm, m)

    def colblk(width, off):
        return pl.BlockSpec((tm, width), lambda i: (i, off // width))

    tab = pl.BlockSpec((tm, LANES), lambda i: (i, 0))
    gain = pl.BlockSpec((1, C_HD), lambda i: (0, 0))
    kw_width = 512

    def out(width, dtype):
        return jax.ShapeDtypeStruct((m, width), dtype), pl.BlockSpec((tm, width), lambda i: (i, 0))

    outs = [out(C_W, BF16), out(C_KV_W, BF16), out(C_KV_W, BF16), out(IDX_HEADS * IDX_HD, BF16),
            out(LANES, BF16), out(LANES, F32)]
    return pl.pallas_call(
        functools.partial(_attn_prep_kernel, tm=tm),
        grid=(m // tm,),
        in_specs=[colblk(C_W, COL_C_Q), colblk(C_KV_W, COL_C_K), colblk(C_KV_W, COL_C_V),
                  colblk(IDX_HEADS * IDX_HD, COL_C_QI), colblk(kw_width, COL_C_KW), tab, tab, tab, tab, gain, gain],
        out_specs=[o[1] for o in outs],
        out_shape=[o[0] for o in outs],
        compiler_params=_cparams(("parallel",)),
        name="attn_prep",
    )(p, p, p, p, p, *tabs, q_gain.reshape(1, C_HD), k_gain.reshape(1, C_HD))


V_ONES = 16


def _attn_kernel(q_ref, k_ref, vt_ref, qi_ref, ki_ref, w_ref, o_ref, key_sc, bias_sc, m_sc, acc_sc, *, tq, tk, topk):
    qb = pl.program_id(1)
    nkb = ((qb + 1) * tq + tk - 1) // tk
    lane = lax.broadcasted_iota(I32, (tq, LANES), 1)
    w_t = w_ref[...].T
    q_chunk = (qb * tq + lax.broadcasted_iota(I32, (tk, tq), 1)) // CHUNK
    key_in_blk = lax.broadcasted_iota(I32, (tk, tq), 0)
    n_pairs = IDX_HEADS * IDX_HD // LANES
    fold = 64

    q_pairs = []
    for hp in range(n_pairs):
        qp = qi_ref[:, hp * LANES:(hp + 1) * LANES]
        zero = jnp.zeros_like(qp)
        q_pairs.append(jnp.concatenate([jnp.where(lane < IDX_HD, qp, zero), jnp.where(lane < IDX_HD, zero, qp)], axis=0))

    def score_block(kb, carry):
        c0 = pl.multiple_of(kb * tk, tk)
        ki2 = ki_ref[pl.ds(c0, tk), :]
        sc = jnp.zeros((tk, tq), F32)
        for hp in range(n_pairs):
            rel = jnp.maximum(_dot_nt(ki2, q_pairs[hp]), 0.0)
            sc = sc + w_t[2 * hp:2 * hp + 1, :] * rel[:, 0:tq] + w_t[2 * hp + 1:2 * hp + 2, :] * rel[:, tq:2 * tq]
        sc = jnp.where(sc == 0.0, 0.0, sc)
        bits = lax.bitcast_convert_type(sc, I32)
        skey = bits ^ ((bits >> 31) & 0x7FFFFFFF)
        allowed = ((c0 + key_in_blk) // CHUNK) <= q_chunk
        key_sc[kb] = jnp.where(allowed, skey, INT_MIN)
        return carry

    lax.fori_loop(0, nkb, score_block, 0)

    def count_ge(cand):
        def body(kb, acc):
            hit = jnp.where(key_sc[kb] >= cand, 1.0, 0.0)
            for j in range(tk // fold):
                acc = acc + hit[j * fold:(j + 1) * fold]
            return acc
        acc = lax.fori_loop(0, nkb, body, jnp.zeros((fold, tq), F32))
        return jnp.sum(acc, axis=0, keepdims=True)

    kf = float(topk)
    thr = jnp.where(count_ge(jnp.zeros((1, tq), I32)) >= kf, 0, INT_MIN).astype(I32)

    def bit_step(i, thr):
        cand = thr | (jnp.int32(1) << (30 - i))
        return jnp.where(count_ge(cand) >= kf, cand, thr)

    thr = lax.fori_loop(0, 31, bit_step, thr)
    thr = jnp.maximum(thr, INT_MIN + 1)
    need = kf - count_ge(thr + 1)

    tri = (lax.broadcasted_iota(I32, (tk, tk), 0) >= lax.broadcasted_iota(I32, (tk, tk), 1)).astype(BF16)

    def bias_block(kb, seen):
        keys = key_sc[kb]
        eq = keys == thr
        eq_f = jnp.where(eq, 1.0, 0.0)
        rank = _dot(tri, eq_f.astype(BF16)) + seen
        take = (keys > thr) | (eq & (rank <= need))
        bias_sc[kb] = jnp.where(take, 0.0, NEG_BIG)
        return rank[tk - 1:tk]

    lax.fori_loop(0, nkb, bias_block, jnp.zeros((1, tq), F32))

    rep = C_HEADS // C_KV_HEADS
    ve = C_HD + V_ONES
    qs = [jnp.concatenate([q_ref[:, (g * rep + r) * C_HD:(g * rep + r + 1) * C_HD] for r in range(rep)], axis=0)
          for g in range(C_KV_HEADS)]
    m_sc[...] = jnp.full(m_sc.shape, NEG_BIG, F32)
    acc_sc[...] = jnp.zeros(acc_sc.shape, F32)

    def body(kb, carry):
        c0 = pl.multiple_of(kb * tk, tk)
        bias = bias_sc[kb]
        s_all = [_dot_nt(k_ref[pl.ds(c0, tk), g * C_HD:(g + 1) * C_HD], qs[g]) for g in range(C_KV_HEADS)]
        for g in range(C_KV_HEADS):
            s = jnp.concatenate([s_all[g][:, r * tq:(r + 1) * tq] + bias for r in range(rep)], axis=1)
            m_prev = m_sc[g]
            m_new = jnp.maximum(m_prev, jnp.max(s, axis=0, keepdims=True))
            p = jnp.exp2(s - m_new)
            acc_sc[g] = jnp.exp2(m_prev - m_new) * acc_sc[g] + _dot(vt_ref[kb, g * ve:(g + 1) * ve, :], p.astype(BF16))
            m_sc[g] = m_new
        return carry

    lax.fori_loop(0, nkb, body, 0)
    for g in range(C_KV_HEADS):
        acc = acc_sc[g]
        out_t = acc[0:C_HD] / acc[C_HD:C_HD + 1]
        for r in range(rep):
            h = g * rep + r
            o_ref[:, h * C_HD:(h + 1) * C_HD] = out_t[:, r * tq:(r + 1) * tq].T.astype(o_ref.dtype)


def sparse_attention(qn, kn, vn, qi, ki2, w, bsz, seq, tq=128, tk=512):
    tk = min(tk, seq)
    topk = min(TOPK_MAX, seq // 4)
    nkb = seq // tk
    rep = C_HEADS // C_KV_HEADS
    ve = C_HD + V_ONES
    r3 = lambda a: a.reshape(bsz, seq, a.shape[-1])
    v_t = vn.reshape(bsz, nkb, tk, C_KV_HEADS, C_HD).transpose(0, 1, 3, 4, 2)
    v_t = jnp.concatenate([v_t, jnp.ones((bsz, nkb, C_KV_HEADS, V_ONES, tk), v_t.dtype)], axis=3)
    v_t = v_t.reshape(bsz, nkb, C_KV_HEADS * ve, tk)
    qblk = lambda width: pl.BlockSpec((None, tq, width), lambda b, i: (b, i, 0))
    sblk = lambda width: pl.BlockSpec((None, seq, width), lambda b, i: (b, 0, 0))
    return pl.pallas_call(
        functools.partial(_attn_kernel, tq=tq, tk=tk, topk=topk),
        grid=(bsz, seq // tq),
        in_specs=[qblk(C_W), sblk(C_KV_W), pl.BlockSpec((None, nkb, C_KV_HEADS * ve, tk), lambda b, i: (b, 0, 0, 0)),
                  qblk(IDX_HEADS * IDX_HD), sblk(LANES), qblk(LANES)],
        out_specs=qblk(C_W),
        out_shape=jax.ShapeDtypeStruct((bsz, seq, C_W), BF16),
        scratch_shapes=[pltpu.VMEM((nkb, tk, tq), I32), pltpu.VMEM((nkb, tk, tq), F32),
                        pltpu.VMEM((C_KV_HEADS, 1, rep * tq), F32), pltpu.VMEM((C_KV_HEADS, ve, rep * tq), F32)],
        compiler_params=_cparams(("parallel", "arbitrary")),
        name="sparse_attention",
    )(r3(qn), r3(kn), v_t, r3(qi), r3(ki2), r3(w))


def _rope_tables(positions):
    pos = positions.astype(F32).reshape(-1, 1)

    def tables(head_dim):
        rot = head_dim // ROPE_FRACTION
        half = rot // 2
        inv_freq = ROPE_THETA ** (-jnp.arange(0, rot, 2, dtype=F32) / rot)
        ang = pos * inv_freq
        cos, sin = jnp.cos(ang), jnp.sin(ang)
        ones = jnp.ones((pos.shape[0], head_dim - rot), F32)
        c = jnp.concatenate([cos, cos, ones], axis=1)
        s = jnp.concatenate([-sin, sin, 0.0 * ones], axis=1)
        reps = LANES // head_dim
        return jnp.tile(c, (1, reps)), jnp.tile(s, (1, reps))

    ca, sa = tables(C_HD)
    ci, si = tables(IDX_HD)
    return ca, sa, ci, si


def _pad_rows(a, before, total):
    return jnp.zeros((total, a.shape[1]), a.dtype).at[before:before + a.shape[0]].set(a)


def kernel(x, positions, norm_mix, w_in, hgrn_lb_logits, hgrn_out_norm, rwkv_mu, rwkv_w0, rwkv_w2, rwkv_a0, rwkv_a2,
           rwkv_g2, rwkv_k_k, rwkv_k_a, rwkv_r_k, rwkv_ln_w, rwkv_ln_b, q_norm, k_norm, w_branch_a, w_branch_b,
           w_branch_c, w_out, norm_ffn, w_up, conv_w, conv_b, w_down):
    bsz, seq, d_model = x.shape
    depth = w_in.shape[0]
    m = bsz * seq
    tabs = _rope_tables(positions)
    lb_all = jnp.cumsum(jax.nn.softmax(hgrn_lb_logits.astype(F32), axis=0), axis=0)
    lb_all = lb_all - lb_all[0:1]
    n_lora = B_LORA_DECAY + B_LORA_A + B_LORA_G
    ab_cols = 4 * A_W + 3 * B_W + n_lora
    c_cols = C_W + 2 * C_KV_W + IDX_HEADS * IDX_HD + IDX_HD + IDX_HEADS
    w_in_t = jnp.swapaxes(w_in, 1, 2)
    w_in_ab = pack_cols(w_in_t, 0, ab_cols, W_AB)
    w_in_c = pack_cols(w_in_t, ab_cols, c_cols, W_C)
    w_in_g = pack_cols(w_in_t, ab_cols + c_cols, 3 * d_model, 3 * d_model)
    w_a, w_b, w_c = w_branch_a.astype(BF16), w_branch_b.astype(BF16), w_branch_c.astype(BF16)
    w_o, w_d = w_out.astype(BF16), w_down.astype(BF16)

    xf = x.reshape(m, d_model)
    for l in range(depth):
        h = rmsnorm(xf, norm_mix[l])
        p = proj_in(h, w_in_ab, l, F32)
        p_c = proj_in(h, w_in_c, l, F32)
        p_g = proj_in(h, w_in_g, l, BF16)

        o_a = hgrn2(p.reshape(bsz, seq, -1), lb_all[l], hgrn_out_norm[l]).reshape(m, A_W)

        mu = rwkv_mu[l]
        mu4 = jnp.stack([mu[0:B_W], mu[B_W:2 * B_W], mu[2 * B_W:3 * B_W],
                         jnp.pad(mu[3 * B_W:], (0, B_W - n_lora))])
        w2p = _pad_rows(rwkv_w2[l], 0, LANES).astype(BF16)
        a2p = _pad_rows(rwkv_a2[l], B_LORA_DECAY, LANES).astype(BF16)
        g2p = _pad_rows(rwkv_g2[l], 0, 2 * LANES).astype(BF16)
        parts = rwkv_prep(p, seq, mu4, rwkv_w0[l], w2p, rwkv_a0[l], a2p, g2p, rwkv_k_k[l], rwkv_k_a[l])
        o_b = rwkv_recurrence(*parts, rwkv_r_k[l], rwkv_ln_w[l], rwkv_ln_b[l], bsz, seq).reshape(m, B_W)

        qn, kn, vn, qi, ki2, wi = attn_prep(p_c, tabs, q_norm[l], k_norm[l])
        o_c = sparse_attention(qn, kn, vn, qi, ki2, wi, bsz, seq).reshape(m, C_W)

        merged = merge_branches(o_a, o_b, o_c, w_a, w_b, w_c, l, p_g, d_model)
        xf = matmul_residual(merged, w_o, l, xf, tm=1024, tn=512)

        h2 = rmsnorm(xf, norm_ffn[l])
        act = ffn_up(h2, w_up, l, conv_w[l], conv_b[l].reshape(1, -1), seq)
        xf = matmul_residual(act, w_d, l, xf, tm=512, tn=512)
    return xf.reshape(bsz, seq, d_model)
```

```python
import functools

import jax
import jax.numpy as jnp
import numpy as np
from jax import lax
from jax.experimental import pallas as pl
from jax.experimental.pallas import tpu as pltpu

F32 = jnp.float32
BF16 = jnp.bfloat16
I32 = jnp.int32

CHUNK = 64
CHUNK_SHIFT = 6
ROPE_THETA = 500000.0
ROPE_FRACTION = 4
NORM_EPS = 1e-6
A_HEADS, A_DK, A_DV = 8, 128, 128
A_W = A_HEADS * A_DV
B_HEADS, B_HD = 16, 64
B_W = B_HEADS * B_HD
B_LORA_DECAY, B_LORA_A, B_LORA_G = 64, 64, 160
B_GN_EPS = 64e-5
C_HEADS, C_KV_HEADS, C_HD = 16, 4, 128
C_W = C_HEADS * C_HD
C_KV_W = C_KV_HEADS * C_HD
IDX_HEADS, IDX_HD = 16, 64
TOPK_MAX = 256
CONV_W = 3

LANES = 128
SUBLANES = 8
VMEM_LIMIT = 56 * 1024 * 1024

COL_A = 0
COL_B_RKV = 4096
COL_B_LORA = 7168
W_AB = 8192
COL_C_Q = 0
COL_C_K = 2048
COL_C_V = 2560
COL_C_QI = 3072
COL_C_KW = 4096
W_C = 4608
LOG2_E = 1.4426950408889634
NEG_BIG = -1e30
INT_MIN = -2147483648


def _sigmoid(x):
    return 1.0 / (1.0 + jnp.exp(-x))


def _dot(a, b):
    return jnp.dot(a, b, preferred_element_type=F32)


def _dot_nt(a, b):
    return lax.dot_general(a, b, (((1,), (1,)), ((), ())), preferred_element_type=F32)


def _dot_tn(a, b):
    return lax.dot_general(a, b, (((0,), (0,)), ((), ())), preferred_element_type=F32)


def _split3(x):
    hi = x.astype(BF16)
    r1 = x - hi.astype(F32)
    mid = r1.astype(BF16)
    lo = (r1 - mid.astype(F32)).astype(BF16)
    return hi, mid, lo


def _dot_exact_lhs(m_bf16, x):
    hi, mid, lo = _split3(x)
    return _dot(m_bf16, hi) + _dot(m_bf16, mid) + _dot(m_bf16, lo)


def _dot_exact_rhs(x, m_bf16):
    hi, mid, lo = _split3(x)
    return _dot(hi, m_bf16) + _dot(mid, m_bf16) + _dot(lo, m_bf16)


def _split2(x):
    hi = x.astype(BF16)
    lo = (x - hi.astype(F32)).astype(BF16)
    return hi, lo


def _dot_hp(a, b):
    ah, al = _split2(a)
    bh, bl = _split2(b)
    return _dot(ah, bh) + _dot(ah, bl) + _dot(al, bh)


def _dot_nt_hp(a, b):
    ah, al = _split2(a)
    bh, bl = _split2(b)
    return _dot_nt(ah, bh) + _dot_nt(ah, bl) + _dot_nt(al, bh)


def _dot_tn_hp(a, b):
    ah, al = _split2(a)
    bh, bl = _split2(b)
    return _dot_tn(ah, bh) + _dot_tn(ah, bl) + _dot_tn(al, bh)


def _cparams(sem):
    return pltpu.CompilerParams(dimension_semantics=sem, vmem_limit_bytes=VMEM_LIMIT)


def _rmsnorm_kernel(x_ref, g_ref, o_ref):
    x = x_ref[...]
    ms = jnp.mean(x * x, axis=-1, keepdims=True)
    o_ref[...] = (x * lax.rsqrt(ms + NORM_EPS) * g_ref[...]).astype(o_ref.dtype)


def rmsnorm(x, gain, tm=256):
    m, d = x.shape
    return pl.pallas_call(
        _rmsnorm_kernel,
        grid=(m // tm,),
        in_specs=[pl.BlockSpec((tm, d), lambda i: (i, 0)), pl.BlockSpec((1, d), lambda i: (0, 0))],
        out_specs=pl.BlockSpec((tm, d), lambda i: (i, 0)),
        out_shape=jax.ShapeDtypeStruct((m, d), BF16),
        compiler_params=_cparams(("parallel",)),
        name="rmsnorm",
    )(x, gain.reshape(1, d))


def _wspec(w, layer, tn, jmap):
    return pl.BlockSpec((None, w.shape[1], tn), lambda *g: (layer, 0, jmap(*g)))


def _proj_in_kernel(h_ref, w_ref, o_ref):
    o_ref[...] = _dot(h_ref[...], w_ref[...]).astype(o_ref.dtype)


def proj_in(h, w, layer, out_dtype, tm=1024, tn=512):
    m, k = h.shape
    n = w.shape[2]
    tm = min(tm, m)
    return pl.pallas_call(
        _proj_in_kernel,
        grid=(m // tm, n // tn),
        in_specs=[pl.BlockSpec((tm, k), lambda i, j: (i, 0)), _wspec(w, layer, tn, lambda i, j: j)],
        out_specs=pl.BlockSpec((tm, tn), lambda i, j: (i, j)),
        out_shape=jax.ShapeDtypeStruct((m, n), out_dtype),
        compiler_params=_cparams(("parallel", "arbitrary")),
        name="proj_in",
    )(h, w)


def _pack_cols_kernel(*refs, shift, width):
    a_ref, o_ref = refs[0], refs[-1]
    t = pl.program_id(1)
    a = a_ref[...]
    if shift:
        a = jnp.concatenate([a[shift:], refs[1][0:shift]], axis=0)
    col = t * LANES + lax.broadcasted_iota(I32, a.shape, 0)
    o_ref[...] = jnp.where(col < width, a, 0.0).T.astype(o_ref.dtype)


def pack_cols(w_t, src_start, width, dst_width):
    depth, n_src, k = w_t.shape
    q0, shift = divmod(src_start, LANES)
    assert shift % SUBLANES == 0
    last = (n_src - 1) // LANES

    def src(extra):
        return pl.BlockSpec((None, LANES, k), lambda l, t: (l, jnp.minimum(q0 + t + extra, last), 0))

    w = w_t
    srcs = [src(0), src(1)] if shift else [src(0)]
    return pl.pallas_call(
        functools.partial(_pack_cols_kernel, shift=shift, width=width),
        grid=(depth, dst_width // LANES),
        in_specs=srcs,
        out_specs=pl.BlockSpec((None, k, LANES), lambda l, t: (l, 0, t)),
        out_shape=jax.ShapeDtypeStruct((depth, k, dst_width), BF16),
        compiler_params=_cparams(("parallel", "arbitrary")),
        name="pack_cols",
    )(*([w] * len(srcs)))


def _mm_res_kernel(a_ref, b_ref, r_ref, o_ref):
    o_ref[...] = r_ref[...] + _dot(a_ref[...], b_ref[...])


def matmul_residual(a, b, layer, res, tm=512, tn=512):
    m, k = a.shape
    n = b.shape[2]
    tm = min(tm, m)
    tn = min(tn, n)
    return pl.pallas_call(
        _mm_res_kernel,
        grid=(m // tm, n // tn),
        in_specs=[
            pl.BlockSpec((tm, k), lambda i, j: (i, 0)),
            _wspec(b, layer, tn, lambda i, j: j),
            pl.BlockSpec((tm, tn), lambda i, j: (i, j)),
        ],
        out_specs=pl.BlockSpec((tm, tn), lambda i, j: (i, j)),
        out_shape=jax.ShapeDtypeStruct((m, n), F32),
        compiler_params=_cparams(("parallel", "arbitrary")),
        name="matmul_residual",
    )(a, b, res)


def _merge_kernel(oa_ref, ob_ref, oc_ref, wa_ref, wb_ref, wc_ref, ga_ref, gb_ref, gc_ref, o_ref):
    acc = _sigmoid(ga_ref[...].astype(F32)) * _dot(oa_ref[...], wa_ref[...])
    acc += _sigmoid(gb_ref[...].astype(F32)) * _dot(ob_ref[...], wb_ref[...])
    acc += _sigmoid(gc_ref[...].astype(F32)) * _dot(oc_ref[...], wc_ref[...])
    o_ref[...] = acc.astype(o_ref.dtype)


def merge_branches(o_a, o_b, o_c, w_a, w_b, w_c, layer, p, d_model, tm=1024, tn=512):
    m = o_a.shape[0]
    tm = min(tm, m)
    g0 = 0
    gstep = d_model // tn
    return pl.pallas_call(
        _merge_kernel,
        grid=(m // tm, d_model // tn),
        in_specs=[
            pl.BlockSpec((tm, o_a.shape[1]), lambda i, j: (i, 0)),
            pl.BlockSpec((tm, o_b.shape[1]), lambda i, j: (i, 0)),
            pl.BlockSpec((tm, o_c.shape[1]), lambda i, j: (i, 0)),
            _wspec(w_a, layer, tn, lambda i, j: j),
            _wspec(w_b, layer, tn, lambda i, j: j),
            _wspec(w_c, layer, tn, lambda i, j: j),
            pl.BlockSpec((tm, tn), lambda i, j: (i, g0 + j)),
            pl.BlockSpec((tm, tn), lambda i, j: (i, g0 + gstep + j)),
            pl.BlockSpec((tm, tn), lambda i, j: (i, g0 + 2 * gstep + j)),
        ],
        out_specs=pl.BlockSpec((tm, tn), lambda i, j: (i, j)),
        out_shape=jax.ShapeDtypeStruct((m, d_model), BF16),
        compiler_params=_cparams(("parallel", "arbitrary")),
        name="merge_branches",
    )(o_a, o_b, o_c, w_a, w_b, w_c, p, p, p)


FFN_K_SPLIT = 4


def _ffn_up_kernel(h_ref, wg_ref, wu_ref, cwg_ref, cwu_ref, cbg_ref, cbu_ref, o_ref, w_sc, u0_sc, u1_sc, cg_ref,
                   cu_ref, *, tiles_per_seq, n_tiles, tm, ts):
    i = pl.program_id(1)
    tn = o_ref.shape[1]
    kdim = h_ref.shape[1]
    n_sub = tm // ts
    tk = kdim // FFN_K_SPLIT
    te = ts // FFN_K_SPLIT
    rows = lax.broadcasted_iota(I32, (te, tn), 0)
    slots = (u0_sc, u1_sc)

    @pl.when(i == 0)
    def _():
        w_sc[:, 0:tn] = wg_ref[...].astype(BF16)
        w_sc[:, tn:2 * tn] = wu_ref[...].astype(BF16)

    @pl.when((i > 0) & (((i - 1) % tiles_per_seq) == 0))
    def _():
        cg_ref[...] = jnp.zeros_like(cg_ref)
        cu_ref[...] = jnp.zeros_like(cu_ref)

    def conv(u, prev, cw_ref, cb_ref):
        u1 = jnp.where(rows == 0, prev[7:8, :], pltpu.roll(u, 1, axis=0))
        u2 = pltpu.roll(u, 2, axis=0)
        u2 = jnp.where(rows == 0, prev[6:7, :], jnp.where(rows == 1, prev[7:8, :], u2))
        cw = cw_ref[...]
        return cw[0:1, :] * u2 + cw[1:2, :] * u1 + cw[2:3, :] * u + cb_ref[...]

    def epilogue_piece(p, src, prev_g, prev_u):
        u = src[p * te:(p + 1) * te, :]
        ug = u[:, 0:tn]
        uu = u[:, tn:2 * tn]
        gate = conv(ug, prev_g, cwg_ref, cbg_ref)
        up = conv(uu, prev_u, cwu_ref, cbu_ref)
        o_ref[p * te:(p + 1) * te, :] = (gate * _sigmoid(gate) * up).astype(o_ref.dtype)
        return ug[te - SUBLANES:te, :], uu[te - SUBLANES:te, :]

    def run(mm_dst, ep_src):
        if ep_src is not None:
            prev_g, prev_u = cg_ref[...], cu_ref[...]
        for sb in range(n_sub):
            acc = None
            for kc in range(FFN_K_SPLIT):
                if mm_dst is not None:
                    part = _dot(h_ref[sb * ts:(sb + 1) * ts, kc * tk:(kc + 1) * tk], w_sc[kc * tk:(kc + 1) * tk, :])
                    acc = part if acc is None else acc + part
                if ep_src is not None:
                    prev_g, prev_u = epilogue_piece(sb * FFN_K_SPLIT + kc, ep_src, prev_g, prev_u)
            if mm_dst is not None:
                mm_dst[sb * ts:(sb + 1) * ts, :] = acc
        if ep_src is not None:
            cg_ref[...] = prev_g
            cu_ref[...] = prev_u

    @pl.when(i == 0)
    def _():
        run(slots[0], None)

    for parity in range(2):
        @pl.when((i > 0) & (i < n_tiles) & (i % 2 == parity))
        def _():
            run(slots[parity], slots[1 - parity])

    @pl.when(i == n_tiles)
    def _():
        run(None, slots[(n_tiles - 1) % 2])


def ffn_up(h, w_up, layer, conv_w, conv_b, seq, tm=1024, tn=256, ts=256):
    m, k = h.shape
    d_ff = w_up.shape[2] // 2
    tm = min(tm, seq)
    ts = min(ts, tm)
    nj = d_ff // tn
    n_tiles = m // tm
    return pl.pallas_call(
        functools.partial(_ffn_up_kernel, tiles_per_seq=seq // tm, n_tiles=n_tiles, tm=tm, ts=ts),
        grid=(nj, n_tiles + 1),
        in_specs=[
            pl.BlockSpec((tm, k), lambda j, i: (jnp.minimum(i, n_tiles - 1), 0)),
            _wspec(w_up, layer, tn, lambda j, i: j),
            _wspec(w_up, layer, tn, lambda j, i: nj + j),
            pl.BlockSpec((CONV_W, tn), lambda j, i: (0, j)),
            pl.BlockSpec((CONV_W, tn), lambda j, i: (0, nj + j)),
            pl.BlockSpec((1, tn), lambda j, i: (0, j)),
            pl.BlockSpec((1, tn), lambda j, i: (0, nj + j)),
        ],
        out_specs=pl.BlockSpec((tm, tn), lambda j, i: (jnp.maximum(i - 1, 0), j)),
        out_shape=jax.ShapeDtypeStruct((m, d_ff), BF16),
        scratch_shapes=[pltpu.VMEM((k, 2 * tn), BF16), pltpu.VMEM((tm, 2 * tn), F32), pltpu.VMEM((tm, 2 * tn), F32),
                        pltpu.VMEM((SUBLANES, tn), F32), pltpu.VMEM((SUBLANES, tn), F32)],
        compiler_params=_cparams(("parallel", "arbitrary")),
        name="ffn_up",
    )(h, w_up, w_up, conv_w, conv_w, conv_b, conv_b)


A_SUB = 16


def _hgrn_kernel(q_ref, f_ref, i_ref, g_ref, lb_ref, gain_ref, o_ref, st_ref, *, n_chunks, heads):
    @pl.when(pl.program_id(2) == 0)
    def _():
        st_ref[...] = jnp.zeros_like(st_ref)

    gain = gain_ref[...]
    tri = (lax.broadcasted_iota(I32, (CHUNK, CHUNK), 0) >= lax.broadcasted_iota(I32, (CHUNK, CHUNK), 1)).astype(BF16)
    rows_sub = lax.broadcasted_iota(I32, (A_SUB, A_DK), 0)
    n_sub = CHUNK // A_SUB

    def chunk(c, carry):
        rows = pl.ds(pl.multiple_of(c * CHUNK, CHUNK), CHUNK)
        hs = range(heads)
        cols = [slice(hh * A_DK, (hh + 1) * A_DK) for hh in hs]
        iv = [i_ref[rows, cl] for cl in cols]
        iv_b = [x.astype(BF16) for x in iv]
        st = [st_ref[hh] for hh in hs]
        qf, kin, b = [], [], []
        for hh in hs:
            lb = lb_ref[:, cols[hh]]
            z = f_ref[rows, cols[hh]]
            qv = q_ref[rows, cols[hh]]
            qf.append(qv * _sigmoid(qv))
            kin.append((1.0 - lb) * _sigmoid(-z))
            b.append(_dot_exact_lhs(tri, jnp.log(lb + (1.0 - lb) * _sigmoid(z))))
        o_inter = [_dot_nt((qf[hh] * jnp.exp(b[hh])).astype(BF16), st[hh].astype(BF16)) for hh in hs]
        outs = [[] for _ in hs]
        for si in range(n_sub):
            lo = si * A_SUB
            sub = slice(lo, lo + A_SUB)
            o_i = [o_inter[hh][sub] for hh in hs]
            if si > 0:
                att = []
                for hh in hs:
                    b_ref_row = b[hh][lo - 1:lo]
                    q_s = (qf[hh][sub] * jnp.exp(b[hh][sub] - b_ref_row)).astype(BF16)
                    k_s = (kin[hh][0:lo] * jnp.exp(b_ref_row - b[hh][0:lo])).astype(BF16)
                    att.append(_dot_nt(q_s, k_s))
                o_i = [o_i[hh] + _dot(att[hh].astype(BF16), iv_b[hh][0:lo]) for hh in hs]
            for s in range(A_SUB):
                for hh in hs:
                    b_i = b[hh][sub]
                    d = jnp.exp(jnp.where(rows_sub >= s, b_i - b_i[s:s + 1], -jnp.inf))
                    a = jnp.sum(qf[hh][sub] * d * kin[hh][lo + s:lo + s + 1], axis=1, keepdims=True)
                    o_i[hh] = o_i[hh] + a * iv[hh][lo + s:lo + s + 1]
            for hh in hs:
                outs[hh].append(o_i[hh])
        for hh in hs:
            b_last = b[hh][CHUNK - 1:CHUNK]
            k_dec = (kin[hh] * jnp.exp(b_last - b[hh])).astype(BF16)
            st_ref[hh] = st[hh] * jnp.exp(b_last) + _dot_tn(iv_b[hh], k_dec)
        for hh in hs:
            o = jnp.concatenate(outs[hh], axis=0)
            gv = g_ref[rows, cols[hh]]
            ms = jnp.mean(o * o, axis=-1, keepdims=True)
            on = o * lax.rsqrt(ms + NORM_EPS) * gain
            o_ref[rows, cols[hh]] = (on * (gv * _sigmoid(gv))).astype(o_ref.dtype)
        return carry

    lax.fori_loop(0, n_chunks, chunk, 0)


def hgrn2(p3, lb, out_gain, s_blk=256, heads=4):
    bsz, seq, _ = p3.shape
    s_blk = min(s_blk, seq)
    width = heads * A_DK
    groups = A_HEADS // heads
    c0 = COL_A // width

    def col(part):
        return pl.BlockSpec((None, s_blk, width), lambda b, h, s, part=part: (b, s, c0 + part * groups + h))

    return pl.pallas_call(
        functools.partial(_hgrn_kernel, n_chunks=s_blk // CHUNK, heads=heads),
        grid=(bsz, groups, seq // s_blk),
        in_specs=[col(0), col(1), col(2), col(3),
                  pl.BlockSpec((1, width), lambda b, h, s: (0, h)),
                  pl.BlockSpec((1, A_DV), lambda b, h, s: (0, 0))],
        out_specs=pl.BlockSpec((None, s_blk, width), lambda b, h, s: (b, s, h)),
        out_shape=jax.ShapeDtypeStruct((bsz, seq, A_W), BF16),
        scratch_shapes=[pltpu.VMEM((heads, A_DV, A_DK), F32)],
        compiler_params=_cparams(("parallel", "parallel", "arbitrary")),
        name="hgrn2",
    )(p3, p3, p3, p3, lb.reshape(1, A_HEADS * A_DK), out_gain.reshape(1, A_DV))


def _rwkv_prep_kernel(r_ref, k_ref, v_ref, l_ref, rp_ref, kp_ref, vp_ref, lp_ref,
                      mu_ref, w0_ref, w2_ref, a0_ref, a2_ref, g2_ref, kk_ref, ka_ref,
                      ro_ref, ld_ref, k2_ref, vo_ref, kko_ref, kka_ref, go_ref, *, tiles_per_seq, tm):
    first = (pl.program_id(0) % tiles_per_seq) == 0
    rows = lax.broadcasted_iota(I32, (tm, B_W), 0)

    def shifted(cur_ref, prev_ref, part):
        cur = cur_ref[...]
        prev = jnp.where(first, 0.0, prev_ref[...])[SUBLANES - 1:SUBLANES, :]
        sh = jnp.where(rows == 0, prev, pltpu.roll(cur, 1, axis=0))
        return cur + (sh - cur) * mu_ref[part:part + 1, :]

    r = shifted(r_ref, rp_ref, 0)
    k = shifted(k_ref, kp_ref, 1)
    v = shifted(v_ref, vp_ref, 2)
    lo = shifted(l_ref, lp_ref, 3)
    lo_a = lo[:, 0:LANES]
    lo_g = lo[:, LANES:3 * LANES]
    wpre = w0_ref[...] + _dot(jnp.tanh(lo_a).astype(BF16), w2_ref[...])
    y = -wpre
    softplus = jnp.maximum(y, 0.0) + jnp.log(1.0 + jnp.exp(-jnp.abs(y)))
    w_log = -softplus - 0.5
    ld_ref[...] = -jnp.exp(w_log)
    a = _sigmoid(a0_ref[...] + _dot(lo_a.astype(BF16), a2_ref[...]))
    go_ref[...] = _dot(_sigmoid(lo_g).astype(BF16), g2_ref[...])
    kk = k * kk_ref[...]
    bd = (lax.broadcasted_iota(I32, (LANES, LANES), 0) // B_HD
          == lax.broadcasted_iota(I32, (LANES, LANES), 1) // B_HD).astype(BF16)
    sq = kk * kk
    ss = jnp.concatenate([_dot_exact_rhs(sq[:, j * LANES:(j + 1) * LANES], bd) for j in range(B_W // LANES)], axis=1)
    kk = kk / jnp.maximum(jnp.sqrt(ss), 1e-12)
    ro_ref[...] = r
    vo_ref[...] = v
    kko_ref[...] = kk
    kka_ref[...] = kk * a
    k2_ref[...] = k * (1.0 + (a - 1.0) * ka_ref[...])


def rwkv_prep(p, seq, mu4, w0, w2p, a0, a2p, g2p, k_k, k_a, tm=256):
    m = p.shape[0]
    tm = min(tm, seq)
    cb = COL_B_RKV // B_W
    pb = tm // SUBLANES

    def cur(j):
        return pl.BlockSpec((tm, B_W), lambda i, j=j: (i, cb + j))

    def prev(j):
        return pl.BlockSpec((SUBLANES, B_W), lambda i, j=j: (jnp.maximum(i * pb - 1, 0), cb + j))

    def full(a):
        return pl.BlockSpec(a.shape, lambda i: (0, 0))

    row = lambda a: a.reshape(1, B_W)
    params = [mu4, row(w0), w2p, row(a0), a2p, g2p, row(k_k), row(k_a)]
    out = jax.ShapeDtypeStruct((m, B_W), F32)
    return pl.pallas_call(
        functools.partial(_rwkv_prep_kernel, tiles_per_seq=seq // tm, tm=tm),
        grid=(m // tm,),
        in_specs=[cur(0), cur(1), cur(2), cur(3), prev(0), prev(1), prev(2), prev(3)] + [full(a) for a in params],
        out_specs=[pl.BlockSpec((tm, B_W), lambda i: (i, 0))] * 7,
        out_shape=[out] * 7,
        compiler_params=_cparams(("parallel",)),
        name="rwkv_prep",
    )(p, p, p, p, p, p, p, p, *params)


B_T = 64


def _rwkv_kernel(r_ref, ld_ref, k_ref, v_ref, kk_ref, kka_ref, g_ref, rk_ref, lnw_ref, lnb_ref, o_ref, st_ref, *,
                 n_chunks, pairs):
    @pl.when(pl.program_id(2) == 0)
    def _():
        st_ref[...] = jnp.zeros_like(st_ref)

    t = B_T
    ii = lax.broadcasted_iota(I32, (t, t), 0)
    jj = lax.broadcasted_iota(I32, (t, t), 1)
    tri = (ii >= jj).astype(BF16)
    i2 = lax.broadcasted_iota(I32, (2 * t, 2 * t), 0)
    j2 = lax.broadcasted_iota(I32, (2 * t, 2 * t), 1)
    same = (i2 // t) == (j2 // t)
    strict_bd = same & ((i2 % t) > (j2 % t))
    incl_bd = same & ((i2 % t) >= (j2 % t))
    head_bd = same.astype(BF16)
    lane = lax.broadcasted_iota(I32, (t, LANES), 1)
    h0 = lane < B_HD

    def stack(x):
        return jnp.concatenate([jnp.where(h0, x, jnp.zeros_like(x)), jnp.where(h0, jnp.zeros_like(x), x)], axis=0)

    def head_sum(x):
        hi, lo = _split2(x)
        return _dot(hi, head_bd) + _dot(lo, head_bd)

    def chunk(c, carry):
        sl = pl.ds(pl.multiple_of(c * t, t), t)
        prs = range(pairs)
        cols = [slice(pp * LANES, (pp + 1) * LANES) for pp in prs]
        r = [r_ref[sl, cl] for cl in cols]
        ld = [ld_ref[sl, cl] for cl in cols]
        k = [k_ref[sl, cl] for cl in cols]
        v = [v_ref[sl, cl] for cl in cols]
        cs = []
        for pp in prs:
            ld_hi, ld_lo = _split2(ld[pp])
            cs.append(_dot(tri, ld_hi) + _dot(tri, ld_lo))
        lhs, rhs, vs = [], [], []
        for pp in prs:
            e_neg = jnp.exp(-cs[pp])
            kka = kka_ref[sl, cols[pp]]
            a_t = (-kk_ref[sl, cols[pp]] * jnp.exp(cs[pp] - ld[pp])).astype(BF16)
            r_t = (r[pp] * jnp.exp(cs[pp])).astype(BF16)
            b_t = (kka * e_neg).astype(BF16)
            k_t = (k[pp] * e_neg).astype(BF16)
            lhs.append(jnp.concatenate([stack(a_t), stack(r_t)], axis=0))
            rhs.append(jnp.concatenate([b_t, b_t, k_t, k_t], axis=0))
            vs.append(stack(v[pp].astype(BF16)))
        sc = [_dot_nt(lhs[pp], rhs[pp]) for pp in prs]
        st = [st_ref[pp] for pp in prs]
        proj = [_dot_nt(lhs[pp], st[pp].astype(BF16)) for pp in prs]
        n = [jnp.where(strict_bd, sc[pp][0:2 * t, 0:2 * t], 0.0).astype(BF16) for pp in prs]
        xs = [proj[pp][0:2 * t] + _dot(jnp.where(strict_bd, sc[pp][0:2 * t, 2 * t:4 * t], 0.0).astype(BF16), vs[pp])
              for pp in prs]
        for it in range(6):
            xs = [xs[pp] + _dot(n[pp], xs[pp].astype(BF16)) for pp in prs]
            if it < 5:
                n = [_dot(n[pp], n[pp]).astype(BF16) for pp in prs]
        os_ = []
        for pp in prs:
            m_r = jnp.concatenate([jnp.where(incl_bd, sc[pp][2 * t:4 * t, 0:2 * t], 0.0),
                                   jnp.where(incl_bd, sc[pp][2 * t:4 * t, 2 * t:4 * t], 0.0)], axis=1)
            uv = jnp.concatenate([xs[pp].astype(BF16), vs[pp]], axis=0)
            os_.append(proj[pp][2 * t:4 * t] + _dot(m_r.astype(BF16), uv))
        upd = []
        for pp in prs:
            u = xs[pp][0:t] + xs[pp][t:2 * t]
            c_last = cs[pp][t - 1:t]
            dec = jnp.exp(c_last - cs[pp])
            upd.append(_dot_tn(jnp.concatenate([u, v[pp]], axis=0).astype(BF16),
                               jnp.concatenate([kka_ref[sl, cols[pp]] * dec, k[pp] * dec], axis=0).astype(BF16)))
        for pp in prs:
            st_ref[pp] = st[pp] * jnp.exp(cs[pp][t - 1:t]) + jnp.where(same, upd[pp], 0.0)
        inv = 1.0 / B_HD
        o = [os_[pp][0:t] + os_[pp][t:2 * t] for pp in prs]
        mean = [head_sum(o[pp]) * inv for pp in prs]
        d = [o[pp] - mean[pp] for pp in prs]
        var = [head_sum(d[pp] * d[pp]) * inv for pp in prs]
        bonus = [head_sum(r[pp] * k[pp] * rk_ref[:, cols[pp]]) for pp in prs]
        for pp in prs:
            on = d[pp] * lax.rsqrt(var[pp] + B_GN_EPS) * lnw_ref[:, cols[pp]] + lnb_ref[:, cols[pp]]
            o_ref[sl, cols[pp]] = ((on + bonus[pp] * v[pp]) * g_ref[sl, cols[pp]]).astype(o_ref.dtype)
        return carry

    lax.fori_loop(0, n_chunks, chunk, 0)


def rwkv_recurrence(r, ld, k2, v, kk, kka, g, r_k, ln_w, ln_b, bsz, seq, s_blk=256, pairs=8):
    s_blk = min(s_blk, seq)
    width = pairs * LANES
    groups = B_W // width
    args = [a.reshape(bsz, seq, B_W) for a in (r, ld, k2, v, kk, kka, g)]
    blk = pl.BlockSpec((None, s_blk, width), lambda b, h, s: (b, s, h))
    par = pl.BlockSpec((1, width), lambda b, h, s: (0, h))
    return pl.pallas_call(
        functools.partial(_rwkv_kernel, n_chunks=s_blk // B_T, pairs=pairs),
        grid=(bsz, groups, seq // s_blk),
        in_specs=[blk] * 7 + [par] * 3,
        out_specs=blk,
        out_shape=jax.ShapeDtypeStruct((bsz, seq, B_W), BF16),
        scratch_shapes=[pltpu.VMEM((pairs, LANES, LANES), F32)],
        compiler_params=_cparams(("parallel", "parallel", "arbitrary")),
        name="rwkv_recurrence",
    )(*args, r_k.reshape(1, B_W), ln_w.reshape(1, B_W), ln_b.reshape(1, B_W))


def _rope(x, cos_t, sin_t, lane_in_head, half):
    partner = jnp.where(lane_in_head < half, pltpu.roll(x, LANES - half, axis=1), pltpu.roll(x, half, axis=1))
    return x * cos_t + partner * sin_t


def _attn_prep_kernel(q_ref, k_ref, v_ref, qi_ref, kw_ref, ca_ref, sa_ref, ci_ref, si_ref, qg_ref, kg_ref,
                      qo_ref, ko_ref, vo_ref, qio_ref, kio_ref, wo_ref, *, tm):
    ca, sa, ci, si = ca_ref[...], sa_ref[...], ci_ref[...], si_ref[...]
    lane = lax.broadcasted_iota(I32, (tm, LANES), 1)
    lane_i = lane % IDX_HD
    half_a = C_HD // ROPE_FRACTION // 2
    half_i = IDX_HD // ROPE_FRACTION // 2
    scale = C_HD ** -0.5 * LOG2_E

    def norm_rope(x, gain):
        ms = jnp.mean(x * x, axis=-1, keepdims=True)
        return _rope(x * lax.rsqrt(ms + NORM_EPS) * gain, ca, sa, lane, half_a)

    for h in range(C_HEADS):
        sl = slice(h * C_HD, (h + 1) * C_HD)
        qo_ref[:, sl] = (norm_rope(q_ref[:, sl], qg_ref[...]) * scale).astype(qo_ref.dtype)
    for h in range(C_KV_HEADS):
        sl = slice(h * C_HD, (h + 1) * C_HD)
        ko_ref[:, sl] = norm_rope(k_ref[:, sl], kg_ref[...]).astype(ko_ref.dtype)
    vo_ref[...] = v_ref[...].astype(vo_ref.dtype)
    for j in range(IDX_HEADS * IDX_HD // LANES):
        sl = slice(j * LANES, (j + 1) * LANES)
        qio_ref[:, sl] = _rope(qi_ref[:, sl], ci, si, lane_i, half_i).astype(qio_ref.dtype)
    kw = kw_ref[:, 0:LANES]
    kr = _rope(kw, ci, si, lane_i, half_i)
    kio_ref[...] = jnp.where(lane < IDX_HD, kr, pltpu.roll(kr, IDX_HD, axis=1)).astype(kio_ref.dtype)
    w = pltpu.roll(kw, LANES - IDX_HD, axis=1) * (IDX_HEADS ** -0.5 * IDX_HD ** -0.5)
    wo_ref[...] = jnp.where(lane < IDX_HEADS, w, 0.0)


def attn_prep(p, tabs, q_gain, k_gain, tm=256):
    m = p.shape[0]
    tm = min(tm, m)

    def colblk(width, off):
        return pl.BlockSpec((tm, width), lambda i: (i, off // width))

    tab = pl.BlockSpec((tm, LANES), lambda i: (i, 0))
    gain = pl.BlockSpec((1, C_HD), lambda i: (0, 0))
    kw_width = 512

    def out(width, dtype):
        return jax.ShapeDtypeStruct((m, width), dtype), pl.BlockSpec((tm, width), lambda i: (i, 0))

    outs = [out(C_W, BF16), out(C_KV_W, BF16), out(C_KV_W, BF16), out(IDX_HEADS * IDX_HD, BF16),
            out(LANES, BF16), out(LANES, F32)]
    return pl.pallas_call(
        functools.partial(_attn_prep_kernel, tm=tm),
        grid=(m // tm,),
        in_specs=[colblk(C_W, COL_C_Q), colblk(C_KV_W, COL_C_K), colblk(C_KV_W, COL_C_V),
                  colblk(IDX_HEADS * IDX_HD, COL_C_QI), colblk(kw_width, COL_C_KW), tab, tab, tab, tab, gain, gain],
        out_specs=[o[1] for o in outs],
        out_shape=[o[0] for o in outs],
        compiler_params=_cparams(("parallel",)),
        name="attn_prep",
    )(p, p, p, p, p, *tabs, q_gain.reshape(1, C_HD), k_gain.reshape(1, C_HD))


V_ONES = 16


def _attn_kernel(q_ref, k_ref, vt_ref, qi_ref, ki_ref, w_ref, o_ref, key_sc, bias_sc, m_sc, acc_sc, *, tq, tk, topk):
    qb = pl.program_id(1)
    nkb = ((qb + 1) * tq + tk - 1) // tk
    lane = lax.broadcasted_iota(I32, (tq, LANES), 1)
    w_t = w_ref[...].T
    q_chunk = lax.shift_right_logical(qb * tq + lax.broadcasted_iota(I32, (1, tq), 1), CHUNK_SHIFT)
    key_chunk_in_blk = lax.shift_right_logical(lax.broadcasted_iota(I32, (tk, tq), 0), CHUNK_SHIFT)
    n_pairs = IDX_HEADS * IDX_HD // LANES
    fold = 64

    q_pairs = []
    for hp in range(n_pairs):
        qp = qi_ref[:, hp * LANES:(hp + 1) * LANES]
        zero = jnp.zeros_like(qp)
        q_pairs.append(jnp.concatenate([jnp.where(lane < IDX_HD, qp, zero), jnp.where(lane < IDX_HD, zero, qp)], axis=0))

    def score_block(kb, carry):
        c0 = pl.multiple_of(kb * tk, tk)
        ki2 = ki_ref[pl.ds(c0, tk), :]
        sc = jnp.zeros((tk, tq), F32)
        for hp in range(n_pairs):
            rel = jnp.maximum(_dot_nt(ki2, q_pairs[hp]), 0.0)
            sc = sc + w_t[2 * hp:2 * hp + 1, :] * rel[:, 0:tq] + w_t[2 * hp + 1:2 * hp + 2, :] * rel[:, tq:2 * tq]
        sc = jnp.where(sc == 0.0, 0.0, sc)
        bits = lax.bitcast_convert_type(sc, I32)
        skey = bits ^ ((bits >> 31) & 0x7FFFFFFF)
        allowed = key_chunk_in_blk <= q_chunk - kb * (tk // CHUNK)
        key_sc[kb] = jnp.where(allowed, skey, INT_MIN)
        return carry

    lax.fori_loop(0, nkb, score_block, 0)

    def count_ge(cand):
        def body(kb, acc):
            hit = jnp.where(key_sc[kb] >= cand, 1.0, 0.0)
            for j in range(tk // fold):
                acc = acc + hit[j * fold:(j + 1) * fold]
            return acc
        acc = lax.fori_loop(0, nkb, body, jnp.zeros((fold, tq), F32))
        return jnp.sum(acc, axis=0, keepdims=True)

    kf = float(topk)
    thr = jnp.where(count_ge(jnp.zeros((1, tq), I32)) >= kf, 0, INT_MIN).astype(I32)

    def bit_step(i, thr):
        cand = thr | (jnp.int32(1) << (30 - i))
        return jnp.where(count_ge(cand) >= kf, cand, thr)

    thr = lax.fori_loop(0, 31, bit_step, thr)
    thr = jnp.maximum(thr, INT_MIN + 1)
    need = kf - count_ge(thr + 1)

    tri = (lax.broadcasted_iota(I32, (tk, tk), 0) >= lax.broadcasted_iota(I32, (tk, tk), 1)).astype(BF16)

    def bias_block(kb, seen):
        keys = key_sc[kb]
        eq = keys == thr
        eq_f = jnp.where(eq, 1.0, 0.0)
        rank = _dot(tri, eq_f.astype(BF16)) + seen
        take = (keys > thr) | (eq & (rank <= need))
        bias_sc[kb] = jnp.where(take, 0.0, NEG_BIG)
        return rank[tk - 1:tk]

    lax.fori_loop(0, nkb, bias_block, jnp.zeros((1, tq), F32))

    rep = C_HEADS // C_KV_HEADS
    ve = C_HD + V_ONES
    qs = [jnp.concatenate([q_ref[:, (g * rep + r) * C_HD:(g * rep + r + 1) * C_HD] for r in range(rep)], axis=0)
          for g in range(C_KV_HEADS)]
    m_sc[...] = jnp.full(m_sc.shape, NEG_BIG, F32)
    acc_sc[...] = jnp.zeros(acc_sc.shape, F32)

    def body(kb, carry):
        c0 = pl.multiple_of(kb * tk, tk)
        bias = bias_sc[kb]
        s_all = [_dot_nt(k_ref[pl.ds(c0, tk), g * C_HD:(g + 1) * C_HD], qs[g]) for g in range(C_KV_HEADS)]
        for g in range(C_KV_HEADS):
            s = jnp.concatenate([s_all[g][:, r * tq:(r + 1) * tq] + bias for r in range(rep)], axis=1)
            m_prev = m_sc[g]
            m_new = jnp.maximum(m_prev, jnp.max(s, axis=0, keepdims=True))
            p = jnp.exp2(s - m_new)
            acc_sc[g] = jnp.exp2(m_prev - m_new) * acc_sc[g] + _dot(vt_ref[kb, g * ve:(g + 1) * ve, :], p.astype(BF16))
            m_sc[g] = m_new
        return carry

    lax.fori_loop(0, nkb, body, 0)
    for g in range(C_KV_HEADS):
        acc = acc_sc[g]
        out_t = acc[0:C_HD] / acc[C_HD:C_HD + 1]
        for r in range(rep):
            h = g * rep + r
            o_ref[:, h * C_HD:(h + 1) * C_HD] = out_t[:, r * tq:(r + 1) * tq].T.astype(o_ref.dtype)


def sparse_attention(qn, kn, vn, qi, ki2, w, bsz, seq, tq=128, tk=512):
    tk = min(tk, seq)
    topk = min(TOPK_MAX, seq // 4)
    nkb = seq // tk
    rep = C_HEADS // C_KV_HEADS
    ve = C_HD + V_ONES
    r3 = lambda a: a.reshape(bsz, seq, a.shape[-1])
    v_t = vn.reshape(bsz, nkb, tk, C_KV_HEADS, C_HD).transpose(0, 1, 3, 4, 2)
    v_t = jnp.concatenate([v_t, jnp.ones((bsz, nkb, C_KV_HEADS, V_ONES, tk), v_t.dtype)], axis=3)
    v_t = v_t.reshape(bsz, nkb, C_KV_HEADS * ve, tk)
    qblk = lambda width: pl.BlockSpec((None, tq, width), lambda b, i: (b, i, 0))
    sblk = lambda width: pl.BlockSpec((None, seq, width), lambda b, i: (b, 0, 0))
    return pl.pallas_call(
        functools.partial(_attn_kernel, tq=tq, tk=tk, topk=topk),
        grid=(bsz, seq // tq),
        in_specs=[qblk(C_W), sblk(C_KV_W), pl.BlockSpec((None, nkb, C_KV_HEADS * ve, tk), lambda b, i: (b, 0, 0, 0)),
                  qblk(IDX_HEADS * IDX_HD), sblk(LANES), qblk(LANES)],
        out_specs=qblk(C_W),
        out_shape=jax.ShapeDtypeStruct((bsz, seq, C_W), BF16),
        scratch_shapes=[pltpu.VMEM((nkb, tk, tq), I32), pltpu.VMEM((nkb, tk, tq), F32),
                        pltpu.VMEM((C_KV_HEADS, 1, rep * tq), F32), pltpu.VMEM((C_KV_HEADS, ve, rep * tq), F32)],
        compiler_params=_cparams(("parallel", "arbitrary")),
        name="sparse_attention",
    )(r3(qn), r3(kn), v_t, r3(qi), r3(ki2), r3(w))


def _rope_tables(positions):
    pos = positions.astype(F32).reshape(-1, 1)

    def tables(head_dim):
        rot = head_dim // ROPE_FRACTION
        half = rot // 2
        inv_freq = ROPE_THETA ** (-jnp.arange(0, rot, 2, dtype=F32) / rot)
        ang = pos * inv_freq
        cos, sin = jnp.cos(ang), jnp.sin(ang)
        ones = jnp.ones((pos.shape[0], head_dim - rot), F32)
        c = jnp.concatenate([cos, cos, ones], axis=1)
        s = jnp.concatenate([-sin, sin, 0.0 * ones], axis=1)
        reps = LANES // head_dim
        return jnp.tile(c, (1, reps)), jnp.tile(s, (1, reps))

    ca, sa = tables(C_HD)
    ci, si = tables(IDX_HD)
    return ca, sa, ci, si


def _pad_rows(a, before, total):
    return jnp.zeros((total, a.shape[1]), a.dtype).at[before:before + a.shape[0]].set(a)


def kernel(x, positions, norm_mix, w_in, hgrn_lb_logits, hgrn_out_norm, rwkv_mu, rwkv_w0, rwkv_w2, rwkv_a0, rwkv_a2,
           rwkv_g2, rwkv_k_k, rwkv_k_a, rwkv_r_k, rwkv_ln_w, rwkv_ln_b, q_norm, k_norm, w_branch_a, w_branch_b,
           w_branch_c, w_out, norm_ffn, w_up, conv_w, conv_b, w_down):
    bsz, seq, d_model = x.shape
    depth = w_in.shape[0]
    m = bsz * seq
    tabs = _rope_tables(positions)
    lb_all = jnp.cumsum(jax.nn.softmax(hgrn_lb_logits.astype(F32), axis=0), axis=0)
    lb_all = lb_all - lb_all[0:1]
    n_lora = B_LORA_DECAY + B_LORA_A + B_LORA_G
    ab_cols = 4 * A_W + 3 * B_W + n_lora
    c_cols = C_W + 2 * C_KV_W + IDX_HEADS * IDX_HD + IDX_HD + IDX_HEADS
    w_in_t = jnp.swapaxes(w_in, 1, 2)
    w_in_ab = pack_cols(w_in_t, 0, ab_cols, W_AB)
    w_in_c = pack_cols(w_in_t, ab_cols, c_cols, W_C)
    w_in_g = pack_cols(w_in_t, ab_cols + c_cols, 3 * d_model, 3 * d_model)
    w_a, w_b, w_c = w_branch_a.astype(BF16), w_branch_b.astype(BF16), w_branch_c.astype(BF16)
    w_o, w_d = w_out.astype(BF16), w_down.astype(BF16)

    xf = x.reshape(m, d_model)
    for l in range(depth):
        h = rmsnorm(xf, norm_mix[l])
        p = proj_in(h, w_in_ab, l, F32)
        p_c = proj_in(h, w_in_c, l, F32)
        p_g = proj_in(h, w_in_g, l, BF16)

        o_a = hgrn2(p.reshape(bsz, seq, -1), lb_all[l], hgrn_out_norm[l]).reshape(m, A_W)

        mu = rwkv_mu[l]
        mu4 = jnp.stack([mu[0:B_W], mu[B_W:2 * B_W], mu[2 * B_W:3 * B_W],
                         jnp.pad(mu[3 * B_W:], (0, B_W - n_lora))])
        w2p = _pad_rows(rwkv_w2[l], 0, LANES).astype(BF16)
        a2p = _pad_rows(rwkv_a2[l], B_LORA_DECAY, LANES).astype(BF16)
        g2p = _pad_rows(rwkv_g2[l], 0, 2 * LANES).astype(BF16)
        parts = rwkv_prep(p, seq, mu4, rwkv_w0[l], w2p, rwkv_a0[l], a2p, g2p, rwkv_k_k[l], rwkv_k_a[l])
        o_b = rwkv_recurrence(*parts, rwkv_r_k[l], rwkv_ln_w[l], rwkv_ln_b[l], bsz, seq).reshape(m, B_W)

        qn, kn, vn, qi, ki2, wi = attn_prep(p_c, tabs, q_norm[l], k_norm[l])
        o_c = sparse_attention(qn, kn, vn, qi, ki2, wi, bsz, seq).reshape(m, C_W)

        merged = merge_branches(o_a, o_b, o_c, w_a, w_b, w_c, l, p_g, d_model)
        xf = matmul_residual(merged, w_o, l, xf, tm=1024, tn=512)

        h2 = rmsnorm(xf, norm_ffn[l])
        act = ffn_up(h2, w_up, l, conv_w[l], conv_b[l].reshape(1, -1), seq)
        xf = matmul_residual(act, w_d, l, xf, tm=512, tn=512)
    return xf.reshape(bsz, seq, d_model)
```

```python
import functools

import jax
import jax.numpy as jnp
import numpy as np
from jax import lax
from jax.experimental import pallas as pl
from jax.experimental.pallas import tpu as pltpu

F32 = jnp.float32
BF16 = jnp.bfloat16
I32 = jnp.int32

CHUNK = 64
CHUNK_SHIFT = 6
ROPE_THETA = 500000.0
ROPE_FRACTION = 4
NORM_EPS = 1e-6
A_HEADS, A_DK, A_DV = 8, 128, 128
A_W = A_HEADS * A_DV
B_HEADS, B_HD = 16, 64
B_W = B_HEADS * B_HD
B_LORA_DECAY, B_LORA_A, B_LORA_G = 64, 64, 160
B_GN_EPS = 64e-5
C_HEADS, C_KV_HEADS, C_HD = 16, 4, 128
C_W = C_HEADS * C_HD
C_KV_W = C_KV_HEADS * C_HD
IDX_HEADS, IDX_HD = 16, 64
TOPK_MAX = 256
CONV_W = 3

LANES = 128
SUBLANES = 8
VMEM_LIMIT = 56 * 1024 * 1024

COL_A = 0
COL_B_RKV = 4096
COL_B_LORA = 7168
W_AB = 8192
COL_C_Q = 0
COL_C_K = 2048
COL_C_V = 2560
COL_C_QI = 3072
COL_C_KW = 4096
W_C = 4608
LOG2_E = 1.4426950408889634
NEG_BIG = -1e30
INT_MIN = -2147483648


def _sigmoid(x):
    return 1.0 / (1.0 + jnp.exp(-x))


def _dot(a, b):
    return jnp.dot(a, b, preferred_element_type=F32)


def _dot_nt(a, b):
    return lax.dot_general(a, b, (((1,), (1,)), ((), ())), preferred_element_type=F32)


def _dot_tn(a, b):
    return lax.dot_general(a, b, (((0,), (0,)), ((), ())), preferred_element_type=F32)


def _split3(x):
    hi = x.astype(BF16)
    r1 = x - hi.astype(F32)
    mid = r1.astype(BF16)
    lo = (r1 - mid.astype(F32)).astype(BF16)
    return hi, mid, lo


def _dot_exact_lhs(m_bf16, x):
    hi, mid, lo = _split3(x)
    return _dot(m_bf16, hi) + _dot(m_bf16, mid) + _dot(m_bf16, lo)


def _dot_exact_rhs(x, m_bf16):
    hi, mid, lo = _split3(x)
    return _dot(hi, m_bf16) + _dot(mid, m_bf16) + _dot(lo, m_bf16)


def _split2(x):
    hi = x.astype(BF16)
    lo = (x - hi.astype(F32)).astype(BF16)
    return hi, lo


def _dot_hp(a, b):
    ah, al = _split2(a)
    bh, bl = _split2(b)
    return _dot(ah, bh) + _dot(ah, bl) + _dot(al, bh)


def _dot_nt_hp(a, b):
    ah, al = _split2(a)
    bh, bl = _split2(b)
    return _dot_nt(ah, bh) + _dot_nt(ah, bl) + _dot_nt(al, bh)


def _dot_tn_hp(a, b):
    ah, al = _split2(a)
    bh, bl = _split2(b)
    return _dot_tn(ah, bh) + _dot_tn(ah, bl) + _dot_tn(al, bh)


def _cparams(sem):
    return pltpu.CompilerParams(dimension_semantics=sem, vmem_limit_bytes=VMEM_LIMIT)


def _rmsnorm_kernel(x_ref, g_ref, o_ref):
    x = x_ref[...]
    ms = jnp.mean(x * x, axis=-1, keepdims=True)
    o_ref[...] = (x * lax.rsqrt(ms + NORM_EPS) * g_ref[...]).astype(o_ref.dtype)


def rmsnorm(x, gain, tm=256):
    m, d = x.shape
    return pl.pallas_call(
        _rmsnorm_kernel,
        grid=(m // tm,),
        in_specs=[pl.BlockSpec((tm, d), lambda i: (i, 0)), pl.BlockSpec((1, d), lambda i: (0, 0))],
        out_specs=pl.BlockSpec((tm, d), lambda i: (i, 0)),
        out_shape=jax.ShapeDtypeStruct((m, d), BF16),
        compiler_params=_cparams(("parallel",)),
        name="rmsnorm",
    )(x, gain.reshape(1, d))


def _wspec(w, layer, tn, jmap):
    return pl.BlockSpec((None, w.shape[1], tn), lambda *g: (layer, 0, jmap(*g)))


def _proj_in_kernel(h_ref, w_ref, o_ref):
    o_ref[...] = _dot(h_ref[...], w_ref[...]).astype(o_ref.dtype)


def proj_in(h, w, layer, out_dtype, tm=1024, tn=512):
    m, k = h.shape
    n = w.shape[2]
    tm = min(tm, m)
    return pl.pallas_call(
        _proj_in_kernel,
        grid=(m // tm, n // tn),
        in_specs=[pl.BlockSpec((tm, k), lambda i, j: (i, 0)), _wspec(w, layer, tn, lambda i, j: j)],
        out_specs=pl.BlockSpec((tm, tn), lambda i, j: (i, j)),
        out_shape=jax.ShapeDtypeStruct((m, n), out_dtype),
        compiler_params=_cparams(("parallel", "arbitrary")),
        name="proj_in",
    )(h, w)


def _pack_cols_kernel(*refs, shift, width):
    a_ref, o_ref = refs[0], refs[-1]
    t = pl.program_id(1)
    a = a_ref[...]
    if shift:
        a = jnp.concatenate([a[shift:], refs[1][0:shift]], axis=0)
    col = t * LANES + lax.broadcasted_iota(I32, a.shape, 0)
    o_ref[...] = jnp.where(col < width, a, 0.0).T.astype(o_ref.dtype)


def pack_cols(w_t, src_start, width, dst_width):
    depth, n_src, k = w_t.shape
    q0, shift = divmod(src_start, LANES)
    assert shift % SUBLANES == 0
    last = (n_src - 1) // LANES

    def src(extra):
        return pl.BlockSpec((None, LANES, k), lambda l, t: (l, jnp.minimum(q0 + t + extra, last), 0))

    w = w_t
    srcs = [src(0), src(1)] if shift else [src(0)]
    return pl.pallas_call(
        functools.partial(_pack_cols_kernel, shift=shift, width=width),
        grid=(depth, dst_width // LANES),
        in_specs=srcs,
        out_specs=pl.BlockSpec((None, k, LANES), lambda l, t: (l, 0, t)),
        out_shape=jax.ShapeDtypeStruct((depth, k, dst_width), BF16),
        compiler_params=_cparams(("parallel", "arbitrary")),
        name="pack_cols",
    )(*([w] * len(srcs)))


def _mm_res_kernel(a_ref, b_ref, r_ref, o_ref):
    o_ref[...] = r_ref[...] + _dot(a_ref[...], b_ref[...])


def matmul_residual(a, b, layer, res, tm=512, tn=512):
    m, k = a.shape
    n = b.shape[2]
    tm = min(tm, m)
    tn = min(tn, n)
    return pl.pallas_call(
        _mm_res_kernel,
        grid=(m // tm, n // tn),
        in_specs=[
            pl.BlockSpec((tm, k), lambda i, j: (i, 0)),
            _wspec(b, layer, tn, lambda i, j: j),
            pl.BlockSpec((tm, tn), lambda i, j: (i, j)),
        ],
        out_specs=pl.BlockSpec((tm, tn), lambda i, j: (i, j)),
        out_shape=jax.ShapeDtypeStruct((m, n), F32),
        compiler_params=_cparams(("parallel", "arbitrary")),
        name="matmul_residual",
    )(a, b, res)


def _merge_kernel(oa_ref, ob_ref, oc_ref, wa_ref, wb_ref, wc_ref, ga_ref, gb_ref, gc_ref, o_ref):
    acc = _sigmoid(ga_ref[...].astype(F32)) * _dot(oa_ref[...], wa_ref[...])
    acc += _sigmoid(gb_ref[...].astype(F32)) * _dot(ob_ref[...], wb_ref[...])
    acc += _sigmoid(gc_ref[...].astype(F32)) * _dot(oc_ref[...], wc_ref[...])
    o_ref[...] = acc.astype(o_ref.dtype)


def merge_branches(o_a, o_b, o_c, w_a, w_b, w_c, layer, p, d_model, tm=1024, tn=512):
    m = o_a.shape[0]
    tm = min(tm, m)
    g0 = 0
    gstep = d_model // tn
    return pl.pallas_call(
        _merge_kernel,
        grid=(m // tm, d_model // tn),
        in_specs=[
            pl.BlockSpec((tm, o_a.shape[1]), lambda i, j: (i, 0)),
            pl.BlockSpec((tm, o_b.shape[1]), lambda i, j: (i, 0)),
            pl.BlockSpec((tm, o_c.shape[1]), lambda i, j: (i, 0)),
            _wspec(w_a, layer, tn, lambda i, j: j),
            _wspec(w_b, layer, tn, lambda i, j: j),
            _wspec(w_c, layer, tn, lambda i, j: j),
            pl.BlockSpec((tm, tn), lambda i, j: (i, g0 + j)),
            pl.BlockSpec((tm, tn), lambda i, j: (i, g0 + gstep + j)),
            pl.BlockSpec((tm, tn), lambda i, j: (i, g0 + 2 * gstep + j)),
        ],
        out_specs=pl.BlockSpec((tm, tn), lambda i, j: (i, j)),
        out_shape=jax.ShapeDtypeStruct((m, d_model), BF16),
        compiler_params=_cparams(("parallel", "arbitrary")),
        name="merge_branches",
    )(o_a, o_b, o_c, w_a, w_b, w_c, p, p, p)


FFN_K_SPLIT = 8


def _ffn_up_kernel(h_ref, wg_ref, wu_ref, cwg_ref, cwu_ref, cbg_ref, cbu_ref, o_ref, w_sc, u0_sc, u1_sc, cg_ref,
                   cu_ref, *, tiles_per_seq, n_tiles, tm, ts):
    i = pl.program_id(1)
    tn = o_ref.shape[1]
    kdim = h_ref.shape[1]
    n_sub = tm // ts
    tk = kdim // FFN_K_SPLIT
    te = ts // FFN_K_SPLIT
    rows = lax.broadcasted_iota(I32, (te, tn), 0)
    slots = (u0_sc, u1_sc)

    @pl.when(i == 0)
    def _():
        w_sc[:, 0:tn] = wg_ref[...].astype(BF16)
        w_sc[:, tn:2 * tn] = wu_ref[...].astype(BF16)

    @pl.when((i > 0) & (((i - 1) % tiles_per_seq) == 0))
    def _():
        cg_ref[...] = jnp.zeros_like(cg_ref)
        cu_ref[...] = jnp.zeros_like(cu_ref)

    def conv(u, prev, cw_ref, cb_ref):
        u1 = jnp.where(rows == 0, prev[7:8, :], pltpu.roll(u, 1, axis=0))
        u2 = pltpu.roll(u, 2, axis=0)
        u2 = jnp.where(rows == 0, prev[6:7, :], jnp.where(rows == 1, prev[7:8, :], u2))
        cw = cw_ref[...]
        return cw[0:1, :] * u2 + cw[1:2, :] * u1 + cw[2:3, :] * u + cb_ref[...]

    def epilogue_piece(p, src, prev_g, prev_u):
        u = src[p * te:(p + 1) * te, :]
        ug = u[:, 0:tn]
        uu = u[:, tn:2 * tn]
        gate = conv(ug, prev_g, cwg_ref, cbg_ref)
        up = conv(uu, prev_u, cwu_ref, cbu_ref)
        o_ref[p * te:(p + 1) * te, :] = (gate * _sigmoid(gate) * up).astype(o_ref.dtype)
        return ug[te - SUBLANES:te, :], uu[te - SUBLANES:te, :]

    def run(mm_dst, ep_src):
        if ep_src is not None:
            prev_g, prev_u = cg_ref[...], cu_ref[...]
        for sb in range(n_sub):
            acc = None
            for kc in range(FFN_K_SPLIT):
                if mm_dst is not None:
                    part = _dot(h_ref[sb * ts:(sb + 1) * ts, kc * tk:(kc + 1) * tk], w_sc[kc * tk:(kc + 1) * tk, :])
                    acc = part if acc is None else acc + part
                if ep_src is not None:
                    prev_g, prev_u = epilogue_piece(sb * FFN_K_SPLIT + kc, ep_src, prev_g, prev_u)
            if mm_dst is not None:
                mm_dst[sb * ts:(sb + 1) * ts, :] = acc
        if ep_src is not None:
            cg_ref[...] = prev_g
            cu_ref[...] = prev_u

    @pl.when(i == 0)
    def _():
        run(slots[0], None)

    for parity in range(2):
        @pl.when((i > 0) & (i < n_tiles) & (i % 2 == parity))
        def _():
            run(slots[parity], slots[1 - parity])

    @pl.when(i == n_tiles)
    def _():
        run(None, slots[(n_tiles - 1) % 2])


def ffn_up(h, w_up, layer, conv_w, conv_b, seq, tm=1024, tn=256, ts=256):
    m, k = h.shape
    d_ff = w_up.shape[2] // 2
    tm = min(tm, seq)
    ts = min(ts, tm)
    nj = d_ff // tn
    n_tiles = m // tm
    return pl.pallas_call(
        functools.partial(_ffn_up_kernel, tiles_per_seq=seq // tm, n_tiles=n_tiles, tm=tm, ts=ts),
        grid=(nj, n_tiles + 1),
        in_specs=[
            pl.BlockSpec((tm, k), lambda j, i: (jnp.minimum(i, n_tiles - 1), 0)),
            _wspec(w_up, layer, tn, lambda j, i: j),
            _wspec(w_up, layer, tn, lambda j, i: nj + j),
            pl.BlockSpec((CONV_W, tn), lambda j, i: (0, j)),
            pl.BlockSpec((CONV_W, tn), lambda j, i: (0, nj + j)),
            pl.BlockSpec((1, tn), lambda j, i: (0, j)),
            pl.BlockSpec((1, tn), lambda j, i: (0, nj + j)),
        ],
        out_specs=pl.BlockSpec((tm, tn), lambda j, i: (jnp.maximum(i - 1, 0), j)),
        out_shape=jax.ShapeDtypeStruct((m, d_ff), BF16),
        scratch_shapes=[pltpu.VMEM((k, 2 * tn), BF16), pltpu.VMEM((tm, 2 * tn), F32), pltpu.VMEM((tm, 2 * tn), F32),
                        pltpu.VMEM((SUBLANES, tn), F32), pltpu.VMEM((SUBLANES, tn), F32)],
        compiler_params=_cparams(("parallel", "arbitrary")),
        name="ffn_up",
    )(h, w_up, w_up, conv_w, conv_w, conv_b, conv_b)


A_SUB = 16


def _hgrn_kernel(q_ref, f_ref, i_ref, g_ref, lb_ref, gain_ref, o_ref, st_ref, *, n_chunks, heads):
    @pl.when(pl.program_id(2) == 0)
    def _():
        st_ref[...] = jnp.zeros_like(st_ref)

    gain = gain_ref[...]
    tri = (lax.broadcasted_iota(I32, (CHUNK, CHUNK), 0) >= lax.broadcasted_iota(I32, (CHUNK, CHUNK), 1)).astype(BF16)
    rows_sub = lax.broadcasted_iota(I32, (A_SUB, A_DK), 0)
    n_sub = CHUNK // A_SUB

    def chunk(c, carry):
        rows = pl.ds(pl.multiple_of(c * CHUNK, CHUNK), CHUNK)
        hs = range(heads)
        cols = [slice(hh * A_DK, (hh + 1) * A_DK) for hh in hs]
        iv = [i_ref[rows, cl] for cl in cols]
        iv_b = [x.astype(BF16) for x in iv]
        st = [st_ref[hh] for hh in hs]
        qf, kin, b = [], [], []
        for hh in hs:
            lb = lb_ref[:, cols[hh]]
            z = f_ref[rows, cols[hh]]
            qv = q_ref[rows, cols[hh]]
            qf.append(qv * _sigmoid(qv))
            kin.append((1.0 - lb) * _sigmoid(-z))
            b.append(_dot_exact_lhs(tri, jnp.log(lb + (1.0 - lb) * _sigmoid(z))))
        o_inter = [_dot_nt((qf[hh] * jnp.exp(b[hh])).astype(BF16), st[hh].astype(BF16)) for hh in hs]
        outs = [[] for _ in hs]
        for si in range(n_sub):
            lo = si * A_SUB
            sub = slice(lo, lo + A_SUB)
            o_i = [o_inter[hh][sub] for hh in hs]
            if si > 0:
                att = []
                for hh in hs:
                    b_ref_row = b[hh][lo - 1:lo]
                    q_s = (qf[hh][sub] * jnp.exp(b[hh][sub] - b_ref_row)).astype(BF16)
                    k_s = (kin[hh][0:lo] * jnp.exp(b_ref_row - b[hh][0:lo])).astype(BF16)
                    att.append(_dot_nt(q_s, k_s))
                o_i = [o_i[hh] + _dot(att[hh].astype(BF16), iv_b[hh][0:lo]) for hh in hs]
            for s in range(A_SUB):
                for hh in hs:
                    b_i = b[hh][sub]
                    d = jnp.exp(jnp.where(rows_sub >= s, b_i - b_i[s:s + 1], -jnp.inf))
                    a = jnp.sum(qf[hh][sub] * d * kin[hh][lo + s:lo + s + 1], axis=1, keepdims=True)
                    o_i[hh] = o_i[hh] + a * iv[hh][lo + s:lo + s + 1]
            for hh in hs:
                outs[hh].append(o_i[hh])
        for hh in hs:
            b_last = b[hh][CHUNK - 1:CHUNK]
            k_dec = (kin[hh] * jnp.exp(b_last - b[hh])).astype(BF16)
            st_ref[hh] = st[hh] * jnp.exp(b_last) + _dot_tn(iv_b[hh], k_dec)
        for hh in hs:
            o = jnp.concatenate(outs[hh], axis=0)
            gv = g_ref[rows, cols[hh]]
            ms = jnp.mean(o * o, axis=-1, keepdims=True)
            on = o * lax.rsqrt(ms + NORM_EPS) * gain
            o_ref[rows, cols[hh]] = (on * (gv * _sigmoid(gv))).astype(o_ref.dtype)
        return carry

    lax.fori_loop(0, n_chunks, chunk, 0)


def hgrn2(p3, lb, out_gain, s_blk=256, heads=4):
    bsz, seq, _ = p3.shape
    s_blk = min(s_blk, seq)
    width = heads * A_DK
    groups = A_HEADS // heads
    c0 = COL_A // width

    def col(part):
        return pl.BlockSpec((None, s_blk, width), lambda b, h, s, part=part: (b, s, c0 + part * groups + h))

    return pl.pallas_call(
        functools.partial(_hgrn_kernel, n_chunks=s_blk // CHUNK, heads=heads),
        grid=(bsz, groups, seq // s_blk),
        in_specs=[col(0), col(1), col(2), col(3),
                  pl.BlockSpec((1, width), lambda b, h, s: (0, h)),
                  pl.BlockSpec((1, A_DV), lambda b, h, s: (0, 0))],
        out_specs=pl.BlockSpec((None, s_blk, width), lambda b, h, s: (b, s, h)),
        out_shape=jax.ShapeDtypeStruct((bsz, seq, A_W), BF16),
        scratch_shapes=[pltpu.VMEM((heads, A_DV, A_DK), F32)],
        compiler_params=_cparams(("parallel", "parallel", "arbitrary")),
        name="hgrn2",
    )(p3, p3, p3, p3, lb.reshape(1, A_HEADS * A_DK), out_gain.reshape(1, A_DV))


def _rwkv_prep_kernel(r_ref, k_ref, v_ref, l_ref, rp_ref, kp_ref, vp_ref, lp_ref,
                      mu_ref, w0_ref, w2_ref, a0_ref, a2_ref, g2_ref, kk_ref, ka_ref,
                      ro_ref, ld_ref, k2_ref, vo_ref, kko_ref, kka_ref, go_ref, *, tiles_per_seq, tm):
    first = (pl.program_id(0) % tiles_per_seq) == 0
    rows = lax.broadcasted_iota(I32, (tm, B_W), 0)

    def shifted(cur_ref, prev_ref, part):
        cur = cur_ref[...]
        prev = jnp.where(first, 0.0, prev_ref[...])[SUBLANES - 1:SUBLANES, :]
        sh = jnp.where(rows == 0, prev, pltpu.roll(cur, 1, axis=0))
        return cur + (sh - cur) * mu_ref[part:part + 1, :]

    r = shifted(r_ref, rp_ref, 0)
    k = shifted(k_ref, kp_ref, 1)
    v = shifted(v_ref, vp_ref, 2)
    lo = shifted(l_ref, lp_ref, 3)
    lo_a = lo[:, 0:LANES]
    lo_g = lo[:, LANES:3 * LANES]
    wpre = w0_ref[...] + _dot(jnp.tanh(lo_a).astype(BF16), w2_ref[...])
    y = -wpre
    softplus = jnp.maximum(y, 0.0) + jnp.log(1.0 + jnp.exp(-jnp.abs(y)))
    w_log = -softplus - 0.5
    ld_ref[...] = -jnp.exp(w_log)
    a = _sigmoid(a0_ref[...] + _dot(lo_a.astype(BF16), a2_ref[...]))
    go_ref[...] = _dot(_sigmoid(lo_g).astype(BF16), g2_ref[...])
    kk = k * kk_ref[...]
    bd = (lax.broadcasted_iota(I32, (LANES, LANES), 0) // B_HD
          == lax.broadcasted_iota(I32, (LANES, LANES), 1) // B_HD).astype(BF16)
    sq = kk * kk
    ss = jnp.concatenate([_dot_exact_rhs(sq[:, j * LANES:(j + 1) * LANES], bd) for j in range(B_W // LANES)], axis=1)
    kk = kk / jnp.maximum(jnp.sqrt(ss), 1e-12)
    ro_ref[...] = r
    vo_ref[...] = v
    kko_ref[...] = kk
    kka_ref[...] = kk * a
    k2_ref[...] = k * (1.0 + (a - 1.0) * ka_ref[...])


def rwkv_prep(p, seq, mu4, w0, w2p, a0, a2p, g2p, k_k, k_a, tm=256):
    m = p.shape[0]
    tm = min(tm, seq)
    cb = COL_B_RKV // B_W
    pb = tm // SUBLANES

    def cur(j):
        return pl.BlockSpec((tm, B_W), lambda i, j=j: (i, cb + j))

    def prev(j):
        return pl.BlockSpec((SUBLANES, B_W), lambda i, j=j: (jnp.maximum(i * pb - 1, 0), cb + j))

    def full(a):
        return pl.BlockSpec(a.shape, lambda i: (0, 0))

    row = lambda a: a.reshape(1, B_W)
    params = [mu4, row(w0), w2p, row(a0), a2p, g2p, row(k_k), row(k_a)]
    out = jax.ShapeDtypeStruct((m, B_W), F32)
    return pl.pallas_call(
        functools.partial(_rwkv_prep_kernel, tiles_per_seq=seq // tm, tm=tm),
        grid=(m // tm,),
        in_specs=[cur(0), cur(1), cur(2), cur(3), prev(0), prev(1), prev(2), prev(3)] + [full(a) for a in params],
        out_specs=[pl.BlockSpec((tm, B_W), lambda i: (i, 0))] * 7,
        out_shape=[out] * 7,
        compiler_params=_cparams(("parallel",)),
        name="rwkv_prep",
    )(p, p, p, p, p, p, p, p, *params)


B_T = 64


def _rwkv_kernel(r_ref, ld_ref, k_ref, v_ref, kk_ref, kka_ref, g_ref, rk_ref, lnw_ref, lnb_ref, o_ref, st_ref, *,
                 n_chunks, pairs):
    @pl.when(pl.program_id(2) == 0)
    def _():
        st_ref[...] = jnp.zeros_like(st_ref)

    t = B_T
    ii = lax.broadcasted_iota(I32, (t, t), 0)
    jj = lax.broadcasted_iota(I32, (t, t), 1)
    tri = (ii >= jj).astype(BF16)
    i2 = lax.broadcasted_iota(I32, (2 * t, 2 * t), 0)
    j2 = lax.broadcasted_iota(I32, (2 * t, 2 * t), 1)
    same = (i2 // t) == (j2 // t)
    strict_bd = same & ((i2 % t) > (j2 % t))
    incl_bd = same & ((i2 % t) >= (j2 % t))
    head_bd = same.astype(BF16)
    lane = lax.broadcasted_iota(I32, (t, LANES), 1)
    h0 = lane < B_HD

    def stack(x):
        return jnp.concatenate([jnp.where(h0, x, jnp.zeros_like(x)), jnp.where(h0, jnp.zeros_like(x), x)], axis=0)

    def head_sum(x):
        hi, lo = _split2(x)
        return _dot(hi, head_bd) + _dot(lo, head_bd)

    def chunk(c, carry):
        sl = pl.ds(pl.multiple_of(c * t, t), t)
        prs = range(pairs)
        cols = [slice(pp * LANES, (pp + 1) * LANES) for pp in prs]
        r = [r_ref[sl, cl] for cl in cols]
        ld = [ld_ref[sl, cl] for cl in cols]
        k = [k_ref[sl, cl] for cl in cols]
        v = [v_ref[sl, cl] for cl in cols]
        cs = []
        for pp in prs:
            ld_hi, ld_lo = _split2(ld[pp])
            cs.append(_dot(tri, ld_hi) + _dot(tri, ld_lo))
        lhs, rhs, vs = [], [], []
        for pp in prs:
            e_neg = jnp.exp(-cs[pp])
            kka = kka_ref[sl, cols[pp]]
            a_t = (-kk_ref[sl, cols[pp]] * jnp.exp(cs[pp] - ld[pp])).astype(BF16)
            r_t = (r[pp] * jnp.exp(cs[pp])).astype(BF16)
            b_t = (kka * e_neg).astype(BF16)
            k_t = (k[pp] * e_neg).astype(BF16)
            lhs.append(jnp.concatenate([stack(a_t), stack(r_t)], axis=0))
            rhs.append(jnp.concatenate([b_t, b_t, k_t, k_t], axis=0))
            vs.append(stack(v[pp].astype(BF16)))
        sc = [_dot_nt(lhs[pp], rhs[pp]) for pp in prs]
        st = [st_ref[pp] for pp in prs]
        proj = [_dot_nt(lhs[pp], st[pp].astype(BF16)) for pp in prs]
        n = [jnp.where(strict_bd, sc[pp][0:2 * t, 0:2 * t], 0.0).astype(BF16) for pp in prs]
        xs = [proj[pp][0:2 * t] + _dot(jnp.where(strict_bd, sc[pp][0:2 * t, 2 * t:4 * t], 0.0).astype(BF16), vs[pp])
              for pp in prs]
        for it in range(6):
            xs = [xs[pp] + _dot(n[pp], xs[pp].astype(BF16)) for pp in prs]
            if it < 5:
                n = [_dot(n[pp], n[pp]).astype(BF16) for pp in prs]
        os_ = []
        for pp in prs:
            m_r = jnp.concatenate([jnp.where(incl_bd, sc[pp][2 * t:4 * t, 0:2 * t], 0.0),
                                   jnp.where(incl_bd, sc[pp][2 * t:4 * t, 2 * t:4 * t], 0.0)], axis=1)
            uv = jnp.concatenate([xs[pp].astype(BF16), vs[pp]], axis=0)
            os_.append(proj[pp][2 * t:4 * t] + _dot(m_r.astype(BF16), uv))
        upd = []
        for pp in prs:
            u = xs[pp][0:t] + xs[pp][t:2 * t]
            c_last = cs[pp][t - 1:t]
            dec = jnp.exp(c_last - cs[pp])
            upd.append(_dot_tn(jnp.concatenate([u, v[pp]], axis=0).astype(BF16),
                               jnp.concatenate([kka_ref[sl, cols[pp]] * dec, k[pp] * dec], axis=0).astype(BF16)))
        for pp in prs:
            st_ref[pp] = st[pp] * jnp.exp(cs[pp][t - 1:t]) + jnp.where(same, upd[pp], 0.0)
        inv = 1.0 / B_HD
        o = [os_[pp][0:t] + os_[pp][t:2 * t] for pp in prs]
        mean = [head_sum(o[pp]) * inv for pp in prs]
        d = [o[pp] - mean[pp] for pp in prs]
        var = [head_sum(d[pp] * d[pp]) * inv for pp in prs]
        bonus = [head_sum(r[pp] * k[pp] * rk_ref[:, cols[pp]]) for pp in prs]
        for pp in prs:
            on = d[pp] * lax.rsqrt(var[pp] + B_GN_EPS) * lnw_ref[:, cols[pp]] + lnb_ref[:, cols[pp]]
            o_ref[sl, cols[pp]] = ((on + bonus[pp] * v[pp]) * g_ref[sl, cols[pp]]).astype(o_ref.dtype)
        return carry

    lax.fori_loop(0, n_chunks, chunk, 0)


def rwkv_recurrence(r, ld, k2, v, kk, kka, g, r_k, ln_w, ln_b, bsz, seq, s_blk=256, pairs=8):
    s_blk = min(s_blk, seq)
    width = pairs * LANES
    groups = B_W // width
    args = [a.reshape(bsz, seq, B_W) for a in (r, ld, k2, v, kk, kka, g)]
    blk = pl.BlockSpec((None, s_blk, width), lambda b, h, s: (b, s, h))
    par = pl.BlockSpec((1, width), lambda b, h, s: (0, h))
    return pl.pallas_call(
        functools.partial(_rwkv_kernel, n_chunks=s_blk // B_T, pairs=pairs),
        grid=(bsz, groups, seq // s_blk),
        in_specs=[blk] * 7 + [par] * 3,
        out_specs=blk,
        out_shape=jax.ShapeDtypeStruct((bsz, seq, B_W), BF16),
        scratch_shapes=[pltpu.VMEM((pairs, LANES, LANES), F32)],
        compiler_params=_cparams(("parallel", "parallel", "arbitrary")),
        name="rwkv_recurrence",
    )(*args, r_k.reshape(1, B_W), ln_w.reshape(1, B_W), ln_b.reshape(1, B_W))


def _rope(x, cos_t, sin_t, lane_in_head, half):
    partner = jnp.where(lane_in_head < half, pltpu.roll(x, LANES - half, axis=1), pltpu.roll(x, half, axis=1))
    return x * cos_t + partner * sin_t


def _attn_prep_kernel(q_ref, k_ref, v_ref, qi_ref, kw_ref, ca_ref, sa_ref, ci_ref, si_ref, qg_ref, kg_ref,
                      qo_ref, ko_ref, vo_ref, qio_ref, kio_ref, wo_ref, *, tm):
    ca, sa, ci, si = ca_ref[...], sa_ref[...], ci_ref[...], si_ref[...]
    lane = lax.broadcasted_iota(I32, (tm, LANES), 1)
    lane_i = lane % IDX_HD
    half_a = C_HD // ROPE_FRACTION // 2
    half_i = IDX_HD // ROPE_FRACTION // 2
    scale = C_HD ** -0.5 * LOG2_E

    def norm_rope(x, gain):
        ms = jnp.mean(x * x, axis=-1, keepdims=True)
        return _rope(x * lax.rsqrt(ms + NORM_EPS) * gain, ca, sa, lane, half_a)

    for h in range(C_HEADS):
        sl = slice(h * C_HD, (h + 1) * C_HD)
        qo_ref[:, sl] = (norm_rope(q_ref[:, sl], qg_ref[...]) * scale).astype(qo_ref.dtype)
    for h in range(C_KV_HEADS):
        sl = slice(h * C_HD, (h + 1) * C_HD)
        ko_ref[:, sl] = norm_rope(k_ref[:, sl], kg_ref[...]).astype(ko_ref.dtype)
    vo_ref[...] = v_ref[...].astype(vo_ref.dtype)
    for j in range(IDX_HEADS * IDX_HD // LANES):
        sl = slice(j * LANES, (j + 1) * LANES)
        qio_ref[:, sl] = _rope(qi_ref[:, sl], ci, si, lane_i, half_i).astype(qio_ref.dtype)
    kw = kw_ref[:, 0:LANES]
    kr = _rope(kw, ci, si, lane_i, half_i)
    kio_ref[...] = jnp.where(lane < IDX_HD, kr, pltpu.roll(kr, IDX_HD, axis=1)).astype(kio_ref.dtype)
    w = pltpu.roll(kw, LANES - IDX_HD, axis=1) * (IDX_HEADS ** -0.5 * IDX_HD ** -0.5)
    wo_ref[...] = jnp.where(lane < IDX_HEADS, w, 0.0)


def attn_prep(p, tabs, q_gain, k_gain, tm=256):
    m = p.shape[0]
    tm = min(tm, m)

    def colblk(width, off):
        return pl.BlockSpec((tm, width), lambda i: (i, off // width))

    tab = pl.BlockSpec((tm, LANES), lambda i: (i, 0))
    gain = pl.BlockSpec((1, C_HD), lambda i: (0, 0))
    kw_width = 512

    def out(width, dtype):
        return jax.ShapeDtypeStruct((m, width), dtype), pl.BlockSpec((tm, width), lambda i: (i, 0))

    outs = [out(C_W, BF16), out(C_KV_W, BF16), out(C_KV_W, BF16), out(IDX_HEADS * IDX_HD, BF16),
            out(LANES, BF16), out(LANES, F32)]
    return pl.pallas_call(
        functools.partial(_attn_prep_kernel, tm=tm),
        grid=(m // tm,),
        in_specs=[colblk(C_W, COL_C_Q), colblk(C_KV_W, COL_C_K), colblk(C_KV_W, COL_C_V),
                  colblk(IDX_HEADS * IDX_HD, COL_C_QI), colblk(kw_width, COL_C_KW), tab, tab, tab, tab, gain, gain],
        out_specs=[o[1] for o in outs],
        out_shape=[o[0] for o in outs],
        compiler_params=_cparams(("parallel",)),
        name="attn_prep",
    )(p, p, p, p, p, *tabs, q_gain.reshape(1, C_HD), k_gain.reshape(1, C_HD))


V_ONES = 16


def _attn_kernel(q_ref, k_ref, vt_ref, qi_ref, ki_ref, w_ref, o_ref, key_sc, bias_sc, m_sc, acc_sc, *, tq, tk, topk):
    qb = pl.program_id(1)
    nkb = ((qb + 1) * tq + tk - 1) // tk
    lane = lax.broadcasted_iota(I32, (tq, LANES), 1)
    w_t = w_ref[...].T
    q_chunk = lax.shift_right_logical(qb * tq + lax.broadcasted_iota(I32, (1, tq), 1), CHUNK_SHIFT)
    key_chunk_in_blk = lax.shift_right_logical(lax.broadcasted_iota(I32, (tk, tq), 0), CHUNK_SHIFT)
    n_pairs = IDX_HEADS * IDX_HD // LANES
    fold = 64

    q_pairs = []
    for hp in range(n_pairs):
        qp = qi_ref[:, hp * LANES:(hp + 1) * LANES]
        zero = jnp.zeros_like(qp)
        q_pairs.append(jnp.concatenate([jnp.where(lane < IDX_HD, qp, zero), jnp.where(lane < IDX_HD, zero, qp)], axis=0))

    def score_block(kb, carry):
        c0 = pl.multiple_of(kb * tk, tk)
        ki2 = ki_ref[pl.ds(c0, tk), :]
        sc = jnp.zeros((tk, tq), F32)
        for hp in range(n_pairs):
            rel = jnp.maximum(_dot_nt(ki2, q_pairs[hp]), 0.0)
            sc = sc + w_t[2 * hp:2 * hp + 1, :] * rel[:, 0:tq] + w_t[2 * hp + 1:2 * hp + 2, :] * rel[:, tq:2 * tq]
        sc = jnp.where(sc == 0.0, 0.0, sc)
        bits = lax.bitcast_convert_type(sc, I32)
        skey = bits ^ ((bits >> 31) & 0x7FFFFFFF)
        allowed = key_chunk_in_blk <= q_chunk - kb * (tk // CHUNK)
        key_sc[kb] = jnp.where(allowed, skey, INT_MIN)
        return carry

    lax.fori_loop(0, nkb, score_block, 0)

    def count_ge(cand):
        def body(kb, acc):
            hit = jnp.where(key_sc[kb] >= cand, 1.0, 0.0)
            for j in range(tk // fold):
                acc = acc + hit[j * fold:(j + 1) * fold]
            return acc
        acc = lax.fori_loop(0, nkb, body, jnp.zeros((fold, tq), F32))
        return jnp.sum(acc, axis=0, keepdims=True)

    kf = float(topk)
    thr = jnp.where(count_ge(jnp.zeros((1, tq), I32)) >= kf, 0, INT_MIN).astype(I32)

    def bit_step(i, thr):
        cand = thr | (jnp.int32(1) << (30 - i))
        return jnp.where(count_ge(cand) >= kf, cand, thr)

    thr = lax.fori_loop(0, 31, bit_step, thr)
    thr = jnp.maximum(thr, INT_MIN + 1)
    need = kf - count_ge(thr + 1)

    tri = (lax.broadcasted_iota(I32, (tk, tk), 0) >= lax.broadcasted_iota(I32, (tk, tk), 1)).astype(BF16)

    def bias_block(kb, seen):
        keys = key_sc[kb]
        eq = keys == thr
        eq_f = jnp.where(eq, 1.0, 0.0)
        rank = _dot(tri, eq_f.astype(BF16)) + seen
        take = (keys > thr) | (eq & (rank <= need))
        bias_sc[kb] = jnp.where(take, 0.0, NEG_BIG)
        return rank[tk - 1:tk]

    lax.fori_loop(0, nkb, bias_block, jnp.zeros((1, tq), F32))

    rep = C_HEADS // C_KV_HEADS
    ve = C_HD + V_ONES
    qs = [jnp.concatenate([q_ref[:, (g * rep + r) * C_HD:(g * rep + r + 1) * C_HD] for r in range(rep)], axis=0)
          for g in range(C_KV_HEADS)]
    m_sc[...] = jnp.full(m_sc.shape, NEG_BIG, F32)
    acc_sc[...] = jnp.zeros(acc_sc.shape, F32)

    def body(kb, carry):
        c0 = pl.multiple_of(kb * tk, tk)
        bias = bias_sc[kb]
        s_all = [_dot_nt(k_ref[pl.ds(c0, tk), g * C_HD:(g + 1) * C_HD], qs[g]) for g in range(C_KV_HEADS)]
        for g in range(C_KV_HEADS):
            s = jnp.concatenate([s_all[g][:, r * tq:(r + 1) * tq] + bias for r in range(rep)], axis=1)
            m_prev = m_sc[g]
            m_new = jnp.maximum(m_prev, jnp.max(s, axis=0, keepdims=True))
            p = jnp.exp2(s - m_new)
            acc_sc[g] = jnp.exp2(m_prev - m_new) * acc_sc[g] + _dot(vt_ref[kb, g * ve:(g + 1) * ve, :], p.astype(BF16))
            m_sc[g] = m_new
        return carry

    lax.fori_loop(0, nkb, body, 0)
    for g in range(C_KV_HEADS):
        acc = acc_sc[g]
        out_t = acc[0:C_HD] / acc[C_HD:C_HD + 1]
        for r in range(rep):
            h = g * rep + r
            o_ref[:, h * C_HD:(h + 1) * C_HD] = out_t[:, r * tq:(r + 1) * tq].T.astype(o_ref.dtype)


def sparse_attention(qn, kn, vn, qi, ki2, w, bsz, seq, tq=128, tk=512):
    tk = min(tk, seq)
    topk = min(TOPK_MAX, seq // 4)
    nkb = seq // tk
    rep = C_HEADS // C_KV_HEADS
    ve = C_HD + V_ONES
    r3 = lambda a: a.reshape(bsz, seq, a.shape[-1])
    v_t = vn.reshape(bsz, nkb, tk, C_KV_HEADS, C_HD).transpose(0, 1, 3, 4, 2)
    v_t = jnp.concatenate([v_t, jnp.ones((bsz, nkb, C_KV_HEADS, V_ONES, tk), v_t.dtype)], axis=3)
    v_t = v_t.reshape(bsz, nkb, C_KV_HEADS * ve, tk)
    qblk = lambda width: pl.BlockSpec((None, tq, width), lambda b, i: (b, i, 0))
    sblk = lambda width: pl.BlockSpec((None, seq, width), lambda b, i: (b, 0, 0))
    return pl.pallas_call(
        functools.partial(_attn_kernel, tq=tq, tk=tk, topk=topk),
        grid=(bsz, seq // tq),
        in_specs=[qblk(C_W), sblk(C_KV_W), pl.BlockSpec((None, nkb, C_KV_HEADS * ve, tk), lambda b, i: (b, 0, 0, 0)),
                  qblk(IDX_HEADS * IDX_HD), sblk(LANES), qblk(LANES)],
        out_specs=qblk(C_W),
        out_shape=jax.ShapeDtypeStruct((bsz, seq, C_W), BF16),
        scratch_shapes=[pltpu.VMEM((nkb, tk, tq), I32), pltpu.VMEM((nkb, tk, tq), F32),
                        pltpu.VMEM((C_KV_HEADS, 1, rep * tq), F32), pltpu.VMEM((C_KV_HEADS, ve, rep * tq), F32)],
        compiler_params=_cparams(("parallel", "arbitrary")),
        name="sparse_attention",
    )(r3(qn), r3(kn), v_t, r3(qi), r3(ki2), r3(w))


def _rope_tables(positions):
    pos = positions.astype(F32).reshape(-1, 1)

    def tables(head_dim):
        rot = head_dim // ROPE_FRACTION
        half = rot // 2
        inv_freq = ROPE_THETA ** (-jnp.arange(0, rot, 2, dtype=F32) / rot)
        ang = pos * inv_freq
        cos, sin = jnp.cos(ang), jnp.sin(ang)
        ones = jnp.ones((pos.shape[0], head_dim - rot), F32)
        c = jnp.concatenate([cos, cos, ones], axis=1)
        s = jnp.concatenate([-sin, sin, 0.0 * ones], axis=1)
        reps = LANES // head_dim
        return jnp.tile(c, (1, reps)), jnp.tile(s, (1, reps))

    ca, sa = tables(C_HD)
    ci, si = tables(IDX_HD)
    return ca, sa, ci, si


def _pad_rows(a, before, total):
    return jnp.zeros((total, a.shape[1]), a.dtype).at[before:before + a.shape[0]].set(a)


def kernel(x, positions, norm_mix, w_in, hgrn_lb_logits, hgrn_out_norm, rwkv_mu, rwkv_w0, rwkv_w2, rwkv_a0, rwkv_a2,
           rwkv_g2, rwkv_k_k, rwkv_k_a, rwkv_r_k, rwkv_ln_w, rwkv_ln_b, q_norm, k_norm, w_branch_a, w_branch_b,
           w_branch_c, w_out, norm_ffn, w_up, conv_w, conv_b, w_down):
    bsz, seq, d_model = x.shape
    depth = w_in.shape[0]
    m = bsz * seq
    tabs = _rope_tables(positions)
    lb_all = jnp.cumsum(jax.nn.softmax(hgrn_lb_logits.astype(F32), axis=0), axis=0)
    lb_all = lb_all - lb_all[0:1]
    n_lora = B_LORA_DECAY + B_LORA_A + B_LORA_G
    ab_cols = 4 * A_W + 3 * B_W + n_lora
    c_cols = C_W + 2 * C_KV_W + IDX_HEADS * IDX_HD + IDX_HD + IDX_HEADS
    w_in_t = jnp.swapaxes(w_in, 1, 2)
    w_in_ab = pack_cols(w_in_t, 0, ab_cols, W_AB)
    w_in_c = pack_cols(w_in_t, ab_cols, c_cols, W_C)
    w_in_g = pack_cols(w_in_t, ab_cols + c_cols, 3 * d_model, 3 * d_model)
    w_a, w_b, w_c = w_branch_a.astype(BF16), w_branch_b.astype(BF16), w_branch_c.astype(BF16)
    w_o, w_d = w_out.astype(BF16), w_down.astype(BF16)

    xf = x.reshape(m, d_model)
    for l in range(depth):
        h = rmsnorm(xf, norm_mix[l])
        p = proj_in(h, w_in_ab, l, F32)
        p_c = proj_in(h, w_in_c, l, F32)
        p_g = proj_in(h, w_in_g, l, BF16)

        o_a = hgrn2(p.reshape(bsz, seq, -1), lb_all[l], hgrn_out_norm[l]).reshape(m, A_W)

        mu = rwkv_mu[l]
        mu4 = jnp.stack([mu[0:B_W], mu[B_W:2 * B_W], mu[2 * B_W:3 * B_W],
                         jnp.pad(mu[3 * B_W:], (0, B_W - n_lora))])
        w2p = _pad_rows(rwkv_w2[l], 0, LANES).astype(BF16)
        a2p = _pad_rows(rwkv_a2[l], B_LORA_DECAY, LANES).astype(BF16)
        g2p = _pad_rows(rwkv_g2[l], 0, 2 * LANES).astype(BF16)
        parts = rwkv_prep(p, seq, mu4, rwkv_w0[l], w2p, rwkv_a0[l], a2p, g2p, rwkv_k_k[l], rwkv_k_a[l])
        o_b = rwkv_recurrence(*parts, rwkv_r_k[l], rwkv_ln_w[l], rwkv_ln_b[l], bsz, seq).reshape(m, B_W)

        qn, kn, vn, qi, ki2, wi = attn_prep(p_c, tabs, q_norm[l], k_norm[l])
        o_c = sparse_attention(qn, kn, vn, qi, ki2, wi, bsz, seq).reshape(m, C_W)

        merged = merge_branches(o_a, o_b, o_c, w_a, w_b, w_c, l, p_g, d_model)
        xf = matmul_residual(merged, w_o, l, xf, tm=1024, tn=512)

        h2 = rmsnorm(xf, norm_ffn[l])
        act = ffn_up(h2, w_up, l, conv_w[l], conv_b[l].reshape(1, -1), seq)
        xf = matmul_residual(act, w_d, l, xf, tm=512, tn=512)
    return xf.reshape(bsz, seq, d_model)
```

```python
import functools

import jax
import jax.numpy as jnp
import numpy as np
from jax import lax
from jax.experimental import pallas as pl
from jax.experimental.pallas import tpu as pltpu

F32 = jnp.float32
BF16 = jnp.bfloat16
I32 = jnp.int32

CHUNK = 64
CHUNK_SHIFT = 6
ROPE_THETA = 500000.0
ROPE_FRACTION = 4
NORM_EPS = 1e-6
A_HEADS, A_DK, A_DV = 8, 128, 128
A_W = A_HEADS * A_DV
B_HEADS, B_HD = 16, 64
B_W = B_HEADS * B_HD
B_LORA_DECAY, B_LORA_A, B_LORA_G = 64, 64, 160
B_GN_EPS = 64e-5
C_HEADS, C_KV_HEADS, C_HD = 16, 4, 128
C_W = C_HEADS * C_HD
C_KV_W = C_KV_HEADS * C_HD
IDX_HEADS, IDX_HD = 16, 64
TOPK_MAX = 256
CONV_W = 3

LANES = 128
SUBLANES = 8
VMEM_LIMIT = 56 * 1024 * 1024

COL_A = 0
COL_B_RKV = 4096
COL_B_LORA = 7168
W_AB = 8192
COL_C_Q = 0
COL_C_K = 2048
COL_C_V = 2560
COL_C_QI = 3072
COL_C_KW = 4096
W_C = 4608
LOG2_E = 1.4426950408889634
NEG_BIG = -1e30
INT_MIN = -2147483648


def _sigmoid(x):
    return 1.0 / (1.0 + jnp.exp(-x))


def _dot(a, b):
    return jnp.dot(a, b, preferred_element_type=F32)


def _dot_nt(a, b):
    return lax.dot_general(a, b, (((1,), (1,)), ((), ())), preferred_element_type=F32)


def _dot_tn(a, b):
    return lax.dot_general(a, b, (((0,), (0,)), ((), ())), preferred_element_type=F32)


def _split3(x):
    hi = x.astype(BF16)
    r1 = x - hi.astype(F32)
    mid = r1.astype(BF16)
    lo = (r1 - mid.astype(F32)).astype(BF16)
    return hi, mid, lo


def _dot_exact_lhs(m_bf16, x):
    hi, mid, lo = _split3(x)
    return _dot(m_bf16, hi) + _dot(m_bf16, mid) + _dot(m_bf16, lo)


def _dot_exact_rhs(x, m_bf16):
    hi, mid, lo = _split3(x)
    return _dot(hi, m_bf16) + _dot(mid, m_bf16) + _dot(lo, m_bf16)


def _split2(x):
    hi = x.astype(BF16)
    lo = (x - hi.astype(F32)).astype(BF16)
    return hi, lo


def _dot_hp(a, b):
    ah, al = _split2(a)
    bh, bl = _split2(b)
    return _dot(ah, bh) + _dot(ah, bl) + _dot(al, bh)


def _dot_nt_hp(a, b):
    ah, al = _split2(a)
    bh, bl = _split2(b)
    return _dot_nt(ah, bh) + _dot_nt(ah, bl) + _dot_nt(al, bh)


def _dot_tn_hp(a, b):
    ah, al = _split2(a)
    bh, bl = _split2(b)
    return _dot_tn(ah, bh) + _dot_tn(ah, bl) + _dot_tn(al, bh)


def _cparams(sem):
    return pltpu.CompilerParams(dimension_semantics=sem, vmem_limit_bytes=VMEM_LIMIT)


def _rmsnorm_kernel(x_ref, g_ref, o_ref):
    x = x_ref[...]
    ms = jnp.mean(x * x, axis=-1, keepdims=True)
    o_ref[...] = (x * lax.rsqrt(ms + NORM_EPS) * g_ref[...]).astype(o_ref.dtype)


def rmsnorm(x, gain, tm=256):
    m, d = x.shape
    return pl.pallas_call(
        _rmsnorm_kernel,
        grid=(m // tm,),
        in_specs=[pl.BlockSpec((tm, d), lambda i: (i, 0)), pl.BlockSpec((1, d), lambda i: (0, 0))],
        out_specs=pl.BlockSpec((tm, d), lambda i: (i, 0)),
        out_shape=jax.ShapeDtypeStruct((m, d), BF16),
        compiler_params=_cparams(("parallel",)),
        name="rmsnorm",
    )(x, gain.reshape(1, d))


def _wspec(w, layer, tn, jmap):
    return pl.BlockSpec((None, w.shape[1], tn), lambda *g: (layer, 0, jmap(*g)))


def _proj_in_kernel(h_ref, w_ref, o_ref):
    o_ref[...] = _dot(h_ref[...], w_ref[...]).astype(o_ref.dtype)


def proj_in(h, w, layer, out_dtype, tm=1024, tn=512):
    m, k = h.shape
    n = w.shape[2]
    tm = min(tm, m)
    return pl.pallas_call(
        _proj_in_kernel,
        grid=(m // tm, n // tn),
        in_specs=[pl.BlockSpec((tm, k), lambda i, j: (i, 0)), _wspec(w, layer, tn, lambda i, j: j)],
        out_specs=pl.BlockSpec((tm, tn), lambda i, j: (i, j)),
        out_shape=jax.ShapeDtypeStruct((m, n), out_dtype),
        compiler_params=_cparams(("parallel", "arbitrary")),
        name="proj_in",
    )(h, w)


def _pack_cols_kernel(*refs, shift, width):
    a_ref, o_ref = refs[0], refs[-1]
    t = pl.program_id(1)
    a = a_ref[...]
    if shift:
        a = jnp.concatenate([a[shift:], refs[1][0:shift]], axis=0)
    col = t * LANES + lax.broadcasted_iota(I32, a.shape, 0)
    o_ref[...] = jnp.where(col < width, a, 0.0).T.astype(o_ref.dtype)


def pack_cols(w_t, src_start, width, dst_width):
    depth, n_src, k = w_t.shape
    q0, shift = divmod(src_start, LANES)
    assert shift % SUBLANES == 0
    last = (n_src - 1) // LANES

    def src(extra):
        return pl.BlockSpec((None, LANES, k), lambda l, t: (l, jnp.minimum(q0 + t + extra, last), 0))

    w = w_t
    srcs = [src(0), src(1)] if shift else [src(0)]
    return pl.pallas_call(
        functools.partial(_pack_cols_kernel, shift=shift, width=width),
        grid=(depth, dst_width // LANES),
        in_specs=srcs,
        out_specs=pl.BlockSpec((None, k, LANES), lambda l, t: (l, 0, t)),
        out_shape=jax.ShapeDtypeStruct((depth, k, dst_width), BF16),
        compiler_params=_cparams(("parallel", "arbitrary")),
        name="pack_cols",
    )(*([w] * len(srcs)))


def _mm_res_kernel(a_ref, b_ref, r_ref, o_ref):
    o_ref[...] = r_ref[...] + _dot(a_ref[...], b_ref[...])


def matmul_residual(a, b, layer, res, tm=512, tn=512):
    m, k = a.shape
    n = b.shape[2]
    tm = min(tm, m)
    tn = min(tn, n)
    return pl.pallas_call(
        _mm_res_kernel,
        grid=(m // tm, n // tn),
        in_specs=[
            pl.BlockSpec((tm, k), lambda i, j: (i, 0)),
            _wspec(b, layer, tn, lambda i, j: j),
            pl.BlockSpec((tm, tn), lambda i, j: (i, j)),
        ],
        out_specs=pl.BlockSpec((tm, tn), lambda i, j: (i, j)),
        out_shape=jax.ShapeDtypeStruct((m, n), F32),
        compiler_params=_cparams(("parallel", "arbitrary")),
        name="matmul_residual",
    )(a, b, res)


def _merge_kernel(oa_ref, ob_ref, oc_ref, wa_ref, wb_ref, wc_ref, ga_ref, gb_ref, gc_ref, o_ref):
    acc = _sigmoid(ga_ref[...].astype(F32)) * _dot(oa_ref[...], wa_ref[...])
    acc += _sigmoid(gb_ref[...].astype(F32)) * _dot(ob_ref[...], wb_ref[...])
    acc += _sigmoid(gc_ref[...].astype(F32)) * _dot(oc_ref[...], wc_ref[...])
    o_ref[...] = acc.astype(o_ref.dtype)


def merge_branches(o_a, o_b, o_c, w_a, w_b, w_c, layer, p, d_model, tm=1024, tn=512):
    m = o_a.shape[0]
    tm = min(tm, m)
    g0 = 0
    gstep = d_model // tn
    return pl.pallas_call(
        _merge_kernel,
        grid=(m // tm, d_model // tn),
        in_specs=[
            pl.BlockSpec((tm, o_a.shape[1]), lambda i, j: (i, 0)),
            pl.BlockSpec((tm, o_b.shape[1]), lambda i, j: (i, 0)),
            pl.BlockSpec((tm, o_c.shape[1]), lambda i, j: (i, 0)),
            _wspec(w_a, layer, tn, lambda i, j: j),
            _wspec(w_b, layer, tn, lambda i, j: j),
            _wspec(w_c, layer, tn, lambda i, j: j),
            pl.BlockSpec((tm, tn), lambda i, j: (i, g0 + j)),
            pl.BlockSpec((tm, tn), lambda i, j: (i, g0 + gstep + j)),
            pl.BlockSpec((tm, tn), lambda i, j: (i, g0 + 2 * gstep + j)),
        ],
        out_specs=pl.BlockSpec((tm, tn), lambda i, j: (i, j)),
        out_shape=jax.ShapeDtypeStruct((m, d_model), BF16),
        compiler_params=_cparams(("parallel", "arbitrary")),
        name="merge_branches",
    )(o_a, o_b, o_c, w_a, w_b, w_c, p, p, p)


FFN_K_SPLIT = 8


def _ffn_up_kernel(h_ref, wg_ref, wu_ref, cwg_ref, cwu_ref, cbg_ref, cbu_ref, o_ref, w_sc, u0_sc, u1_sc, cg_ref,
                   cu_ref, *, tiles_per_seq, n_tiles, tm, ts):
    i = pl.program_id(1)
    tn = o_ref.shape[1]
    kdim = h_ref.shape[1]
    n_sub = tm // ts
    tk = kdim // FFN_K_SPLIT
    te = ts // FFN_K_SPLIT
    rows = lax.broadcasted_iota(I32, (te, tn), 0)
    slots = (u0_sc, u1_sc)

    @pl.when(i == 0)
    def _():
        w_sc[:, 0:tn] = wg_ref[...].astype(BF16)
        w_sc[:, tn:2 * tn] = wu_ref[...].astype(BF16)

    @pl.when((i > 0) & (((i - 1) % tiles_per_seq) == 0))
    def _():
        cg_ref[...] = jnp.zeros_like(cg_ref)
        cu_ref[...] = jnp.zeros_like(cu_ref)

    def conv(u, prev, cw_ref, cb_ref):
        u1 = jnp.where(rows == 0, prev[7:8, :], pltpu.roll(u, 1, axis=0))
        u2 = pltpu.roll(u, 2, axis=0)
        u2 = jnp.where(rows == 0, prev[6:7, :], jnp.where(rows == 1, prev[7:8, :], u2))
        cw = cw_ref[...]
        return cw[0:1, :] * u2 + cw[1:2, :] * u1 + cw[2:3, :] * u + cb_ref[...]

    def epilogue_piece(p, src, prev_g, prev_u):
        u = src[p * te:(p + 1) * te, :]
        ug = u[:, 0:tn]
        uu = u[:, tn:2 * tn]
        gate = conv(ug, prev_g, cwg_ref, cbg_ref)
        up = conv(uu, prev_u, cwu_ref, cbu_ref)
        o_ref[p * te:(p + 1) * te, :] = (gate * _sigmoid(gate) * up).astype(o_ref.dtype)
        return ug[te - SUBLANES:te, :], uu[te - SUBLANES:te, :]

    def run(mm_dst, ep_src):
        if ep_src is not None:
            prev_g, prev_u = cg_ref[...], cu_ref[...]
        for sb in range(n_sub):
            acc = None
            for kc in range(FFN_K_SPLIT):
                if mm_dst is not None:
                    part = _dot(h_ref[sb * ts:(sb + 1) * ts, kc * tk:(kc + 1) * tk], w_sc[kc * tk:(kc + 1) * tk, :])
                    acc = part if acc is None else acc + part
                if ep_src is not None:
                    prev_g, prev_u = epilogue_piece(sb * FFN_K_SPLIT + kc, ep_src, prev_g, prev_u)
            if mm_dst is not None:
                mm_dst[sb * ts:(sb + 1) * ts, :] = acc
        if ep_src is not None:
            cg_ref[...] = prev_g
            cu_ref[...] = prev_u

    @pl.when(i == 0)
    def _():
        run(slots[0], None)

    for parity in range(2):
        @pl.when((i > 0) & (i < n_tiles) & (i % 2 == parity))
        def _():
            run(slots[parity], slots[1 - parity])

    @pl.when(i == n_tiles)
    def _():
        run(None, slots[(n_tiles - 1) % 2])


def ffn_up(h, w_up, layer, conv_w, conv_b, seq, tm=1024, tn=256, ts=256):
    m, k = h.shape
    d_ff = w_up.shape[2] // 2
    tm = min(tm, seq)
    ts = min(ts, tm)
    nj = d_ff // tn
    n_tiles = m // tm
    return pl.pallas_call(
        functools.partial(_ffn_up_kernel, tiles_per_seq=seq // tm, n_tiles=n_tiles, tm=tm, ts=ts),
        grid=(nj, n_tiles + 1),
        in_specs=[
            pl.BlockSpec((tm, k), lambda j, i: (jnp.minimum(i, n_tiles - 1), 0)),
            _wspec(w_up, layer, tn, lambda j, i: j),
            _wspec(w_up, layer, tn, lambda j, i: nj + j),
            pl.BlockSpec((CONV_W, tn), lambda j, i: (0, j)),
            pl.BlockSpec((CONV_W, tn), lambda j, i: (0, nj + j)),
            pl.BlockSpec((1, tn), lambda j, i: (0, j)),
            pl.BlockSpec((1, tn), lambda j, i: (0, nj + j)),
        ],
        out_specs=pl.BlockSpec((tm, tn), lambda j, i: (jnp.maximum(i - 1, 0), j)),
        out_shape=jax.ShapeDtypeStruct((m, d_ff), BF16),
        scratch_shapes=[pltpu.VMEM((k, 2 * tn), BF16), pltpu.VMEM((tm, 2 * tn), F32), pltpu.VMEM((tm, 2 * tn), F32),
                        pltpu.VMEM((SUBLANES, tn), F32), pltpu.VMEM((SUBLANES, tn), F32)],
        compiler_params=_cparams(("parallel", "arbitrary")),
        name="ffn_up",
    )(h, w_up, w_up, conv_w, conv_w, conv_b, conv_b)


A_SUB = 16


def _hgrn_kernel(q_ref, f_ref, i_ref, g_ref, lb_ref, gain_ref, o_ref, st_ref, *, n_chunks, heads):
    @pl.when(pl.program_id(2) == 0)
    def _():
        st_ref[...] = jnp.zeros_like(st_ref)

    gain = gain_ref[...]
    tri = (lax.broadcasted_iota(I32, (CHUNK, CHUNK), 0) >= lax.broadcasted_iota(I32, (CHUNK, CHUNK), 1)).astype(BF16)
    rows_half = lax.broadcasted_iota(I32, (SUBLANES, A_DK), 0)
    n_sub = CHUNK // A_SUB

    def chunk(c, carry):
        rows = pl.ds(pl.multiple_of(c * CHUNK, CHUNK), CHUNK)
        hs = range(heads)
        cols = [slice(hh * A_DK, (hh + 1) * A_DK) for hh in hs]
        iv = [i_ref[rows, cl] for cl in cols]
        iv_b = [x.astype(BF16) for x in iv]
        st = [st_ref[hh] for hh in hs]
        qf, kin, b = [], [], []
        for hh in hs:
            lb = lb_ref[:, cols[hh]]
            z = f_ref[rows, cols[hh]]
            qv = q_ref[rows, cols[hh]]
            qf.append(qv * _sigmoid(qv))
            kin.append((1.0 - lb) * _sigmoid(-z))
            b.append(_dot_exact_lhs(tri, jnp.log(lb + (1.0 - lb) * _sigmoid(z))))
        o_inter = [_dot_nt((qf[hh] * jnp.exp(b[hh])).astype(BF16), st[hh].astype(BF16)) for hh in hs]
        outs = [[] for _ in hs]
        for si in range(n_sub):
            lo = si * A_SUB
            sub = slice(lo, lo + A_SUB)
            o_i = [o_inter[hh][sub] for hh in hs]
            if si > 0:
                att = []
                for hh in hs:
                    b_ref_row = b[hh][lo - 1:lo]
                    q_s = (qf[hh][sub] * jnp.exp(b[hh][sub] - b_ref_row)).astype(BF16)
                    k_s = (kin[hh][0:lo] * jnp.exp(b_ref_row - b[hh][0:lo])).astype(BF16)
                    att.append(_dot_nt(q_s, k_s))
                o_i = [o_i[hh] + _dot(att[hh].astype(BF16), iv_b[hh][0:lo]) for hh in hs]
            half = SUBLANES
            o_h = [[o_i[hh][0:half], o_i[hh][half:A_SUB]] for hh in hs]
            for s in range(A_SUB):
                for hh in hs:
                    b_s = b[hh][lo + s:lo + s + 1]
                    k_s = kin[hh][lo + s:lo + s + 1]
                    i_s = iv[hh][lo + s:lo + s + 1]
                    for part in range(s // half, A_SUB // half):
                        rs = slice(lo + part * half, lo + (part + 1) * half)
                        e = b[hh][rs] - b_s
                        if part == s // half:
                            e = jnp.where(rows_half >= s % half, e, -jnp.inf)
                        a = jnp.sum(qf[hh][rs] * jnp.exp(e) * k_s, axis=1, keepdims=True)
                        o_h[hh][part] = o_h[hh][part] + a * i_s
            for hh in hs:
                outs[hh].append(jnp.concatenate(o_h[hh], axis=0))
        for hh in hs:
            b_last = b[hh][CHUNK - 1:CHUNK]
            k_dec = (kin[hh] * jnp.exp(b_last - b[hh])).astype(BF16)
            st_ref[hh] = st[hh] * jnp.exp(b_last) + _dot_tn(iv_b[hh], k_dec)
        for hh in hs:
            o = jnp.concatenate(outs[hh], axis=0)
            gv = g_ref[rows, cols[hh]]
            ms = jnp.mean(o * o, axis=-1, keepdims=True)
            on = o * lax.rsqrt(ms + NORM_EPS) * gain
            o_ref[rows, cols[hh]] = (on * (gv * _sigmoid(gv))).astype(o_ref.dtype)
        return carry

    lax.fori_loop(0, n_chunks, chunk, 0)


def hgrn2(p3, lb, out_gain, s_blk=256, heads=4):
    bsz, seq, _ = p3.shape
    s_blk = min(s_blk, seq)
    width = heads * A_DK
    groups = A_HEADS // heads
    c0 = COL_A // width

    def col(part):
        return pl.BlockSpec((None, s_blk, width), lambda b, h, s, part=part: (b, s, c0 + part * groups + h))

    return pl.pallas_call(
        functools.partial(_hgrn_kernel, n_chunks=s_blk // CHUNK, heads=heads),
        grid=(bsz, groups, seq // s_blk),
        in_specs=[col(0), col(1), col(2), col(3),
                  pl.BlockSpec((1, width), lambda b, h, s: (0, h)),
                  pl.BlockSpec((1, A_DV), lambda b, h, s: (0, 0))],
        out_specs=pl.BlockSpec((None, s_blk, width), lambda b, h, s: (b, s, h)),
        out_shape=jax.ShapeDtypeStruct((bsz, seq, A_W), BF16),
        scratch_shapes=[pltpu.VMEM((heads, A_DV, A_DK), F32)],
        compiler_params=_cparams(("parallel", "parallel", "arbitrary")),
        name="hgrn2",
    )(p3, p3, p3, p3, lb.reshape(1, A_HEADS * A_DK), out_gain.reshape(1, A_DV))


def _rwkv_prep_kernel(r_ref, k_ref, v_ref, l_ref, rp_ref, kp_ref, vp_ref, lp_ref,
                      mu_ref, w0_ref, w2_ref, a0_ref, a2_ref, g2_ref, kk_ref, ka_ref,
                      ro_ref, ld_ref, k2_ref, vo_ref, kko_ref, kka_ref, go_ref, *, tiles_per_seq, tm):
    first = (pl.program_id(0) % tiles_per_seq) == 0
    rows = lax.broadcasted_iota(I32, (tm, B_W), 0)

    def shifted(cur_ref, prev_ref, part):
        cur = cur_ref[...]
        prev = jnp.where(first, 0.0, prev_ref[...])[SUBLANES - 1:SUBLANES, :]
        sh = jnp.where(rows == 0, prev, pltpu.roll(cur, 1, axis=0))
        return cur + (sh - cur) * mu_ref[part:part + 1, :]

    r = shifted(r_ref, rp_ref, 0)
    k = shifted(k_ref, kp_ref, 1)
    v = shifted(v_ref, vp_ref, 2)
    lo = shifted(l_ref, lp_ref, 3)
    lo_a = lo[:, 0:LANES]
    lo_g = lo[:, LANES:3 * LANES]
    wpre = w0_ref[...] + _dot(jnp.tanh(lo_a).astype(BF16), w2_ref[...])
    y = -wpre
    softplus = jnp.maximum(y, 0.0) + jnp.log(1.0 + jnp.exp(-jnp.abs(y)))
    w_log = -softplus - 0.5
    ld_ref[...] = -jnp.exp(w_log)
    a = _sigmoid(a0_ref[...] + _dot(lo_a.astype(BF16), a2_ref[...]))
    go_ref[...] = _dot(_sigmoid(lo_g).astype(BF16), g2_ref[...]).astype(go_ref.dtype)
    kk = k * kk_ref[...]
    bd = (lax.broadcasted_iota(I32, (LANES, LANES), 0) // B_HD
          == lax.broadcasted_iota(I32, (LANES, LANES), 1) // B_HD).astype(BF16)
    sq = kk * kk
    ss = jnp.concatenate([_dot_exact_rhs(sq[:, j * LANES:(j + 1) * LANES], bd) for j in range(B_W // LANES)], axis=1)
    kk = kk / jnp.maximum(jnp.sqrt(ss), 1e-12)
    ro_ref[...] = r.astype(ro_ref.dtype)
    vo_ref[...] = v.astype(vo_ref.dtype)
    kko_ref[...] = kk.astype(kko_ref.dtype)
    kka_ref[...] = (kk * a).astype(kka_ref.dtype)
    k2_ref[...] = (k * (1.0 + (a - 1.0) * ka_ref[...])).astype(k2_ref.dtype)


def rwkv_prep(p, seq, mu4, w0, w2p, a0, a2p, g2p, k_k, k_a, tm=256):
    m = p.shape[0]
    tm = min(tm, seq)
    cb = COL_B_RKV // B_W
    pb = tm // SUBLANES

    def cur(j):
        return pl.BlockSpec((tm, B_W), lambda i, j=j: (i, cb + j))

    def prev(j):
        return pl.BlockSpec((SUBLANES, B_W), lambda i, j=j: (jnp.maximum(i * pb - 1, 0), cb + j))

    def full(a):
        return pl.BlockSpec(a.shape, lambda i: (0, 0))

    row = lambda a: a.reshape(1, B_W)
    params = [mu4, row(w0), w2p, row(a0), a2p, g2p, row(k_k), row(k_a)]
    out = jax.ShapeDtypeStruct((m, B_W), F32)
    out_b = jax.ShapeDtypeStruct((m, B_W), BF16)
    return pl.pallas_call(
        functools.partial(_rwkv_prep_kernel, tiles_per_seq=seq // tm, tm=tm),
        grid=(m // tm,),
        in_specs=[cur(0), cur(1), cur(2), cur(3), prev(0), prev(1), prev(2), prev(3)] + [full(a) for a in params],
        out_specs=[pl.BlockSpec((tm, B_W), lambda i: (i, 0))] * 7,
        out_shape=[out_b, out, out_b, out_b, out_b, out_b, out_b],
        compiler_params=_cparams(("parallel",)),
        name="rwkv_prep",
    )(p, p, p, p, p, p, p, p, *params)


B_T = 64


def _rwkv_kernel(r_ref, ld_ref, k_ref, v_ref, kk_ref, kka_ref, g_ref, rk_ref, lnw_ref, lnb_ref, o_ref, st_ref, *,
                 n_chunks, pairs):
    @pl.when(pl.program_id(2) == 0)
    def _():
        st_ref[...] = jnp.zeros_like(st_ref)

    t = B_T
    ii = lax.broadcasted_iota(I32, (t, t), 0)
    jj = lax.broadcasted_iota(I32, (t, t), 1)
    tri = (ii >= jj).astype(BF16)
    i2 = lax.broadcasted_iota(I32, (2 * t, 2 * t), 0)
    j2 = lax.broadcasted_iota(I32, (2 * t, 2 * t), 1)
    same = (i2 // t) == (j2 // t)
    strict_bd = same & ((i2 % t) > (j2 % t))
    incl_bd = same & ((i2 % t) >= (j2 % t))
    head_bd = same.astype(BF16)
    lane = lax.broadcasted_iota(I32, (t, LANES), 1)
    h0 = lane < B_HD

    def stack(x):
        return jnp.concatenate([jnp.where(h0, x, jnp.zeros_like(x)), jnp.where(h0, jnp.zeros_like(x), x)], axis=0)

    def head_sum(x):
        hi, lo = _split2(x)
        return _dot(hi, head_bd) + _dot(lo, head_bd)

    def chunk(c, carry):
        sl = pl.ds(pl.multiple_of(c * t, t), t)
        prs = range(pairs)
        cols = [slice(pp * LANES, (pp + 1) * LANES) for pp in prs]
        r = [r_ref[sl, cl].astype(F32) for cl in cols]
        ld = [ld_ref[sl, cl] for cl in cols]
        k = [k_ref[sl, cl].astype(F32) for cl in cols]
        v = [v_ref[sl, cl].astype(F32) for cl in cols]
        cs = []
        for pp in prs:
            ld_hi, ld_lo = _split2(ld[pp])
            cs.append(_dot(tri, ld_hi) + _dot(tri, ld_lo))
        lhs, rhs, vs = [], [], []
        for pp in prs:
            e_neg = jnp.exp(-cs[pp])
            kka = kka_ref[sl, cols[pp]].astype(F32)
            a_t = (-kk_ref[sl, cols[pp]].astype(F32) * jnp.exp(cs[pp] - ld[pp])).astype(BF16)
            r_t = (r[pp] * jnp.exp(cs[pp])).astype(BF16)
            b_t = (kka * e_neg).astype(BF16)
            k_t = (k[pp] * e_neg).astype(BF16)
            lhs.append(jnp.concatenate([stack(a_t), stack(r_t)], axis=0))
            rhs.append(jnp.concatenate([b_t, b_t, k_t, k_t], axis=0))
            vs.append(stack(v[pp].astype(BF16)))
        sc = [_dot_nt(lhs[pp], rhs[pp]) for pp in prs]
        st = [st_ref[pp] for pp in prs]
        proj = [_dot_nt(lhs[pp], st[pp].astype(BF16)) for pp in prs]
        n = [jnp.where(strict_bd, sc[pp][0:2 * t, 0:2 * t], 0.0).astype(BF16) for pp in prs]
        xs = [proj[pp][0:2 * t] + _dot(jnp.where(strict_bd, sc[pp][0:2 * t, 2 * t:4 * t], 0.0).astype(BF16), vs[pp])
              for pp in prs]
        for it in range(6):
            xs = [xs[pp] + _dot(n[pp], xs[pp].astype(BF16)) for pp in prs]
            if it < 5:
                n = [_dot(n[pp], n[pp]).astype(BF16) for pp in prs]
        os_ = []
        for pp in prs:
            m_r = jnp.concatenate([jnp.where(incl_bd, sc[pp][2 * t:4 * t, 0:2 * t], 0.0),
                                   jnp.where(incl_bd, sc[pp][2 * t:4 * t, 2 * t:4 * t], 0.0)], axis=1)
            uv = jnp.concatenate([xs[pp].astype(BF16), vs[pp]], axis=0)
            os_.append(proj[pp][2 * t:4 * t] + _dot(m_r.astype(BF16), uv))
        upd = []
        for pp in prs:
            u = xs[pp][0:t] + xs[pp][t:2 * t]
            c_last = cs[pp][t - 1:t]
            dec = jnp.exp(c_last - cs[pp])
            upd.append(_dot_tn(jnp.concatenate([u, v[pp]], axis=0).astype(BF16),
                               jnp.concatenate([kka_ref[sl, cols[pp]].astype(F32) * dec, k[pp] * dec], axis=0).astype(BF16)))
        for pp in prs:
            st_ref[pp] = st[pp] * jnp.exp(cs[pp][t - 1:t]) + jnp.where(same, upd[pp], 0.0)
        inv = 1.0 / B_HD
        o = [os_[pp][0:t] + os_[pp][t:2 * t] for pp in prs]
        mean = [head_sum(o[pp]) * inv for pp in prs]
        d = [o[pp] - mean[pp] for pp in prs]
        var = [head_sum(d[pp] * d[pp]) * inv for pp in prs]
        bonus = [head_sum(r[pp] * k[pp] * rk_ref[:, cols[pp]]) for pp in prs]
        for pp in prs:
            on = d[pp] * lax.rsqrt(var[pp] + B_GN_EPS) * lnw_ref[:, cols[pp]] + lnb_ref[:, cols[pp]]
            o_ref[sl, cols[pp]] = ((on + bonus[pp] * v[pp]) * g_ref[sl, cols[pp]].astype(F32)).astype(o_ref.dtype)
        return carry

    lax.fori_loop(0, n_chunks, chunk, 0)


def rwkv_recurrence(r, ld, k2, v, kk, kka, g, r_k, ln_w, ln_b, bsz, seq, s_blk=256, pairs=8):
    s_blk = min(s_blk, seq)
    width = pairs * LANES
    groups = B_W // width
    args = [a.reshape(bsz, seq, B_W) for a in (r, ld, k2, v, kk, kka, g)]
    blk = pl.BlockSpec((None, s_blk, width), lambda b, h, s: (b, s, h))
    par = pl.BlockSpec((1, width), lambda b, h, s: (0, h))
    return pl.pallas_call(
        functools.partial(_rwkv_kernel, n_chunks=s_blk // B_T, pairs=pairs),
        grid=(bsz, groups, seq // s_blk),
        in_specs=[blk] * 7 + [par] * 3,
        out_specs=blk,
        out_shape=jax.ShapeDtypeStruct((bsz, seq, B_W), BF16),
        scratch_shapes=[pltpu.VMEM((pairs, LANES, LANES), F32)],
        compiler_params=_cparams(("parallel", "parallel", "arbitrary")),
        name="rwkv_recurrence",
    )(*args, r_k.reshape(1, B_W), ln_w.reshape(1, B_W), ln_b.reshape(1, B_W))


def _rope(x, cos_t, sin_t, lane_in_head, half):
    partner = jnp.where(lane_in_head < half, pltpu.roll(x, LANES - half, axis=1), pltpu.roll(x, half, axis=1))
    return x * cos_t + partner * sin_t


def _attn_prep_kernel(q_ref, k_ref, v_ref, qi_ref, kw_ref, ca_ref, sa_ref, ci_ref, si_ref, qg_ref, kg_ref,
                      qo_ref, ko_ref, vo_ref, qio_ref, kio_ref, wo_ref, *, tm):
    ca, sa, ci, si = ca_ref[...], sa_ref[...], ci_ref[...], si_ref[...]
    lane = lax.broadcasted_iota(I32, (tm, LANES), 1)
    lane_i = lane % IDX_HD
    half_a = C_HD // ROPE_FRACTION // 2
    half_i = IDX_HD // ROPE_FRACTION // 2
    scale = C_HD ** -0.5 * LOG2_E

    def norm_rope(x, gain):
        ms = jnp.mean(x * x, axis=-1, keepdims=True)
        return _rope(x * lax.rsqrt(ms + NORM_EPS) * gain, ca, sa, lane, half_a)

    for h in range(C_HEADS):
        sl = slice(h * C_HD, (h + 1) * C_HD)
        qo_ref[:, sl] = (norm_rope(q_ref[:, sl], qg_ref[...]) * scale).astype(qo_ref.dtype)
    for h in range(C_KV_HEADS):
        sl = slice(h * C_HD, (h + 1) * C_HD)
        ko_ref[:, sl] = norm_rope(k_ref[:, sl], kg_ref[...]).astype(ko_ref.dtype)
    vo_ref[...] = v_ref[...].astype(vo_ref.dtype)
    for j in range(IDX_HEADS * IDX_HD // LANES):
        sl = slice(j * LANES, (j + 1) * LANES)
        qio_ref[:, sl] = _rope(qi_ref[:, sl], ci, si, lane_i, half_i).astype(qio_ref.dtype)
    kw = kw_ref[:, 0:LANES]
    kr = _rope(kw, ci, si, lane_i, half_i)
    kio_ref[...] = jnp.where(lane < IDX_HD, kr, pltpu.roll(kr, IDX_HD, axis=1)).astype(kio_ref.dtype)
    w = pltpu.roll(kw, LANES - IDX_HD, axis=1) * (IDX_HEADS ** -0.5 * IDX_HD ** -0.5)
    wo_ref[...] = jnp.where(lane < IDX_HEADS, w, 0.0)


def attn_prep(p, tabs, q_gain, k_gain, tm=256):
    m = p.shape[0]
    tm = min(tm, m)

    def colblk(width, off):
        return pl.BlockSpec((tm, width), lambda i: (i, off // width))

    tab = pl.BlockSpec((tm, LANES), lambda i: (i, 0))
    gain = pl.BlockSpec((1, C_HD), lambda i: (0, 0))
    kw_width = 512

    def out(width, dtype):
        return jax.ShapeDtypeStruct((m, width), dtype), pl.BlockSpec((tm, width), lambda i: (i, 0))

    outs = [out(C_W, BF16), out(C_KV_W, BF16), out(C_KV_W, BF16), out(IDX_HEADS * IDX_HD, BF16),
            out(LANES, BF16), out(LANES, F32)]
    return pl.pallas_call(
        functools.partial(_attn_prep_kernel, tm=tm),
        grid=(m // tm,),
        in_specs=[colblk(C_W, COL_C_Q), colblk(C_KV_W, COL_C_K), colblk(C_KV_W, COL_C_V),
                  colblk(IDX_HEADS * IDX_HD, COL_C_QI), colblk(kw_width, COL_C_KW), tab, tab, tab, tab, gain, gain],
        out_specs=[o[1] for o in outs],
        out_shape=[o[0] for o in outs],
        compiler_params=_cparams(("parallel",)),
        name="attn_prep",
    )(p, p, p, p, p, *tabs, q_gain.reshape(1, C_HD), k_gain.reshape(1, C_HD))


V_ONES = 16


def _attn_kernel(q_ref, k_ref, vt_ref, qi_ref, ki_ref, w_ref, o_ref, key_sc, bias_sc, m_sc, acc_sc, *, tq, tk, topk):
    qb = pl.program_id(1)
    nkb = ((qb + 1) * tq + tk - 1) // tk
    lane = lax.broadcasted_iota(I32, (tq, LANES), 1)
    w_t = w_ref[...].T
    q_chunk = lax.shift_right_logical(qb * tq + lax.broadcasted_iota(I32, (1, tq), 1), CHUNK_SHIFT)
    key_chunk_in_blk = lax.shift_right_logical(lax.broadcasted_iota(I32, (tk, tq), 0), CHUNK_SHIFT)
    n_pairs = IDX_HEADS * IDX_HD // LANES
    fold = 64

    q_pairs = []
    for hp in range(n_pairs):
        qp = qi_ref[:, hp * LANES:(hp + 1) * LANES]
        zero = jnp.zeros_like(qp)
        q_pairs.append(jnp.concatenate([jnp.where(lane < IDX_HD, qp, zero), jnp.where(lane < IDX_HD, zero, qp)], axis=0))

    def score_block(kb, carry):
        c0 = pl.multiple_of(kb * tk, tk)
        ki2 = ki_ref[pl.ds(c0, tk), :]
        sc = jnp.zeros((tk, tq), F32)
        for hp in range(n_pairs):
            rel = jnp.maximum(_dot_nt(ki2, q_pairs[hp]), 0.0)
            sc = sc + w_t[2 * hp:2 * hp + 1, :] * rel[:, 0:tq] + w_t[2 * hp + 1:2 * hp + 2, :] * rel[:, tq:2 * tq]
        sc = jnp.where(sc == 0.0, 0.0, sc)
        bits = lax.bitcast_convert_type(sc, I32)
        skey = bits ^ ((bits >> 31) & 0x7FFFFFFF)
        allowed = key_chunk_in_blk <= q_chunk - kb * (tk // CHUNK)
        key_sc[kb] = jnp.where(allowed, skey, INT_MIN)
        return carry

    lax.fori_loop(0, nkb, score_block, 0)

    def count_ge(cand):
        def body(kb, acc):
            hit = jnp.where(key_sc[kb] >= cand, 1.0, 0.0)
            for j in range(tk // fold):
                acc = acc + hit[j * fold:(j + 1) * fold]
            return acc
        acc = lax.fori_loop(0, nkb, body, jnp.zeros((fold, tq), F32))
        return jnp.sum(acc, axis=0, keepdims=True)

    kf = float(topk)
    thr = jnp.where(count_ge(jnp.zeros((1, tq), I32)) >= kf, 0, INT_MIN).astype(I32)

    def bit_step(i, thr):
        cand = thr | (jnp.int32(1) << (30 - i))
        return jnp.where(count_ge(cand) >= kf, cand, thr)

    thr = lax.fori_loop(0, 31, bit_step, thr)
    thr = jnp.maximum(thr, INT_MIN + 1)
    need = kf - count_ge(thr + 1)

    tri = (lax.broadcasted_iota(I32, (tk, tk), 0) >= lax.broadcasted_iota(I32, (tk, tk), 1)).astype(BF16)

    def bias_block(kb, seen):
        keys = key_sc[kb]
        eq = keys == thr
        eq_f = jnp.where(eq, 1.0, 0.0)
        rank = _dot(tri, eq_f.astype(BF16)) + seen
        take = (keys > thr) | (eq & (rank <= need))
        bias_sc[kb] = jnp.where(take, 0.0, NEG_BIG)
        return rank[tk - 1:tk]

    lax.fori_loop(0, nkb, bias_block, jnp.zeros((1, tq), F32))

    rep = C_HEADS // C_KV_HEADS
    ve = C_HD + V_ONES
    qs = [jnp.concatenate([q_ref[:, (g * rep + r) * C_HD:(g * rep + r + 1) * C_HD] for r in range(rep)], axis=0)
          for g in range(C_KV_HEADS)]
    m_sc[...] = jnp.full(m_sc.shape, NEG_BIG, F32)
    acc_sc[...] = jnp.zeros(acc_sc.shape, F32)

    def body(kb, carry):
        c0 = pl.multiple_of(kb * tk, tk)
        bias = bias_sc[kb]
        s_all = [_dot_nt(k_ref[pl.ds(c0, tk), g * C_HD:(g + 1) * C_HD], qs[g]) for g in range(C_KV_HEADS)]
        for g in range(C_KV_HEADS):
            s = jnp.concatenate([s_all[g][:, r * tq:(r + 1) * tq] + bias for r in range(rep)], axis=1)
            m_prev = m_sc[g]
            m_new = jnp.maximum(m_prev, jnp.max(s, axis=0, keepdims=True))
            p = jnp.exp2(s - m_new)
            acc_sc[g] = jnp.exp2(m_prev - m_new) * acc_sc[g] + _dot(vt_ref[kb, g * ve:(g + 1) * ve, :], p.astype(BF16))
            m_sc[g] = m_new
        return carry

    lax.fori_loop(0, nkb, body, 0)
    for g in range(C_KV_HEADS):
        acc = acc_sc[g]
        out_t = acc[0:C_HD] / acc[C_HD:C_HD + 1]
        for r in range(rep):
            h = g * rep + r
            o_ref[:, h * C_HD:(h + 1) * C_HD] = out_t[:, r * tq:(r + 1) * tq].T.astype(o_ref.dtype)


def sparse_attention(qn, kn, vn, qi, ki2, w, bsz, seq, tq=128, tk=512):
    tk = min(tk, seq)
    topk = min(TOPK_MAX, seq // 4)
    nkb = seq // tk
    rep = C_HEADS // C_KV_HEADS
    ve = C_HD + V_ONES
    r3 = lambda a: a.reshape(bsz, seq, a.shape[-1])
    v_t = vn.reshape(bsz, nkb, tk, C_KV_HEADS, C_HD).transpose(0, 1, 3, 4, 2)
    v_t = jnp.concatenate([v_t, jnp.ones((bsz, nkb, C_KV_HEADS, V_ONES, tk), v_t.dtype)], axis=3)
    v_t = v_t.reshape(bsz, nkb, C_KV_HEADS * ve, tk)
    qblk = lambda width: pl.BlockSpec((None, tq, width), lambda b, i: (b, i, 0))
    sblk = lambda width: pl.BlockSpec((None, seq, width), lambda b, i: (b, 0, 0))
    return pl.pallas_call(
        functools.partial(_attn_kernel, tq=tq, tk=tk, topk=topk),
        grid=(bsz, seq // tq),
        in_specs=[qblk(C_W), sblk(C_KV_W), pl.BlockSpec((None, nkb, C_KV_HEADS * ve, tk), lambda b, i: (b, 0, 0, 0)),
                  qblk(IDX_HEADS * IDX_HD), sblk(LANES), qblk(LANES)],
        out_specs=qblk(C_W),
        out_shape=jax.ShapeDtypeStruct((bsz, seq, C_W), BF16),
        scratch_shapes=[pltpu.VMEM((nkb, tk, tq), I32), pltpu.VMEM((nkb, tk, tq), F32),
                        pltpu.VMEM((C_KV_HEADS, 1, rep * tq), F32), pltpu.VMEM((C_KV_HEADS, ve, rep * tq), F32)],
        compiler_params=_cparams(("parallel", "arbitrary")),
        name="sparse_attention",
    )(r3(qn), r3(kn), v_t, r3(qi), r3(ki2), r3(w))


def _rope_tables(positions):
    pos = positions.astype(F32).reshape(-1, 1)

    def tables(head_dim):
        rot = head_dim // ROPE_FRACTION
        half = rot // 2
        inv_freq = ROPE_THETA ** (-jnp.arange(0, rot, 2, dtype=F32) / rot)
        ang = pos * inv_freq
        cos, sin = jnp.cos(ang), jnp.sin(ang)
        ones = jnp.ones((pos.shape[0], head_dim - rot), F32)
        c = jnp.concatenate([cos, cos, ones], axis=1)
        s = jnp.concatenate([-sin, sin, 0.0 * ones], axis=1)
        reps = LANES // head_dim
        return jnp.tile(c, (1, reps)), jnp.tile(s, (1, reps))

    ca, sa = tables(C_HD)
    ci, si = tables(IDX_HD)
    return ca, sa, ci, si


def _pad_rows(a, before, total):
    return jnp.zeros((total, a.shape[1]), a.dtype).at[before:before + a.shape[0]].set(a)


def kernel(x, positions, norm_mix, w_in, hgrn_lb_logits, hgrn_out_norm, rwkv_mu, rwkv_w0, rwkv_w2, rwkv_a0, rwkv_a2,
           rwkv_g2, rwkv_k_k, rwkv_k_a, rwkv_r_k, rwkv_ln_w, rwkv_ln_b, q_norm, k_norm, w_branch_a, w_branch_b,
           w_branch_c, w_out, norm_ffn, w_up, conv_w, conv_b, w_down):
    bsz, seq, d_model = x.shape
    depth = w_in.shape[0]
    m = bsz * seq
    tabs = _rope_tables(positions)
    lb_all = jnp.cumsum(jax.nn.softmax(hgrn_lb_logits.astype(F32), axis=0), axis=0)
    lb_all = lb_all - lb_all[0:1]
    n_lora = B_LORA_DECAY + B_LORA_A + B_LORA_G
    ab_cols = 4 * A_W + 3 * B_W + n_lora
    c_cols = C_W + 2 * C_KV_W + IDX_HEADS * IDX_HD + IDX_HD + IDX_HEADS
    w_in_t = jnp.swapaxes(w_in, 1, 2)
    w_in_ab = pack_cols(w_in_t, 0, ab_cols, W_AB)
    w_in_c = pack_cols(w_in_t, ab_cols, c_cols, W_C)
    w_in_g = pack_cols(w_in_t, ab_cols + c_cols, 3 * d_model, 3 * d_model)
    w_a, w_b, w_c = w_branch_a.astype(BF16), w_branch_b.astype(BF16), w_branch_c.astype(BF16)
    w_o, w_d = w_out.astype(BF16), w_down.astype(BF16)

    xf = x.reshape(m, d_model)
    for l in range(depth):
        h = rmsnorm(xf, norm_mix[l])
        p = proj_in(h, w_in_ab, l, F32)
        p_c = proj_in(h, w_in_c, l, F32)
        p_g = proj_in(h, w_in_g, l, BF16)

        o_a = hgrn2(p.reshape(bsz, seq, -1), lb_all[l], hgrn_out_norm[l]).reshape(m, A_W)

        mu = rwkv_mu[l]
        mu4 = jnp.stack([mu[0:B_W], mu[B_W:2 * B_W], mu[2 * B_W:3 * B_W],
                         jnp.pad(mu[3 * B_W:], (0, B_W - n_lora))])
        w2p = _pad_rows(rwkv_w2[l], 0, LANES).astype(BF16)
        a2p = _pad_rows(rwkv_a2[l], B_LORA_DECAY, LANES).astype(BF16)
        g2p = _pad_rows(rwkv_g2[l], 0, 2 * LANES).astype(BF16)
        parts = rwkv_prep(p, seq, mu4, rwkv_w0[l], w2p, rwkv_a0[l], a2p, g2p, rwkv_k_k[l], rwkv_k_a[l])
        o_b = rwkv_recurrence(*parts, rwkv_r_k[l], rwkv_ln_w[l], rwkv_ln_b[l], bsz, seq).reshape(m, B_W)

        qn, kn, vn, qi, ki2, wi = attn_prep(p_c, tabs, q_norm[l], k_norm[l])
        o_c = sparse_attention(qn, kn, vn, qi, ki2, wi, bsz, seq).reshape(m, C_W)

        merged = merge_branches(o_a, o_b, o_c, w_a, w_b, w_c, l, p_g, d_model)
        xf = matmul_residual(merged, w_o, l, xf, tm=1024, tn=512)

        h2 = rmsnorm(xf, norm_ffn[l])
        act = ffn_up(h2, w_up, l, conv_w[l], conv_b[l].reshape(1, -1), seq)
        xf = matmul_residual(act, w_d, l, xf, tm=512, tn=512)
    return xf.reshape(bsz, seq, d_model)
```

```python
import functools

import jax
import jax.numpy as jnp
import numpy as np
from jax import lax
from jax.experimental import pallas as pl
from jax.experimental.pallas import tpu as pltpu

F32 = jnp.float32
BF16 = jnp.bfloat16
I32 = jnp.int32

CHUNK = 64
CHUNK_SHIFT = 6
ROPE_THETA = 500000.0
ROPE_FRACTION = 4
NORM_EPS = 1e-6
A_HEADS, A_DK, A_DV = 8, 128, 128
A_W = A_HEADS * A_DV
B_HEADS, B_HD = 16, 64
B_W = B_HEADS * B_HD
B_LORA_DECAY, B_LORA_A, B_LORA_G = 64, 64, 160
B_GN_EPS = 64e-5
C_HEADS, C_KV_HEADS, C_HD = 16, 4, 128
C_W = C_HEADS * C_HD
C_KV_W = C_KV_HEADS * C_HD
IDX_HEADS, IDX_HD = 16, 64
TOPK_MAX = 256
CONV_W = 3

LANES = 128
SUBLANES = 8
VMEM_LIMIT = 56 * 1024 * 1024

COL_A = 0
COL_B_RKV = 4096
COL_B_LORA = 7168
W_AB = 8192
COL_C_Q = 0
COL_C_K = 2048
COL_C_V = 2560
COL_C_QI = 3072
COL_C_KW = 4096
W_C = 4608
LOG2_E = 1.4426950408889634
NEG_BIG = -1e30
INT_MIN = -2147483648


def _sigmoid(x):
    return 1.0 / (1.0 + jnp.exp(-x))


def _dot(a, b):
    return jnp.dot(a, b, preferred_element_type=F32)


def _dot_nt(a, b):
    return lax.dot_general(a, b, (((1,), (1,)), ((), ())), preferred_element_type=F32)


def _dot_tn(a, b):
    return lax.dot_general(a, b, (((0,), (0,)), ((), ())), preferred_element_type=F32)


def _split3(x):
    hi = x.astype(BF16)
    r1 = x - hi.astype(F32)
    mid = r1.astype(BF16)
    lo = (r1 - mid.astype(F32)).astype(BF16)
    return hi, mid, lo


def _dot_exact_lhs(m_bf16, x):
    hi, mid, lo = _split3(x)
    return _dot(m_bf16, hi) + _dot(m_bf16, mid) + _dot(m_bf16, lo)


def _dot_exact_rhs(x, m_bf16):
    hi, mid, lo = _split3(x)
    return _dot(hi, m_bf16) + _dot(mid, m_bf16) + _dot(lo, m_bf16)


def _split2(x):
    hi = x.astype(BF16)
    lo = (x - hi.astype(F32)).astype(BF16)
    return hi, lo


def _dot_hp(a, b):
    ah, al = _split2(a)
    bh, bl = _split2(b)
    return _dot(ah, bh) + _dot(ah, bl) + _dot(al, bh)


def _dot_nt_hp(a, b):
    ah, al = _split2(a)
    bh, bl = _split2(b)
    return _dot_nt(ah, bh) + _dot_nt(ah, bl) + _dot_nt(al, bh)


def _dot_tn_hp(a, b):
    ah, al = _split2(a)
    bh, bl = _split2(b)
    return _dot_tn(ah, bh) + _dot_tn(ah, bl) + _dot_tn(al, bh)


def _cparams(sem):
    return pltpu.CompilerParams(dimension_semantics=sem, vmem_limit_bytes=VMEM_LIMIT)


def _rmsnorm_kernel(x_ref, g_ref, o_ref):
    x = x_ref[...]
    ms = jnp.mean(x * x, axis=-1, keepdims=True)
    o_ref[...] = (x * lax.rsqrt(ms + NORM_EPS) * g_ref[...]).astype(o_ref.dtype)


def rmsnorm(x, gain, tm=256):
    m, d = x.shape
    return pl.pallas_call(
        _rmsnorm_kernel,
        grid=(m // tm,),
        in_specs=[pl.BlockSpec((tm, d), lambda i: (i, 0)), pl.BlockSpec((1, d), lambda i: (0, 0))],
        out_specs=pl.BlockSpec((tm, d), lambda i: (i, 0)),
        out_shape=jax.ShapeDtypeStruct((m, d), BF16),
        compiler_params=_cparams(("parallel",)),
        name="rmsnorm",
    )(x, gain.reshape(1, d))


def _wspec(w, layer, tn, jmap):
    return pl.BlockSpec((None, w.shape[1], tn), lambda *g: (layer, 0, jmap(*g)))


def _proj_in_kernel(h_ref, w_ref, o_ref):
    o_ref[...] = _dot(h_ref[...], w_ref[...]).astype(o_ref.dtype)


def proj_in(h, w, layer, out_dtype, tm=1024, tn=512):
    m, k = h.shape
    n = w.shape[2]
    tm = min(tm, m)
    return pl.pallas_call(
        _proj_in_kernel,
        grid=(m // tm, n // tn),
        in_specs=[pl.BlockSpec((tm, k), lambda i, j: (i, 0)), _wspec(w, layer, tn, lambda i, j: j)],
        out_specs=pl.BlockSpec((tm, tn), lambda i, j: (i, j)),
        out_shape=jax.ShapeDtypeStruct((m, n), out_dtype),
        compiler_params=_cparams(("parallel", "arbitrary")),
        name="proj_in",
    )(h, w)


def _pack_cols_kernel(*refs, shift, width):
    a_ref, o_ref = refs[0], refs[-1]
    t = pl.program_id(1)
    a = a_ref[...]
    if shift:
        a = jnp.concatenate([a[shift:], refs[1][0:shift]], axis=0)
    col = t * LANES + lax.broadcasted_iota(I32, a.shape, 0)
    o_ref[...] = jnp.where(col < width, a, 0.0).T.astype(o_ref.dtype)


def pack_cols(w_t, src_start, width, dst_width):
    depth, n_src, k = w_t.shape
    q0, shift = divmod(src_start, LANES)
    assert shift % SUBLANES == 0
    last = (n_src - 1) // LANES

    def src(extra):
        return pl.BlockSpec((None, LANES, k), lambda l, t: (l, jnp.minimum(q0 + t + extra, last), 0))

    w = w_t
    srcs = [src(0), src(1)] if shift else [src(0)]
    return pl.pallas_call(
        functools.partial(_pack_cols_kernel, shift=shift, width=width),
        grid=(depth, dst_width // LANES),
        in_specs=srcs,
        out_specs=pl.BlockSpec((None, k, LANES), lambda l, t: (l, 0, t)),
        out_shape=jax.ShapeDtypeStruct((depth, k, dst_width), BF16),
        compiler_params=_cparams(("parallel", "arbitrary")),
        name="pack_cols",
    )(*([w] * len(srcs)))


def _mm_res_kernel(a_ref, b_ref, r_ref, o_ref):
    o_ref[...] = r_ref[...] + _dot(a_ref[...], b_ref[...])


def matmul_residual(a, b, layer, res, tm=512, tn=512):
    m, k = a.shape
    n = b.shape[2]
    tm = min(tm, m)
    tn = min(tn, n)
    return pl.pallas_call(
        _mm_res_kernel,
        grid=(m // tm, n // tn),
        in_specs=[
            pl.BlockSpec((tm, k), lambda i, j: (i, 0)),
            _wspec(b, layer, tn, lambda i, j: j),
            pl.BlockSpec((tm, tn), lambda i, j: (i, j)),
        ],
        out_specs=pl.BlockSpec((tm, tn), lambda i, j: (i, j)),
        out_shape=jax.ShapeDtypeStruct((m, n), F32),
        compiler_params=_cparams(("parallel", "arbitrary")),
        name="matmul_residual",
    )(a, b, res)


def _merge_kernel(oa_ref, ob_ref, oc_ref, wa_ref, wb_ref, wc_ref, ga_ref, gb_ref, gc_ref, o_ref):
    acc = _sigmoid(ga_ref[...].astype(F32)) * _dot(oa_ref[...], wa_ref[...])
    acc += _sigmoid(gb_ref[...].astype(F32)) * _dot(ob_ref[...], wb_ref[...])
    acc += _sigmoid(gc_ref[...].astype(F32)) * _dot(oc_ref[...], wc_ref[...])
    o_ref[...] = acc.astype(o_ref.dtype)


def merge_branches(o_a, o_b, o_c, w_a, w_b, w_c, layer, p, d_model, tm=1024, tn=512):
    m = o_a.shape[0]
    tm = min(tm, m)
    g0 = 0
    gstep = d_model // tn
    return pl.pallas_call(
        _merge_kernel,
        grid=(m // tm, d_model // tn),
        in_specs=[
            pl.BlockSpec((tm, o_a.shape[1]), lambda i, j: (i, 0)),
            pl.BlockSpec((tm, o_b.shape[1]), lambda i, j: (i, 0)),
            pl.BlockSpec((tm, o_c.shape[1]), lambda i, j: (i, 0)),
            _wspec(w_a, layer, tn, lambda i, j: j),
            _wspec(w_b, layer, tn, lambda i, j: j),
            _wspec(w_c, layer, tn, lambda i, j: j),
            pl.BlockSpec((tm, tn), lambda i, j: (i, g0 + j)),
            pl.BlockSpec((tm, tn), lambda i, j: (i, g0 + gstep + j)),
            pl.BlockSpec((tm, tn), lambda i, j: (i, g0 + 2 * gstep + j)),
        ],
        out_specs=pl.BlockSpec((tm, tn), lambda i, j: (i, j)),
        out_shape=jax.ShapeDtypeStruct((m, d_model), BF16),
        compiler_params=_cparams(("parallel", "arbitrary")),
        name="merge_branches",
    )(o_a, o_b, o_c, w_a, w_b, w_c, p, p, p)


FFN_K_SPLIT = 8


def _ffn_up_kernel(h_ref, wg_ref, wu_ref, cwg_ref, cwu_ref, cbg_ref, cbu_ref, o_ref, w_sc, u0_sc, u1_sc, cg_ref,
                   cu_ref, *, tiles_per_seq, n_tiles, tm, ts):
    i = pl.program_id(1)
    tn = o_ref.shape[1]
    kdim = h_ref.shape[1]
    n_sub = tm // ts
    tk = kdim // FFN_K_SPLIT
    te = ts // FFN_K_SPLIT
    rows = lax.broadcasted_iota(I32, (te, tn), 0)
    slots = (u0_sc, u1_sc)

    @pl.when(i == 0)
    def _():
        w_sc[:, 0:tn] = wg_ref[...].astype(BF16)
        w_sc[:, tn:2 * tn] = wu_ref[...].astype(BF16)

    @pl.when((i > 0) & (((i - 1) % tiles_per_seq) == 0))
    def _():
        cg_ref[...] = jnp.zeros_like(cg_ref)
        cu_ref[...] = jnp.zeros_like(cu_ref)

    def conv(u, prev, cw_ref, cb_ref):
        u1 = jnp.where(rows == 0, prev[7:8, :], pltpu.roll(u, 1, axis=0))
        u2 = pltpu.roll(u, 2, axis=0)
        u2 = jnp.where(rows == 0, prev[6:7, :], jnp.where(rows == 1, prev[7:8, :], u2))
        cw = cw_ref[...]
        return cw[0:1, :] * u2 + cw[1:2, :] * u1 + cw[2:3, :] * u + cb_ref[...]

    def epilogue_piece(p, src, prev_g, prev_u):
        u = src[p * te:(p + 1) * te, :]
        ug = u[:, 0:tn]
        uu = u[:, tn:2 * tn]
        gate = conv(ug, prev_g, cwg_ref, cbg_ref)
        up = conv(uu, prev_u, cwu_ref, cbu_ref)
        o_ref[p * te:(p + 1) * te, :] = (gate * _sigmoid(gate) * up).astype(o_ref.dtype)
        return ug[te - SUBLANES:te, :], uu[te - SUBLANES:te, :]

    def run(mm_dst, ep_src):
        if ep_src is not None:
            prev_g, prev_u = cg_ref[...], cu_ref[...]
        for sb in range(n_sub):
            acc = None
            for kc in range(FFN_K_SPLIT):
                if mm_dst is not None:
                    part = _dot(h_ref[sb * ts:(sb + 1) * ts, kc * tk:(kc + 1) * tk], w_sc[kc * tk:(kc + 1) * tk, :])
                    acc = part if acc is None else acc + part
                if ep_src is not None:
                    prev_g, prev_u = epilogue_piece(sb * FFN_K_SPLIT + kc, ep_src, prev_g, prev_u)
            if mm_dst is not None:
                mm_dst[sb * ts:(sb + 1) * ts, :] = acc
        if ep_src is not None:
            cg_ref[...] = prev_g
            cu_ref[...] = prev_u

    @pl.when(i == 0)
    def _():
        run(slots[0], None)

    for parity in range(2):
        @pl.when((i > 0) & (i < n_tiles) & (i % 2 == parity))
        def _():
            run(slots[parity], slots[1 - parity])

    @pl.when(i == n_tiles)
    def _():
        run(None, slots[(n_tiles - 1) % 2])


def ffn_up(h, w_up, layer, conv_w, conv_b, seq, tm=1024, tn=256, ts=256):
    m, k = h.shape
    d_ff = w_up.shape[2] // 2
    tm = min(tm, seq)
    ts = min(ts, tm)
    nj = d_ff // tn
    n_tiles = m // tm
    return pl.pallas_call(
        functools.partial(_ffn_up_kernel, tiles_per_seq=seq // tm, n_tiles=n_tiles, tm=tm, ts=ts),
        grid=(nj, n_tiles + 1),
        in_specs=[
            pl.BlockSpec((tm, k), lambda j, i: (jnp.minimum(i, n_tiles - 1), 0)),
            _wspec(w_up, layer, tn, lambda j, i: j),
            _wspec(w_up, layer, tn, lambda j, i: nj + j),
            pl.BlockSpec((CONV_W, tn), lambda j, i: (0, j)),
            pl.BlockSpec((CONV_W, tn), lambda j, i: (0, nj + j)),
            pl.BlockSpec((1, tn), lambda j, i: (0, j)),
            pl.BlockSpec((1, tn), lambda j, i: (0, nj + j)),
        ],
        out_specs=pl.BlockSpec((tm, tn), lambda j, i: (jnp.maximum(i - 1, 0), j)),
        out_shape=jax.ShapeDtypeStruct((m, d_ff), BF16),
        scratch_shapes=[pltpu.VMEM((k, 2 * tn), BF16), pltpu.VMEM((tm, 2 * tn), F32), pltpu.VMEM((tm, 2 * tn), F32),
                        pltpu.VMEM((SUBLANES, tn), F32), pltpu.VMEM((SUBLANES, tn), F32)],
        compiler_params=_cparams(("parallel", "arbitrary")),
        name="ffn_up",
    )(h, w_up, w_up, conv_w, conv_w, conv_b, conv_b)


A_SUB = 16


def _hgrn_kernel(q_ref, f_ref, i_ref, g_ref, lb_ref, gain_ref, o_ref, st_ref, *, n_chunks, heads):
    @pl.when(pl.program_id(2) == 0)
    def _():
        st_ref[...] = jnp.zeros_like(st_ref)

    gain = gain_ref[...]
    tri = (lax.broadcasted_iota(I32, (CHUNK, CHUNK), 0) >= lax.broadcasted_iota(I32, (CHUNK, CHUNK), 1)).astype(BF16)
    rows_half = lax.broadcasted_iota(I32, (SUBLANES, A_DK), 0)
    n_sub = CHUNK // A_SUB

    def chunk(c, carry):
        rows = pl.ds(pl.multiple_of(c * CHUNK, CHUNK), CHUNK)
        hs = range(heads)
        cols = [slice(hh * A_DK, (hh + 1) * A_DK) for hh in hs]
        iv = [i_ref[rows, cl] for cl in cols]
        iv_b = [x.astype(BF16) for x in iv]
        st = [st_ref[hh] for hh in hs]
        qf, kin, b = [], [], []
        for hh in hs:
            lb = lb_ref[:, cols[hh]]
            z = f_ref[rows, cols[hh]]
            qv = q_ref[rows, cols[hh]]
            qf.append(qv * _sigmoid(qv))
            kin.append((1.0 - lb) * _sigmoid(-z))
            b.append(_dot_exact_lhs(tri, jnp.log(lb + (1.0 - lb) * _sigmoid(z))))
        o_inter = [_dot_nt((qf[hh] * jnp.exp(b[hh])).astype(BF16), st[hh].astype(BF16)) for hh in hs]
        outs = [[] for _ in hs]
        for si in range(n_sub):
            lo = si * A_SUB
            sub = slice(lo, lo + A_SUB)
            o_i = [o_inter[hh][sub] for hh in hs]
            if si > 0:
                att = []
                for hh in hs:
                    b_ref_row = b[hh][lo - 1:lo]
                    q_s = (qf[hh][sub] * jnp.exp(b[hh][sub] - b_ref_row)).astype(BF16)
                    k_s = (kin[hh][0:lo] * jnp.exp(b_ref_row - b[hh][0:lo])).astype(BF16)
                    att.append(_dot_nt(q_s, k_s))
                o_i = [o_i[hh] + _dot(att[hh].astype(BF16), iv_b[hh][0:lo]) for hh in hs]
            half = SUBLANES
            o_h = [[o_i[hh][0:half], o_i[hh][half:A_SUB]] for hh in hs]
            for s in range(A_SUB):
                for hh in hs:
                    b_s = b[hh][lo + s:lo + s + 1]
                    k_s = kin[hh][lo + s:lo + s + 1]
                    i_s = iv[hh][lo + s:lo + s + 1]
                    for part in range(s // half, A_SUB // half):
                        rs = slice(lo + part * half, lo + (part + 1) * half)
                        e = b[hh][rs] - b_s
                        if part == s // half:
                            e = jnp.where(rows_half >= s % half, e, -jnp.inf)
                        a = jnp.sum(qf[hh][rs] * jnp.exp(e) * k_s, axis=1, keepdims=True)
                        o_h[hh][part] = o_h[hh][part] + a * i_s
            for hh in hs:
                outs[hh].append(jnp.concatenate(o_h[hh], axis=0))
        for hh in hs:
            b_last = b[hh][CHUNK - 1:CHUNK]
            k_dec = (kin[hh] * jnp.exp(b_last - b[hh])).astype(BF16)
            st_ref[hh] = st[hh] * jnp.exp(b_last) + _dot_tn(iv_b[hh], k_dec)
        for hh in hs:
            o = jnp.concatenate(outs[hh], axis=0)
            gv = g_ref[rows, cols[hh]]
            ms = jnp.mean(o * o, axis=-1, keepdims=True)
            on = o * lax.rsqrt(ms + NORM_EPS) * gain
            o_ref[rows, cols[hh]] = (on * (gv * _sigmoid(gv))).astype(o_ref.dtype)
        return carry

    lax.fori_loop(0, n_chunks, chunk, 0)


def hgrn2(p3, lb, out_gain, s_blk=256, heads=4):
    bsz, seq, _ = p3.shape
    s_blk = min(s_blk, seq)
    width = heads * A_DK
    groups = A_HEADS // heads
    c0 = COL_A // width

    def col(part):
        return pl.BlockSpec((None, s_blk, width), lambda b, h, s, part=part: (b, s, c0 + part * groups + h))

    return pl.pallas_call(
        functools.partial(_hgrn_kernel, n_chunks=s_blk // CHUNK, heads=heads),
        grid=(bsz, groups, seq // s_blk),
        in_specs=[col(0), col(1), col(2), col(3),
                  pl.BlockSpec((1, width), lambda b, h, s: (0, h)),
                  pl.BlockSpec((1, A_DV), lambda b, h, s: (0, 0))],
        out_specs=pl.BlockSpec((None, s_blk, width), lambda b, h, s: (b, s, h)),
        out_shape=jax.ShapeDtypeStruct((bsz, seq, A_W), BF16),
        scratch_shapes=[pltpu.VMEM((heads, A_DV, A_DK), F32)],
        compiler_params=_cparams(("parallel", "parallel", "arbitrary")),
        name="hgrn2",
    )(p3, p3, p3, p3, lb.reshape(1, A_HEADS * A_DK), out_gain.reshape(1, A_DV))


def _rwkv_prep_kernel(r_ref, k_ref, v_ref, l_ref, rp_ref, kp_ref, vp_ref, lp_ref,
                      mu_ref, w0_ref, w2_ref, a0_ref, a2_ref, g2_ref, kk_ref, ka_ref,
                      ro_ref, ld_ref, k2_ref, vo_ref, kko_ref, kka_ref, go_ref, *, tiles_per_seq, tm):
    first = (pl.program_id(0) % tiles_per_seq) == 0
    rows = lax.broadcasted_iota(I32, (tm, B_W), 0)

    def shifted(cur_ref, prev_ref, part):
        cur = cur_ref[...]
        prev = jnp.where(first, 0.0, prev_ref[...])[SUBLANES - 1:SUBLANES, :]
        sh = jnp.where(rows == 0, prev, pltpu.roll(cur, 1, axis=0))
        return cur + (sh - cur) * mu_ref[part:part + 1, :]

    r = shifted(r_ref, rp_ref, 0)
    k = shifted(k_ref, kp_ref, 1)
    v = shifted(v_ref, vp_ref, 2)
    lo = shifted(l_ref, lp_ref, 3)
    lo_a = lo[:, 0:LANES]
    lo_g = lo[:, LANES:3 * LANES]
    wpre = w0_ref[...] + _dot(jnp.tanh(lo_a).astype(BF16), w2_ref[...])
    y = -wpre
    softplus = jnp.maximum(y, 0.0) + jnp.log(1.0 + jnp.exp(-jnp.abs(y)))
    w_log = -softplus - 0.5
    ld_ref[...] = -jnp.exp(w_log)
    a = _sigmoid(a0_ref[...] + _dot(lo_a.astype(BF16), a2_ref[...]))
    go_ref[...] = _dot(_sigmoid(lo_g).astype(BF16), g2_ref[...]).astype(go_ref.dtype)
    kk = k * kk_ref[...]
    bd = (lax.broadcasted_iota(I32, (LANES, LANES), 0) // B_HD
          == lax.broadcasted_iota(I32, (LANES, LANES), 1) // B_HD).astype(BF16)
    sq = kk * kk
    ss = jnp.concatenate([_dot_exact_rhs(sq[:, j * LANES:(j + 1) * LANES], bd) for j in range(B_W // LANES)], axis=1)
    kk = kk / jnp.maximum(jnp.sqrt(ss), 1e-12)
    ro_ref[...] = r.astype(ro_ref.dtype)
    vo_ref[...] = v.astype(vo_ref.dtype)
    kko_ref[...] = kk.astype(kko_ref.dtype)
    kka_ref[...] = (kk * a).astype(kka_ref.dtype)
    k2_ref[...] = (k * (1.0 + (a - 1.0) * ka_ref[...])).astype(k2_ref.dtype)


def rwkv_prep(p, seq, mu4, w0, w2p, a0, a2p, g2p, k_k, k_a, tm=256):
    m = p.shape[0]
    tm = min(tm, seq)
    cb = COL_B_RKV // B_W
    pb = tm // SUBLANES

    def cur(j):
        return pl.BlockSpec((tm, B_W), lambda i, j=j: (i, cb + j))

    def prev(j):
        return pl.BlockSpec((SUBLANES, B_W), lambda i, j=j: (jnp.maximum(i * pb - 1, 0), cb + j))

    def full(a):
        return pl.BlockSpec(a.shape, lambda i: (0, 0))

    row = lambda a: a.reshape(1, B_W)
    params = [mu4, row(w0), w2p, row(a0), a2p, g2p, row(k_k), row(k_a)]
    out = jax.ShapeDtypeStruct((m, B_W), F32)
    out_b = jax.ShapeDtypeStruct((m, B_W), BF16)
    return pl.pallas_call(
        functools.partial(_rwkv_prep_kernel, tiles_per_seq=seq // tm, tm=tm),
        grid=(m // tm,),
        in_specs=[cur(0), cur(1), cur(2), cur(3), prev(0), prev(1), prev(2), prev(3)] + [full(a) for a in params],
        out_specs=[pl.BlockSpec((tm, B_W), lambda i: (i, 0))] * 7,
        out_shape=[out_b, out, out_b, out_b, out_b, out_b, out_b],
        compiler_params=_cparams(("parallel",)),
        name="rwkv_prep",
    )(p, p, p, p, p, p, p, p, *params)


B_T = 64


def _rwkv_kernel(r_ref, ld_ref, k_ref, v_ref, kk_ref, kka_ref, g_ref, rk_ref, lnw_ref, lnb_ref, o_ref, st_ref, *,
                 n_chunks, pairs):
    @pl.when(pl.program_id(2) == 0)
    def _():
        st_ref[...] = jnp.zeros_like(st_ref)

    t = B_T
    ii = lax.broadcasted_iota(I32, (t, t), 0)
    jj = lax.broadcasted_iota(I32, (t, t), 1)
    tri = (ii >= jj).astype(BF16)
    i2 = lax.broadcasted_iota(I32, (2 * t, 2 * t), 0)
    j2 = lax.broadcasted_iota(I32, (2 * t, 2 * t), 1)
    same = (i2 // t) == (j2 // t)
    strict_bd = same & ((i2 % t) > (j2 % t))
    incl_bd = same & ((i2 % t) >= (j2 % t))
    head_bd = same.astype(BF16)
    lane = lax.broadcasted_iota(I32, (t, LANES), 1)
    h0 = lane < B_HD

    def stack(x):
        return jnp.concatenate([jnp.where(h0, x, jnp.zeros_like(x)), jnp.where(h0, jnp.zeros_like(x), x)], axis=0)

    def head_sum(x):
        hi, lo = _split2(x)
        return _dot(hi, head_bd) + _dot(lo, head_bd)

    def chunk(c, carry):
        sl = pl.ds(pl.multiple_of(c * t, t), t)
        prs = range(pairs)
        cols = [slice(pp * LANES, (pp + 1) * LANES) for pp in prs]
        r = [r_ref[sl, cl].astype(F32) for cl in cols]
        ld = [ld_ref[sl, cl] for cl in cols]
        k = [k_ref[sl, cl].astype(F32) for cl in cols]
        v = [v_ref[sl, cl].astype(F32) for cl in cols]
        cs = []
        for pp in prs:
            ld_hi, ld_lo = _split2(ld[pp])
            cs.append(_dot(tri, ld_hi) + _dot(tri, ld_lo))
        lhs, rhs, vs = [], [], []
        for pp in prs:
            e_neg = jnp.exp(-cs[pp])
            kka = kka_ref[sl, cols[pp]].astype(F32)
            a_t = (-kk_ref[sl, cols[pp]].astype(F32) * jnp.exp(cs[pp] - ld[pp])).astype(BF16)
            r_t = (r[pp] * jnp.exp(cs[pp])).astype(BF16)
            b_t = (kka * e_neg).astype(BF16)
            k_t = (k[pp] * e_neg).astype(BF16)
            lhs.append(jnp.concatenate([stack(a_t), stack(r_t)], axis=0))
            rhs.append(jnp.concatenate([b_t, b_t, k_t, k_t], axis=0))
            vs.append(stack(v[pp].astype(BF16)))
        sc = [_dot_nt(lhs[pp], rhs[pp]) for pp in prs]
        st = [st_ref[pp] for pp in prs]
        proj = [_dot_nt(lhs[pp], st[pp].astype(BF16)) for pp in prs]
        n = [jnp.where(strict_bd, sc[pp][0:2 * t, 0:2 * t], 0.0).astype(BF16) for pp in prs]
        xs = [proj[pp][0:2 * t] + _dot(jnp.where(strict_bd, sc[pp][0:2 * t, 2 * t:4 * t], 0.0).astype(BF16), vs[pp])
              for pp in prs]
        for it in range(6):
            xs = [xs[pp] + _dot(n[pp], xs[pp].astype(BF16)) for pp in prs]
            if it < 5:
                n = [_dot(n[pp], n[pp]).astype(BF16) for pp in prs]
        os_ = []
        for pp in prs:
            m_r = jnp.concatenate([jnp.where(incl_bd, sc[pp][2 * t:4 * t, 0:2 * t], 0.0),
                                   jnp.where(incl_bd, sc[pp][2 * t:4 * t, 2 * t:4 * t], 0.0)], axis=1)
            uv = jnp.concatenate([xs[pp].astype(BF16), vs[pp]], axis=0)
            os_.append(proj[pp][2 * t:4 * t] + _dot(m_r.astype(BF16), uv))
        upd = []
        for pp in prs:
            u = xs[pp][0:t] + xs[pp][t:2 * t]
            c_last = cs[pp][t - 1:t]
            dec = jnp.exp(c_last - cs[pp])
            upd.append(_dot_tn(jnp.concatenate([u, v[pp]], axis=0).astype(BF16),
                               jnp.concatenate([kka_ref[sl, cols[pp]].astype(F32) * dec, k[pp] * dec], axis=0).astype(BF16)))
        for pp in prs:
            st_ref[pp] = st[pp] * jnp.exp(cs[pp][t - 1:t]) + jnp.where(same, upd[pp], 0.0)
        inv = 1.0 / B_HD
        o = [os_[pp][0:t] + os_[pp][t:2 * t] for pp in prs]
        mean = [head_sum(o[pp]) * inv for pp in prs]
        d = [o[pp] - mean[pp] for pp in prs]
        var = [head_sum(d[pp] * d[pp]) * inv for pp in prs]
        bonus = [head_sum(r[pp] * k[pp] * rk_ref[:, cols[pp]]) for pp in prs]
        for pp in prs:
            on = d[pp] * lax.rsqrt(var[pp] + B_GN_EPS) * lnw_ref[:, cols[pp]] + lnb_ref[:, cols[pp]]
            o_ref[sl, cols[pp]] = ((on + bonus[pp] * v[pp]) * g_ref[sl, cols[pp]].astype(F32)).astype(o_ref.dtype)
        return carry

    lax.fori_loop(0, n_chunks, chunk, 0)


def rwkv_recurrence(r, ld, k2, v, kk, kka, g, r_k, ln_w, ln_b, bsz, seq, s_blk=256, pairs=8):
    s_blk = min(s_blk, seq)
    width = pairs * LANES
    groups = B_W // width
    args = [a.reshape(bsz, seq, B_W) for a in (r, ld, k2, v, kk, kka, g)]
    blk = pl.BlockSpec((None, s_blk, width), lambda b, h, s: (b, s, h))
    par = pl.BlockSpec((1, width), lambda b, h, s: (0, h))
    return pl.pallas_call(
        functools.partial(_rwkv_kernel, n_chunks=s_blk // B_T, pairs=pairs),
        grid=(bsz, groups, seq // s_blk),
        in_specs=[blk] * 7 + [par] * 3,
        out_specs=blk,
        out_shape=jax.ShapeDtypeStruct((bsz, seq, B_W), BF16),
        scratch_shapes=[pltpu.VMEM((pairs, LANES, LANES), F32)],
        compiler_params=_cparams(("parallel", "parallel", "arbitrary")),
        name="rwkv_recurrence",
    )(*args, r_k.reshape(1, B_W), ln_w.reshape(1, B_W), ln_b.reshape(1, B_W))


def _rope(x, cos_t, sin_t, lane_in_head, half):
    partner = jnp.where(lane_in_head < half, pltpu.roll(x, LANES - half, axis=1), pltpu.roll(x, half, axis=1))
    return x * cos_t + partner * sin_t


def _attn_prep_kernel(q_ref, k_ref, v_ref, qi_ref, kw_ref, ca_ref, sa_ref, ci_ref, si_ref, qg_ref, kg_ref,
                      qo_ref, ko_ref, vo_ref, qio_ref, kio_ref, wo_ref, *, tm):
    ca, sa, ci, si = ca_ref[...], sa_ref[...], ci_ref[...], si_ref[...]
    lane = lax.broadcasted_iota(I32, (tm, LANES), 1)
    lane_i = lane % IDX_HD
    half_a = C_HD // ROPE_FRACTION // 2
    half_i = IDX_HD // ROPE_FRACTION // 2
    scale = C_HD ** -0.5 * LOG2_E

    def norm_rope(x, gain):
        ms = jnp.mean(x * x, axis=-1, keepdims=True)
        return _rope(x * lax.rsqrt(ms + NORM_EPS) * gain, ca, sa, lane, half_a)

    for h in range(C_HEADS):
        sl = slice(h * C_HD, (h + 1) * C_HD)
        qo_ref[:, sl] = (norm_rope(q_ref[:, sl], qg_ref[...]) * scale).astype(qo_ref.dtype)
    for h in range(C_KV_HEADS):
        sl = slice(h * C_HD, (h + 1) * C_HD)
        ko_ref[:, sl] = norm_rope(k_ref[:, sl], kg_ref[...]).astype(ko_ref.dtype)
    vo_ref[...] = v_ref[...].astype(vo_ref.dtype)
    for j in range(IDX_HEADS * IDX_HD // LANES):
        sl = slice(j * LANES, (j + 1) * LANES)
        qio_ref[:, sl] = _rope(qi_ref[:, sl], ci, si, lane_i, half_i).astype(qio_ref.dtype)
    kw = kw_ref[:, 0:LANES]
    kr = _rope(kw, ci, si, lane_i, half_i)
    kio_ref[...] = jnp.where(lane < IDX_HD, kr, pltpu.roll(kr, IDX_HD, axis=1)).astype(kio_ref.dtype)
    w = pltpu.roll(kw, LANES - IDX_HD, axis=1) * (IDX_HEADS ** -0.5 * IDX_HD ** -0.5)
    wo_ref[...] = jnp.where(lane < IDX_HEADS, w, 0.0)


def attn_prep(p, tabs, q_gain, k_gain, tm=256):
    m = p.shape[0]
    tm = min(tm, m)

    def colblk(width, off):
        return pl.BlockSpec((tm, width), lambda i: (i, off // width))

    tab = pl.BlockSpec((tm, LANES), lambda i: (i, 0))
    gain = pl.BlockSpec((1, C_HD), lambda i: (0, 0))
    kw_width = 512

    def out(width, dtype):
        return jax.ShapeDtypeStruct((m, width), dtype), pl.BlockSpec((tm, width), lambda i: (i, 0))

    outs = [out(C_W, BF16), out(C_KV_W, BF16), out(C_KV_W, BF16), out(IDX_HEADS * IDX_HD, BF16),
            out(LANES, BF16), out(LANES, F32)]
    return pl.pallas_call(
        functools.partial(_attn_prep_kernel, tm=tm),
        grid=(m // tm,),
        in_specs=[colblk(C_W, COL_C_Q), colblk(C_KV_W, COL_C_K), colblk(C_KV_W, COL_C_V),
                  colblk(IDX_HEADS * IDX_HD, COL_C_QI), colblk(kw_width, COL_C_KW), tab, tab, tab, tab, gain, gain],
        out_specs=[o[1] for o in outs],
        out_shape=[o[0] for o in outs],
        compiler_params=_cparams(("parallel",)),
        name="attn_prep",
    )(p, p, p, p, p, *tabs, q_gain.reshape(1, C_HD), k_gain.reshape(1, C_HD))


V_ONES = 16


def _attn_kernel(q_ref, k_ref, vt_ref, qi_ref, ki_ref, w_ref, o_ref, key_sc, bias_sc, m_sc, acc_sc, *, tq, tk, topk):
    qb = pl.program_id(1)
    nkb = ((qb + 1) * tq + tk - 1) // tk
    lane = lax.broadcasted_iota(I32, (tq, LANES), 1)
    w_t = w_ref[...].T
    q_chunk = lax.shift_right_logical(qb * tq + lax.broadcasted_iota(I32, (1, tq), 1), CHUNK_SHIFT)
    key_chunk_in_blk = lax.shift_right_logical(lax.broadcasted_iota(I32, (tk, tq), 0), CHUNK_SHIFT)
    n_pairs = IDX_HEADS * IDX_HD // LANES
    fold = 64

    q_pairs = []
    for hp in range(n_pairs):
        qp = qi_ref[:, hp * LANES:(hp + 1) * LANES]
        zero = jnp.zeros_like(qp)
        q_pairs.append(jnp.concatenate([jnp.where(lane < IDX_HD, qp, zero), jnp.where(lane < IDX_HD, zero, qp)], axis=0))

    def score_block(kb, carry):
        c0 = pl.multiple_of(kb * tk, tk)
        ki2 = ki_ref[pl.ds(c0, tk), :]
        sc = jnp.zeros((tk, tq), F32)
        for hp in range(n_pairs):
            rel = jnp.maximum(_dot_nt(ki2, q_pairs[hp]), 0.0)
            sc = sc + w_t[2 * hp:2 * hp + 1, :] * rel[:, 0:tq] + w_t[2 * hp + 1:2 * hp + 2, :] * rel[:, tq:2 * tq]
        sc = jnp.where(sc == 0.0, 0.0, sc)
        bits = lax.bitcast_convert_type(sc, I32)
        skey = bits ^ ((bits >> 31) & 0x7FFFFFFF)
        allowed = key_chunk_in_blk <= q_chunk - kb * (tk // CHUNK)
        key_sc[kb] = jnp.where(allowed, skey, INT_MIN)
        return carry

    lax.fori_loop(0, nkb, score_block, 0)

    def count_ge(cand):
        def body(kb, acc):
            hit = jnp.where(key_sc[kb] >= cand, 1.0, 0.0)
            for j in range(tk // fold):
                acc = acc + hit[j * fold:(j + 1) * fold]
            return acc
        acc = lax.fori_loop(0, nkb, body, jnp.zeros((fold, tq), F32))
        return jnp.sum(acc, axis=0, keepdims=True)

    kf = float(topk)
    cnt0 = count_ge(jnp.zeros((1, tq), I32))
    thr = jnp.where(cnt0 >= kf, 0, INT_MIN).astype(I32)
    cnt_thr = jnp.where(cnt0 >= kf, cnt0, kf)

    def bit_step(i, carry):
        thr, cnt_thr = carry
        cand = thr | (jnp.int32(1) << (30 - i))
        cnt = count_ge(cand)
        ok = cnt >= kf
        return jnp.where(ok, cand, thr), jnp.where(ok, cnt, cnt_thr)

    thr, cnt_thr = lax.fori_loop(0, 31, bit_step, (thr, cnt_thr))
    thr = jnp.maximum(thr, INT_MIN + 1)
    any_ties = jnp.max(cnt_thr) > kf

    def bias_plain():
        def bias_block(kb, carry):
            bias_sc[kb] = jnp.where(key_sc[kb] >= thr, 0.0, NEG_BIG)
            return carry
        lax.fori_loop(0, nkb, bias_block, 0)

    def bias_with_ties():
        need = kf - count_ge(thr + 1)
        tri = (lax.broadcasted_iota(I32, (tk, tk), 0) >= lax.broadcasted_iota(I32, (tk, tk), 1)).astype(BF16)

        def bias_block(kb, seen):
            keys = key_sc[kb]
            eq = keys == thr
            eq_f = jnp.where(eq, 1.0, 0.0)
            rank = _dot(tri, eq_f.astype(BF16)) + seen
            take = (keys > thr) | (eq & (rank <= need))
            bias_sc[kb] = jnp.where(take, 0.0, NEG_BIG)
            return rank[tk - 1:tk]

        lax.fori_loop(0, nkb, bias_block, jnp.zeros((1, tq), F32))

    lax.cond(any_ties, bias_with_ties, bias_plain)

    rep = C_HEADS // C_KV_HEADS
    ve = C_HD + V_ONES
    qs = [jnp.concatenate([q_ref[:, (g * rep + r) * C_HD:(g * rep + r + 1) * C_HD] for r in range(rep)], axis=0)
          for g in range(C_KV_HEADS)]
    m_sc[...] = jnp.full(m_sc.shape, NEG_BIG, F32)
    acc_sc[...] = jnp.zeros(acc_sc.shape, F32)

    def body(kb, carry):
        c0 = pl.multiple_of(kb * tk, tk)
        bias = bias_sc[kb]
        s_all = [_dot_nt(k_ref[pl.ds(c0, tk), g * C_HD:(g + 1) * C_HD], qs[g]) for g in range(C_KV_HEADS)]
        for g in range(C_KV_HEADS):
            s = jnp.concatenate([s_all[g][:, r * tq:(r + 1) * tq] + bias for r in range(rep)], axis=1)
            m_prev = m_sc[g]
            m_new = jnp.maximum(m_prev, jnp.max(s, axis=0, keepdims=True))
            p = jnp.exp2(s - m_new)
            acc_sc[g] = jnp.exp2(m_prev - m_new) * acc_sc[g] + _dot(vt_ref[kb, g * ve:(g + 1) * ve, :], p.astype(BF16))
            m_sc[g] = m_new
        return carry

    lax.fori_loop(0, nkb, body, 0)
    for g in range(C_KV_HEADS):
        acc = acc_sc[g]
        out_t = acc[0:C_HD] / acc[C_HD:C_HD + 1]
        for r in range(rep):
            h = g * rep + r
            o_ref[:, h * C_HD:(h + 1) * C_HD] = out_t[:, r * tq:(r + 1) * tq].T.astype(o_ref.dtype)


def sparse_attention(qn, kn, vn, qi, ki2, w, bsz, seq, tq=128, tk=512):
    tk = min(tk, seq)
    topk = min(TOPK_MAX, seq // 4)
    nkb = seq // tk
    rep = C_HEADS // C_KV_HEADS
    ve = C_HD + V_ONES
    r3 = lambda a: a.reshape(bsz, seq, a.shape[-1])
    v_t = vn.reshape(bsz, nkb, tk, C_KV_HEADS, C_HD).transpose(0, 1, 3, 4, 2)
    v_t = jnp.concatenate([v_t, jnp.ones((bsz, nkb, C_KV_HEADS, V_ONES, tk), v_t.dtype)], axis=3)
    v_t = v_t.reshape(bsz, nkb, C_KV_HEADS * ve, tk)
    qblk = lambda width: pl.BlockSpec((None, tq, width), lambda b, i: (b, i, 0))
    sblk = lambda width: pl.BlockSpec((None, seq, width), lambda b, i: (b, 0, 0))
    return pl.pallas_call(
        functools.partial(_attn_kernel, tq=tq, tk=tk, topk=topk),
        grid=(bsz, seq // tq),
        in_specs=[qblk(C_W), sblk(C_KV_W), pl.BlockSpec((None, nkb, C_KV_HEADS * ve, tk), lambda b, i: (b, 0, 0, 0)),
                  qblk(IDX_HEADS * IDX_HD), sblk(LANES), qblk(LANES)],
        out_specs=qblk(C_W),
        out_shape=jax.ShapeDtypeStruct((bsz, seq, C_W), BF16),
        scratch_shapes=[pltpu.VMEM((nkb, tk, tq), I32), pltpu.VMEM((nkb, tk, tq), F32),
                        pltpu.VMEM((C_KV_HEADS, 1, rep * tq), F32), pltpu.VMEM((C_KV_HEADS, ve, rep * tq), F32)],
        compiler_params=_cparams(("parallel", "arbitrary")),
        name="sparse_attention",
    )(r3(qn), r3(kn), v_t, r3(qi), r3(ki2), r3(w))


def _rope_tables(positions):
    pos = positions.astype(F32).reshape(-1, 1)

    def tables(head_dim):
        rot = head_dim // ROPE_FRACTION
        half = rot // 2
        inv_freq = ROPE_THETA ** (-jnp.arange(0, rot, 2, dtype=F32) / rot)
        ang = pos * inv_freq
        cos, sin = jnp.cos(ang), jnp.sin(ang)
        ones = jnp.ones((pos.shape[0], head_dim - rot), F32)
        c = jnp.concatenate([cos, cos, ones], axis=1)
        s = jnp.concatenate([-sin, sin, 0.0 * ones], axis=1)
        reps = LANES // head_dim
        return jnp.tile(c, (1, reps)), jnp.tile(s, (1, reps))

    ca, sa = tables(C_HD)
    ci, si = tables(IDX_HD)
    return ca, sa, ci, si


def _pad_rows(a, before, total):
    return jnp.zeros((total, a.shape[1]), a.dtype).at[before:before + a.shape[0]].set(a)


def kernel(x, positions, norm_mix, w_in, hgrn_lb_logits, hgrn_out_norm, rwkv_mu, rwkv_w0, rwkv_w2, rwkv_a0, rwkv_a2,
           rwkv_g2, rwkv_k_k, rwkv_k_a, rwkv_r_k, rwkv_ln_w, rwkv_ln_b, q_norm, k_norm, w_branch_a, w_branch_b,
           w_branch_c, w_out, norm_ffn, w_up, conv_w, conv_b, w_down):
    bsz, seq, d_model = x.shape
    depth = w_in.shape[0]
    m = bsz * seq
    tabs = _rope_tables(positions)
    lb_all = jnp.cumsum(jax.nn.softmax(hgrn_lb_logits.astype(F32), axis=0), axis=0)
    lb_all = lb_all - lb_all[0:1]
    n_lora = B_LORA_DECAY + B_LORA_A + B_LORA_G
    ab_cols = 4 * A_W + 3 * B_W + n_lora
    c_cols = C_W + 2 * C_KV_W + IDX_HEADS * IDX_HD + IDX_HD + IDX_HEADS
    w_in_t = jnp.swapaxes(w_in, 1, 2)
    w_in_ab = pack_cols(w_in_t, 0, ab_cols, W_AB)
    w_in_c = pack_cols(w_in_t, ab_cols, c_cols, W_C)
    w_in_g = pack_cols(w_in_t, ab_cols + c_cols, 3 * d_model, 3 * d_model)
    w_a, w_b, w_c = w_branch_a.astype(BF16), w_branch_b.astype(BF16), w_branch_c.astype(BF16)
    w_o, w_d = w_out.astype(BF16), w_down.astype(BF16)

    xf = x.reshape(m, d_model)
    for l in range(depth):
        h = rmsnorm(xf, norm_mix[l])
        p = proj_in(h, w_in_ab, l, F32)
        p_c = proj_in(h, w_in_c, l, F32)
        p_g = proj_in(h, w_in_g, l, BF16)

        o_a = hgrn2(p.reshape(bsz, seq, -1), lb_all[l], hgrn_out_norm[l]).reshape(m, A_W)

        mu = rwkv_mu[l]
        mu4 = jnp.stack([mu[0:B_W], mu[B_W:2 * B_W], mu[2 * B_W:3 * B_W],
                         jnp.pad(mu[3 * B_W:], (0, B_W - n_lora))])
        w2p = _pad_rows(rwkv_w2[l], 0, LANES).astype(BF16)
        a2p = _pad_rows(rwkv_a2[l], B_LORA_DECAY, LANES).astype(BF16)
        g2p = _pad_rows(rwkv_g2[l], 0, 2 * LANES).astype(BF16)
        parts = rwkv_prep(p, seq, mu4, rwkv_w0[l], w2p, rwkv_a0[l], a2p, g2p, rwkv_k_k[l], rwkv_k_a[l])
        o_b = rwkv_recurrence(*parts, rwkv_r_k[l], rwkv_ln_w[l], rwkv_ln_b[l], bsz, seq).reshape(m, B_W)

        qn, kn, vn, qi, ki2, wi = attn_prep(p_c, tabs, q_norm[l], k_norm[l])
        o_c = sparse_attention(qn, kn, vn, qi, ki2, wi, bsz, seq).reshape(m, C_W)

        merged = merge_branches(o_a, o_b, o_c, w_a, w_b, w_c, l, p_g, d_model)
        xf = matmul_residual(merged, w_o, l, xf, tm=1024, tn=512)

        h2 = rmsnorm(xf, norm_ffn[l])
        act = ffn_up(h2, w_up, l, conv_w[l], conv_b[l].reshape(1, -1), seq)
        xf = matmul_residual(act, w_d, l, xf, tm=512, tn=512)
    return xf.reshape(bsz, seq, d_model)
```

```python
import functools

import jax
import jax.numpy as jnp
import numpy as np
from jax import lax
from jax.experimental import pallas as pl
from jax.experimental.pallas import tpu as pltpu

F32 = jnp.float32
BF16 = jnp.bfloat16
I32 = jnp.int32

CHUNK = 64
CHUNK_SHIFT = 6
ROPE_THETA = 500000.0
ROPE_FRACTION = 4
NORM_EPS = 1e-6
A_HEADS, A_DK, A_DV = 8, 128, 128
A_W = A_HEADS * A_DV
B_HEADS, B_HD = 16, 64
B_W = B_HEADS * B_HD
B_LORA_DECAY, B_LORA_A, B_LORA_G = 64, 64, 160
B_GN_EPS = 64e-5
C_HEADS, C_KV_HEADS, C_HD = 16, 4, 128
C_W = C_HEADS * C_HD
C_KV_W = C_KV_HEADS * C_HD
IDX_HEADS, IDX_HD = 16, 64
TOPK_MAX = 256
CONV_W = 3

LANES = 128
SUBLANES = 8
VMEM_LIMIT = 56 * 1024 * 1024

COL_A = 0
COL_B_RKV = 4096
COL_B_LORA = 7168
W_AB = 8192
COL_C_Q = 0
COL_C_K = 2048
COL_C_V = 2560
COL_C_QI = 3072
COL_C_KW = 4096
W_C = 4608
LOG2_E = 1.4426950408889634
NEG_BIG = -1e30
INT_MIN = -2147483648


def _sigmoid(x):
    return 1.0 / (1.0 + jnp.exp(-x))


def _dot(a, b):
    return jnp.dot(a, b, preferred_element_type=F32)


def _dot_nt(a, b):
    return lax.dot_general(a, b, (((1,), (1,)), ((), ())), preferred_element_type=F32)


def _dot_tn(a, b):
    return lax.dot_general(a, b, (((0,), (0,)), ((), ())), preferred_element_type=F32)


def _split3(x):
    hi = x.astype(BF16)
    r1 = x - hi.astype(F32)
    mid = r1.astype(BF16)
    lo = (r1 - mid.astype(F32)).astype(BF16)
    return hi, mid, lo


def _dot_exact_lhs(m_bf16, x):
    hi, mid, lo = _split3(x)
    return _dot(m_bf16, hi) + _dot(m_bf16, mid) + _dot(m_bf16, lo)


def _dot_exact_rhs(x, m_bf16):
    hi, mid, lo = _split3(x)
    return _dot(hi, m_bf16) + _dot(mid, m_bf16) + _dot(lo, m_bf16)


def _split2(x):
    hi = x.astype(BF16)
    lo = (x - hi.astype(F32)).astype(BF16)
    return hi, lo


def _cparams(sem):
    return pltpu.CompilerParams(dimension_semantics=sem, vmem_limit_bytes=VMEM_LIMIT)


def _rmsnorm_kernel(x_ref, g_ref, o_ref):
    x = x_ref[...]
    ms = jnp.mean(x * x, axis=-1, keepdims=True)
    o_ref[...] = (x * lax.rsqrt(ms + NORM_EPS) * g_ref[...]).astype(o_ref.dtype)


def rmsnorm(x, gain, tm=256):
    m, d = x.shape
    return pl.pallas_call(
        _rmsnorm_kernel,
        grid=(m // tm,),
        in_specs=[pl.BlockSpec((tm, d), lambda i: (i, 0)), pl.BlockSpec((1, d), lambda i: (0, 0))],
        out_specs=pl.BlockSpec((tm, d), lambda i: (i, 0)),
        out_shape=jax.ShapeDtypeStruct((m, d), BF16),
        compiler_params=_cparams(("parallel",)),
        name="rmsnorm",
    )(x, gain.reshape(1, d))


def _wspec(w, layer, tn, jmap):
    return pl.BlockSpec((None, w.shape[1], tn), lambda *g: (layer, 0, jmap(*g)))


def _proj_in_kernel(h_ref, w_ref, o_ref):
    o_ref[...] = _dot(h_ref[...], w_ref[...]).astype(o_ref.dtype)


def proj_in(h, w, layer, out_dtype, tm=1024, tn=512):
    m, k = h.shape
    n = w.shape[2]
    tm = min(tm, m)
    return pl.pallas_call(
        _proj_in_kernel,
        grid=(m // tm, n // tn),
        in_specs=[pl.BlockSpec((tm, k), lambda i, j: (i, 0)), _wspec(w, layer, tn, lambda i, j: j)],
        out_specs=pl.BlockSpec((tm, tn), lambda i, j: (i, j)),
        out_shape=jax.ShapeDtypeStruct((m, n), out_dtype),
        compiler_params=_cparams(("parallel", "arbitrary")),
        name="proj_in",
    )(h, w)


def _pack_cols_kernel(*refs, shift, width):
    a_ref, o_ref = refs[0], refs[-1]
    t = pl.program_id(1)
    a = a_ref[...]
    if shift:
        a = jnp.concatenate([a[shift:], refs[1][0:shift]], axis=0)
    col = t * LANES + lax.broadcasted_iota(I32, a.shape, 0)
    o_ref[...] = jnp.where(col < width, a, 0.0).T.astype(o_ref.dtype)


def pack_cols(w_t, src_start, width, dst_width):
    depth, n_src, k = w_t.shape
    q0, shift = divmod(src_start, LANES)
    assert shift % SUBLANES == 0
    last = (n_src - 1) // LANES

    def src(extra):
        return pl.BlockSpec((None, LANES, k), lambda l, t: (l, jnp.minimum(q0 + t + extra, last), 0))

    w = w_t
    srcs = [src(0), src(1)] if shift else [src(0)]
    return pl.pallas_call(
        functools.partial(_pack_cols_kernel, shift=shift, width=width),
        grid=(depth, dst_width // LANES),
        in_specs=srcs,
        out_specs=pl.BlockSpec((None, k, LANES), lambda l, t: (l, 0, t)),
        out_shape=jax.ShapeDtypeStruct((depth, k, dst_width), BF16),
        compiler_params=_cparams(("parallel", "arbitrary")),
        name="pack_cols",
    )(*([w] * len(srcs)))


def _mm_res_kernel(a_ref, b_ref, r_ref, o_ref):
    o_ref[...] = r_ref[...] + _dot(a_ref[...], b_ref[...])


def matmul_residual(a, b, layer, res, tm=512, tn=512):
    m, k = a.shape
    n = b.shape[2]
    tm = min(tm, m)
    tn = min(tn, n)
    return pl.pallas_call(
        _mm_res_kernel,
        grid=(m // tm, n // tn),
        in_specs=[
            pl.BlockSpec((tm, k), lambda i, j: (i, 0)),
            _wspec(b, layer, tn, lambda i, j: j),
            pl.BlockSpec((tm, tn), lambda i, j: (i, j)),
        ],
        out_specs=pl.BlockSpec((tm, tn), lambda i, j: (i, j)),
        out_shape=jax.ShapeDtypeStruct((m, n), F32),
        compiler_params=_cparams(("parallel", "arbitrary")),
        name="matmul_residual",
    )(a, b, res)


MERGE_ROWS = 256


def _merge_kernel(oa_ref, ob_ref, oc_ref, wa_ref, wb_ref, wc_ref, ga_ref, gb_ref, gc_ref, o_ref):
    ts = min(MERGE_ROWS, o_ref.shape[0])
    for sb in range(o_ref.shape[0] // ts):
        rows = slice(sb * ts, (sb + 1) * ts)
        acc = _sigmoid(ga_ref[rows, :].astype(F32)) * _dot(oa_ref[rows, :], wa_ref[...])
        acc += _sigmoid(gb_ref[rows, :].astype(F32)) * _dot(ob_ref[rows, :], wb_ref[...])
        acc += _sigmoid(gc_ref[rows, :].astype(F32)) * _dot(oc_ref[rows, :], wc_ref[...])
        o_ref[rows, :] = acc.astype(o_ref.dtype)


def merge_branches(o_a, o_b, o_c, w_a, w_b, w_c, layer, p, d_model, tm=1024, tn=512):
    m = o_a.shape[0]
    tm = min(tm, m)
    g0 = 0
    gstep = d_model // tn
    return pl.pallas_call(
        _merge_kernel,
        grid=(m // tm, d_model // tn),
        in_specs=[
            pl.BlockSpec((tm, o_a.shape[1]), lambda i, j: (i, 0)),
            pl.BlockSpec((tm, o_b.shape[1]), lambda i, j: (i, 0)),
            pl.BlockSpec((tm, o_c.shape[1]), lambda i, j: (i, 0)),
            _wspec(w_a, layer, tn, lambda i, j: j),
            _wspec(w_b, layer, tn, lambda i, j: j),
            _wspec(w_c, layer, tn, lambda i, j: j),
            pl.BlockSpec((tm, tn), lambda i, j: (i, g0 + j)),
            pl.BlockSpec((tm, tn), lambda i, j: (i, g0 + gstep + j)),
            pl.BlockSpec((tm, tn), lambda i, j: (i, g0 + 2 * gstep + j)),
        ],
        out_specs=pl.BlockSpec((tm, tn), lambda i, j: (i, j)),
        out_shape=jax.ShapeDtypeStruct((m, d_model), BF16),
        compiler_params=_cparams(("parallel", "arbitrary")),
        name="merge_branches",
    )(o_a, o_b, o_c, w_a, w_b, w_c, p, p, p)


FFN_K_SPLIT = 8


def _ffn_up_kernel(h_ref, wg_ref, wu_ref, cwg_ref, cwu_ref, cbg_ref, cbu_ref, o_ref, w_sc, u0_sc, u1_sc, cg_ref,
                   cu_ref, *, tiles_per_seq, n_tiles, tm, ts):
    i = pl.program_id(1)
    tn = o_ref.shape[1]
    kdim = h_ref.shape[1]
    n_sub = tm // ts
    tk = kdim // FFN_K_SPLIT
    te = ts // FFN_K_SPLIT
    rows = lax.broadcasted_iota(I32, (te, tn), 0)
    slots = (u0_sc, u1_sc)

    @pl.when(i == 0)
    def _():
        w_sc[:, 0:tn] = wg_ref[...].astype(BF16)
        w_sc[:, tn:2 * tn] = wu_ref[...].astype(BF16)

    @pl.when((i > 0) & (((i - 1) % tiles_per_seq) == 0))
    def _():
        cg_ref[...] = jnp.zeros_like(cg_ref)
        cu_ref[...] = jnp.zeros_like(cu_ref)

    def conv(u, prev, cw_ref, cb_ref):
        u1 = jnp.where(rows == 0, prev[7:8, :], pltpu.roll(u, 1, axis=0))
        u2 = pltpu.roll(u, 2, axis=0)
        u2 = jnp.where(rows == 0, prev[6:7, :], jnp.where(rows == 1, prev[7:8, :], u2))
        cw = cw_ref[...]
        return cw[0:1, :] * u2 + cw[1:2, :] * u1 + cw[2:3, :] * u + cb_ref[...]

    def epilogue_piece(p, src, prev_g, prev_u):
        u = src[p * te:(p + 1) * te, :]
        ug = u[:, 0:tn]
        uu = u[:, tn:2 * tn]
        gate = conv(ug, prev_g, cwg_ref, cbg_ref)
        up = conv(uu, prev_u, cwu_ref, cbu_ref)
        o_ref[p * te:(p + 1) * te, :] = (gate * _sigmoid(gate) * up).astype(o_ref.dtype)
        return ug[te - SUBLANES:te, :], uu[te - SUBLANES:te, :]

    def run(mm_dst, ep_src):
        if ep_src is not None:
            prev_g, prev_u = cg_ref[...], cu_ref[...]
        for sb in range(n_sub):
            acc = None
            for kc in range(FFN_K_SPLIT):
                if mm_dst is not None:
                    part = _dot(h_ref[sb * ts:(sb + 1) * ts, kc * tk:(kc + 1) * tk], w_sc[kc * tk:(kc + 1) * tk, :])
                    acc = part if acc is None else acc + part
                if ep_src is not None:
                    prev_g, prev_u = epilogue_piece(sb * FFN_K_SPLIT + kc, ep_src, prev_g, prev_u)
            if mm_dst is not None:
                mm_dst[sb * ts:(sb + 1) * ts, :] = acc
        if ep_src is not None:
            cg_ref[...] = prev_g
            cu_ref[...] = prev_u

    @pl.when(i == 0)
    def _():
        run(slots[0], None)

    for parity in range(2):
        @pl.when((i > 0) & (i < n_tiles) & (i % 2 == parity))
        def _():
            run(slots[parity], slots[1 - parity])

    @pl.when(i == n_tiles)
    def _():
        run(None, slots[(n_tiles - 1) % 2])


def ffn_up(h, w_up, layer, conv_w, conv_b, seq, tm=1024, tn=256, ts=256):
    m, k = h.shape
    d_ff = w_up.shape[2] // 2
    tm = min(tm, seq)
    ts = min(ts, tm)
    nj = d_ff // tn
    n_tiles = m // tm
    return pl.pallas_call(
        functools.partial(_ffn_up_kernel, tiles_per_seq=seq // tm, n_tiles=n_tiles, tm=tm, ts=ts),
        grid=(nj, n_tiles + 1),
        in_specs=[
            pl.BlockSpec((tm, k), lambda j, i: (jnp.minimum(i, n_tiles - 1), 0)),
            _wspec(w_up, layer, tn, lambda j, i: j),
            _wspec(w_up, layer, tn, lambda j, i: nj + j),
            pl.BlockSpec((CONV_W, tn), lambda j, i: (0, j)),
            pl.BlockSpec((CONV_W, tn), lambda j, i: (0, nj + j)),
            pl.BlockSpec((1, tn), lambda j, i: (0, j)),
            pl.BlockSpec((1, tn), lambda j, i: (0, nj + j)),
        ],
        out_specs=pl.BlockSpec((tm, tn), lambda j, i: (jnp.maximum(i - 1, 0), j)),
        out_shape=jax.ShapeDtypeStruct((m, d_ff), BF16),
        scratch_shapes=[pltpu.VMEM((k, 2 * tn), BF16), pltpu.VMEM((tm, 2 * tn), F32), pltpu.VMEM((tm, 2 * tn), F32),
                        pltpu.VMEM((SUBLANES, tn), F32), pltpu.VMEM((SUBLANES, tn), F32)],
        compiler_params=_cparams(("parallel", "arbitrary")),
        name="ffn_up",
    )(h, w_up, w_up, conv_w, conv_w, conv_b, conv_b)


A_SUB = 16


def _hgrn_kernel(q_ref, f_ref, i_ref, g_ref, lb_ref, gain_ref, o_ref, st_ref, *, n_chunks, heads):
    @pl.when(pl.program_id(2) == 0)
    def _():
        st_ref[...] = jnp.zeros_like(st_ref)

    gain = gain_ref[...]
    tri = (lax.broadcasted_iota(I32, (CHUNK, CHUNK), 0) >= lax.broadcasted_iota(I32, (CHUNK, CHUNK), 1)).astype(BF16)
    rows_half = lax.broadcasted_iota(I32, (SUBLANES, A_DK), 0)
    n_sub = CHUNK // A_SUB

    def chunk(c, carry):
        rows = pl.ds(pl.multiple_of(c * CHUNK, CHUNK), CHUNK)
        hs = range(heads)
        cols = [slice(hh * A_DK, (hh + 1) * A_DK) for hh in hs]
        iv = [i_ref[rows, cl] for cl in cols]
        iv_b = [x.astype(BF16) for x in iv]
        st = [st_ref[hh] for hh in hs]
        qf, kin, b = [], [], []
        for hh in hs:
            lb = lb_ref[:, cols[hh]]
            z = f_ref[rows, cols[hh]]
            qv = q_ref[rows, cols[hh]]
            qf.append(qv * _sigmoid(qv))
            kin.append((1.0 - lb) * _sigmoid(-z))
            b.append(_dot_exact_lhs(tri, jnp.log(lb + (1.0 - lb) * _sigmoid(z))))
        o_inter = [_dot_nt((qf[hh] * jnp.exp(b[hh])).astype(BF16), st[hh].astype(BF16)) for hh in hs]
        outs = [[] for _ in hs]
        for si in range(n_sub):
            lo = si * A_SUB
            sub = slice(lo, lo + A_SUB)
            o_i = [o_inter[hh][sub] for hh in hs]
            if si > 0:
                att = []
                for hh in hs:
                    b_ref_row = b[hh][lo - 1:lo]
                    q_s = (qf[hh][sub] * jnp.exp(b[hh][sub] - b_ref_row)).astype(BF16)
                    k_s = (kin[hh][0:lo] * jnp.exp(b_ref_row - b[hh][0:lo])).astype(BF16)
                    att.append(_dot_nt(q_s, k_s))
                o_i = [o_i[hh] + _dot(att[hh].astype(BF16), iv_b[hh][0:lo]) for hh in hs]
            half = SUBLANES
            o_h = [[o_i[hh][0:half], o_i[hh][half:A_SUB]] for hh in hs]
            for s in range(A_SUB):
                for hh in hs:
                    b_s = b[hh][lo + s:lo + s + 1]
                    k_s = kin[hh][lo + s:lo + s + 1]
                    i_s = iv[hh][lo + s:lo + s + 1]
                    for part in range(s // half, A_SUB // half):
                        rs = slice(lo + part * half, lo + (part + 1) * half)
                        e = b[hh][rs] - b_s
                        if part == s // half:
                            e = jnp.where(rows_half >= s % half, e, -jnp.inf)
                        a = jnp.sum(qf[hh][rs] * jnp.exp(e) * k_s, axis=1, keepdims=True)
                        o_h[hh][part] = o_h[hh][part] + a * i_s
            for hh in hs:
                outs[hh].append(jnp.concatenate(o_h[hh], axis=0))
        for hh in hs:
            b_last = b[hh][CHUNK - 1:CHUNK]
            k_dec = (kin[hh] * jnp.exp(b_last - b[hh])).astype(BF16)
            st_ref[hh] = st[hh] * jnp.exp(b_last) + _dot_tn(iv_b[hh], k_dec)
        for hh in hs:
            o = jnp.concatenate(outs[hh], axis=0)
            gv = g_ref[rows, cols[hh]]
            ms = jnp.mean(o * o, axis=-1, keepdims=True)
            on = o * lax.rsqrt(ms + NORM_EPS) * gain
            o_ref[rows, cols[hh]] = (on * (gv * _sigmoid(gv))).astype(o_ref.dtype)
        return carry

    lax.fori_loop(0, n_chunks, chunk, 0)


def hgrn2(p3, lb, out_gain, s_blk=256, heads=4):
    bsz, seq, _ = p3.shape
    s_blk = min(s_blk, seq)
    width = heads * A_DK
    groups = A_HEADS // heads
    c0 = COL_A // width

    def col(part):
        return pl.BlockSpec((None, s_blk, width), lambda b, h, s, part=part: (b, s, c0 + part * groups + h))

    return pl.pallas_call(
        functools.partial(_hgrn_kernel, n_chunks=s_blk // CHUNK, heads=heads),
        grid=(bsz, groups, seq // s_blk),
        in_specs=[col(0), col(1), col(2), col(3),
                  pl.BlockSpec((1, width), lambda b, h, s: (0, h)),
                  pl.BlockSpec((1, A_DV), lambda b, h, s: (0, 0))],
        out_specs=pl.BlockSpec((None, s_blk, width), lambda b, h, s: (b, s, h)),
        out_shape=jax.ShapeDtypeStruct((bsz, seq, A_W), BF16),
        scratch_shapes=[pltpu.VMEM((heads, A_DV, A_DK), F32)],
        compiler_params=_cparams(("parallel", "parallel", "arbitrary")),
        name="hgrn2",
    )(p3, p3, p3, p3, lb.reshape(1, A_HEADS * A_DK), out_gain.reshape(1, A_DV))


def _rwkv_prep_kernel(r_ref, k_ref, v_ref, l_ref, rp_ref, kp_ref, vp_ref, lp_ref,
                      mu_ref, w0_ref, w2_ref, a0_ref, a2_ref, g2_ref, kk_ref, ka_ref,
                      ro_ref, ld_ref, k2_ref, vo_ref, kko_ref, kka_ref, go_ref, *, tiles_per_seq, tm):
    first = (pl.program_id(0) % tiles_per_seq) == 0
    rows = lax.broadcasted_iota(I32, (tm, B_W), 0)

    def shifted(cur_ref, prev_ref, part):
        cur = cur_ref[...]
        prev = jnp.where(first, 0.0, prev_ref[...])[SUBLANES - 1:SUBLANES, :]
        sh = jnp.where(rows == 0, prev, pltpu.roll(cur, 1, axis=0))
        return cur + (sh - cur) * mu_ref[part:part + 1, :]

    r = shifted(r_ref, rp_ref, 0)
    k = shifted(k_ref, kp_ref, 1)
    v = shifted(v_ref, vp_ref, 2)
    lo = shifted(l_ref, lp_ref, 3)
    lo_a = lo[:, 0:LANES]
    lo_g = lo[:, LANES:3 * LANES]
    wpre = w0_ref[...] + _dot(jnp.tanh(lo_a).astype(BF16), w2_ref[...])
    y = -wpre
    softplus = jnp.maximum(y, 0.0) + jnp.log(1.0 + jnp.exp(-jnp.abs(y)))
    w_log = -softplus - 0.5
    ld_ref[...] = -jnp.exp(w_log)
    a = _sigmoid(a0_ref[...] + _dot(lo_a.astype(BF16), a2_ref[...]))
    go_ref[...] = _dot(_sigmoid(lo_g).astype(BF16), g2_ref[...]).astype(go_ref.dtype)
    kk = k * kk_ref[...]
    bd = (lax.broadcasted_iota(I32, (LANES, LANES), 0) // B_HD
          == lax.broadcasted_iota(I32, (LANES, LANES), 1) // B_HD).astype(BF16)
    sq = kk * kk
    ss = jnp.concatenate([_dot_exact_rhs(sq[:, j * LANES:(j + 1) * LANES], bd) for j in range(B_W // LANES)], axis=1)
    kk = kk / jnp.maximum(jnp.sqrt(ss), 1e-12)
    ro_ref[...] = r.astype(ro_ref.dtype)
    vo_ref[...] = v.astype(vo_ref.dtype)
    kko_ref[...] = kk.astype(kko_ref.dtype)
    kka_ref[...] = (kk * a).astype(kka_ref.dtype)
    k2_ref[...] = (k * (1.0 + (a - 1.0) * ka_ref[...])).astype(k2_ref.dtype)


def rwkv_prep(p, seq, mu4, w0, w2p, a0, a2p, g2p, k_k, k_a, tm=256):
    m = p.shape[0]
    tm = min(tm, seq)
    cb = COL_B_RKV // B_W
    pb = tm // SUBLANES

    def cur(j):
        return pl.BlockSpec((tm, B_W), lambda i, j=j: (i, cb + j))

    def prev(j):
        return pl.BlockSpec((SUBLANES, B_W), lambda i, j=j: (jnp.maximum(i * pb - 1, 0), cb + j))

    def full(a):
        return pl.BlockSpec(a.shape, lambda i: (0, 0))

    row = lambda a: a.reshape(1, B_W)
    params = [mu4, row(w0), w2p, row(a0), a2p, g2p, row(k_k), row(k_a)]
    out = jax.ShapeDtypeStruct((m, B_W), F32)
    out_b = jax.ShapeDtypeStruct((m, B_W), BF16)
    return pl.pallas_call(
        functools.partial(_rwkv_prep_kernel, tiles_per_seq=seq // tm, tm=tm),
        grid=(m // tm,),
        in_specs=[cur(0), cur(1), cur(2), cur(3), prev(0), prev(1), prev(2), prev(3)] + [full(a) for a in params],
        out_specs=[pl.BlockSpec((tm, B_W), lambda i: (i, 0))] * 7,
        out_shape=[out_b, out, out_b, out_b, out_b, out_b, out_b],
        compiler_params=_cparams(("parallel",)),
        name="rwkv_prep",
    )(p, p, p, p, p, p, p, p, *params)


B_T = 64


def _rwkv_kernel(r_ref, ld_ref, k_ref, v_ref, kk_ref, kka_ref, g_ref, rk_ref, lnw_ref, lnb_ref, o_ref, st_ref, *,
                 n_chunks, pairs):
    @pl.when(pl.program_id(2) == 0)
    def _():
        st_ref[...] = jnp.zeros_like(st_ref)

    t = B_T
    ii = lax.broadcasted_iota(I32, (t, t), 0)
    jj = lax.broadcasted_iota(I32, (t, t), 1)
    tri = (ii >= jj).astype(BF16)
    i2 = lax.broadcasted_iota(I32, (2 * t, 2 * t), 0)
    j2 = lax.broadcasted_iota(I32, (2 * t, 2 * t), 1)
    same = (i2 // t) == (j2 // t)
    strict_bd = same & ((i2 % t) > (j2 % t))
    incl_bd = same & ((i2 % t) >= (j2 % t))
    head_bd = same.astype(BF16)
    lane = lax.broadcasted_iota(I32, (t, LANES), 1)
    h0 = lane < B_HD

    def stack(x):
        return jnp.concatenate([jnp.where(h0, x, jnp.zeros_like(x)), jnp.where(h0, jnp.zeros_like(x), x)], axis=0)

    def head_sum(x):
        hi, lo = _split2(x)
        return _dot(hi, head_bd) + _dot(lo, head_bd)

    def chunk(c, carry):
        sl = pl.ds(pl.multiple_of(c * t, t), t)
        prs = range(pairs)
        cols = [slice(pp * LANES, (pp + 1) * LANES) for pp in prs]
        r = [r_ref[sl, cl].astype(F32) for cl in cols]
        ld = [ld_ref[sl, cl] for cl in cols]
        k = [k_ref[sl, cl].astype(F32) for cl in cols]
        v = [v_ref[sl, cl].astype(F32) for cl in cols]
        cs = []
        for pp in prs:
            ld_hi, ld_lo = _split2(ld[pp])
            cs.append(_dot(tri, ld_hi) + _dot(tri, ld_lo))
        lhs, rhs, vs = [], [], []
        for pp in prs:
            e_neg = jnp.exp(-cs[pp])
            kka = kka_ref[sl, cols[pp]].astype(F32)
            a_t = (-kk_ref[sl, cols[pp]].astype(F32) * jnp.exp(cs[pp] - ld[pp])).astype(BF16)
            r_t = (r[pp] * jnp.exp(cs[pp])).astype(BF16)
            b_t = (kka * e_neg).astype(BF16)
            k_t = (k[pp] * e_neg).astype(BF16)
            lhs.append(jnp.concatenate([stack(a_t), stack(r_t)], axis=0))
            rhs.append(jnp.concatenate([b_t, b_t, k_t, k_t], axis=0))
            vs.append(stack(v[pp].astype(BF16)))
        sc = [_dot_nt(lhs[pp], rhs[pp]) for pp in prs]
        st = [st_ref[pp] for pp in prs]
        proj = [_dot_nt(lhs[pp], st[pp].astype(BF16)) for pp in prs]
        n = [jnp.where(strict_bd, sc[pp][0:2 * t, 0:2 * t], 0.0).astype(BF16) for pp in prs]
        xs = [proj[pp][0:2 * t] + _dot(jnp.where(strict_bd, sc[pp][0:2 * t, 2 * t:4 * t], 0.0).astype(BF16), vs[pp])
              for pp in prs]
        for it in range(6):
            xs = [xs[pp] + _dot(n[pp], xs[pp].astype(BF16)) for pp in prs]
            if it < 5:
                n = [_dot(n[pp], n[pp]).astype(BF16) for pp in prs]
        os_ = []
        for pp in prs:
            m_r = jnp.concatenate([jnp.where(incl_bd, sc[pp][2 * t:4 * t, 0:2 * t], 0.0),
                                   jnp.where(incl_bd, sc[pp][2 * t:4 * t, 2 * t:4 * t], 0.0)], axis=1)
            uv = jnp.concatenate([xs[pp].astype(BF16), vs[pp]], axis=0)
            os_.append(proj[pp][2 * t:4 * t] + _dot(m_r.astype(BF16), uv))
        upd = []
        for pp in prs:
            u = xs[pp][0:t] + xs[pp][t:2 * t]
            c_last = cs[pp][t - 1:t]
            dec = jnp.exp(c_last - cs[pp])
            upd.append(_dot_tn(jnp.concatenate([u, v[pp]], axis=0).astype(BF16),
                               jnp.concatenate([kka_ref[sl, cols[pp]].astype(F32) * dec, k[pp] * dec], axis=0).astype(BF16)))
        for pp in prs:
            st_ref[pp] = st[pp] * jnp.exp(cs[pp][t - 1:t]) + jnp.where(same, upd[pp], 0.0)
        inv = 1.0 / B_HD
        o = [os_[pp][0:t] + os_[pp][t:2 * t] for pp in prs]
        mean = [head_sum(o[pp]) * inv for pp in prs]
        d = [o[pp] - mean[pp] for pp in prs]
        var = [head_sum(d[pp] * d[pp]) * inv for pp in prs]
        bonus = [head_sum(r[pp] * k[pp] * rk_ref[:, cols[pp]]) for pp in prs]
        for pp in prs:
            on = d[pp] * lax.rsqrt(var[pp] + B_GN_EPS) * lnw_ref[:, cols[pp]] + lnb_ref[:, cols[pp]]
            o_ref[sl, cols[pp]] = ((on + bonus[pp] * v[pp]) * g_ref[sl, cols[pp]].astype(F32)).astype(o_ref.dtype)
        return carry

    lax.fori_loop(0, n_chunks, chunk, 0)


def rwkv_recurrence(r, ld, k2, v, kk, kka, g, r_k, ln_w, ln_b, bsz, seq, s_blk=256, pairs=8):
    s_blk = min(s_blk, seq)
    width = pairs * LANES
    groups = B_W // width
    args = [a.reshape(bsz, seq, B_W) for a in (r, ld, k2, v, kk, kka, g)]
    blk = pl.BlockSpec((None, s_blk, width), lambda b, h, s: (b, s, h))
    par = pl.BlockSpec((1, width), lambda b, h, s: (0, h))
    return pl.pallas_call(
        functools.partial(_rwkv_kernel, n_chunks=s_blk // B_T, pairs=pairs),
        grid=(bsz, groups, seq // s_blk),
        in_specs=[blk] * 7 + [par] * 3,
        out_specs=blk,
        out_shape=jax.ShapeDtypeStruct((bsz, seq, B_W), BF16),
        scratch_shapes=[pltpu.VMEM((pairs, LANES, LANES), F32)],
        compiler_params=_cparams(("parallel", "parallel", "arbitrary")),
        name="rwkv_recurrence",
    )(*args, r_k.reshape(1, B_W), ln_w.reshape(1, B_W), ln_b.reshape(1, B_W))


def _rope(x, cos_t, sin_t, lane_in_head, half):
    partner = jnp.where(lane_in_head < half, pltpu.roll(x, LANES - half, axis=1), pltpu.roll(x, half, axis=1))
    return x * cos_t + partner * sin_t


def _attn_prep_kernel(q_ref, k_ref, v_ref, qi_ref, kw_ref, ca_ref, sa_ref, ci_ref, si_ref, qg_ref, kg_ref,
                      qo_ref, ko_ref, vo_ref, qio_ref, kio_ref, wo_ref, *, tm):
    ca, sa, ci, si = ca_ref[...], sa_ref[...], ci_ref[...], si_ref[...]
    lane = lax.broadcasted_iota(I32, (tm, LANES), 1)
    lane_i = lane % IDX_HD
    half_a = C_HD // ROPE_FRACTION // 2
    half_i = IDX_HD // ROPE_FRACTION // 2
    scale = C_HD ** -0.5 * LOG2_E

    def norm_rope(x, gain):
        ms = jnp.mean(x * x, axis=-1, keepdims=True)
        return _rope(x * lax.rsqrt(ms + NORM_EPS) * gain, ca, sa, lane, half_a)

    for h in range(C_HEADS):
        sl = slice(h * C_HD, (h + 1) * C_HD)
        qo_ref[:, sl] = (norm_rope(q_ref[:, sl], qg_ref[...]) * scale).astype(qo_ref.dtype)
    for h in range(C_KV_HEADS):
        sl = slice(h * C_HD, (h + 1) * C_HD)
        ko_ref[:, sl] = norm_rope(k_ref[:, sl], kg_ref[...]).astype(ko_ref.dtype)
    vo_ref[...] = v_ref[...].astype(vo_ref.dtype)
    for j in range(IDX_HEADS * IDX_HD // LANES):
        sl = slice(j * LANES, (j + 1) * LANES)
        qio_ref[:, sl] = _rope(qi_ref[:, sl], ci, si, lane_i, half_i).astype(qio_ref.dtype)
    kw = kw_ref[:, 0:LANES]
    kr = _rope(kw, ci, si, lane_i, half_i)
    kio_ref[...] = jnp.where(lane < IDX_HD, kr, pltpu.roll(kr, IDX_HD, axis=1)).astype(kio_ref.dtype)
    w = pltpu.roll(kw, LANES - IDX_HD, axis=1) * (IDX_HEADS ** -0.5 * IDX_HD ** -0.5)
    wo_ref[...] = jnp.where(lane < IDX_HEADS, w, 0.0)


def attn_prep(p, tabs, q_gain, k_gain, tm=256):
    m = p.shape[0]
    tm = min(tm, m)

    def colblk(width, off):
        return pl.BlockSpec((tm, width), lambda i: (i, off // width))

    tab = pl.BlockSpec((tm, LANES), lambda i: (i, 0))
    gain = pl.BlockSpec((1, C_HD), lambda i: (0, 0))
    kw_width = 512

    def out(width, dtype):
        return jax.ShapeDtypeStruct((m, width), dtype), pl.BlockSpec((tm, width), lambda i: (i, 0))

    outs = [out(C_W, BF16), out(C_KV_W, BF16), out(C_KV_W, BF16), out(IDX_HEADS * IDX_HD, BF16),
            out(LANES, BF16), out(LANES, F32)]
    return pl.pallas_call(
        functools.partial(_attn_prep_kernel, tm=tm),
        grid=(m // tm,),
        in_specs=[colblk(C_W, COL_C_Q), colblk(C_KV_W, COL_C_K), colblk(C_KV_W, COL_C_V),
                  colblk(IDX_HEADS * IDX_HD, COL_C_QI), colblk(kw_width, COL_C_KW), tab, tab, tab, tab, gain, gain],
        out_specs=[o[1] for o in outs],
        out_shape=[o[0] for o in outs],
        compiler_params=_cparams(("parallel",)),
        name="attn_prep",
    )(p, p, p, p, p, *tabs, q_gain.reshape(1, C_HD), k_gain.reshape(1, C_HD))


V_ONES = 16


def _attn_kernel(q_ref, k_ref, vt_ref, qi_ref, ki_ref, w_ref, o_ref, key_sc, bias_sc, m_sc, acc_sc, *, tq, tk, topk):
    qb = pl.program_id(1)
    nkb = ((qb + 1) * tq + tk - 1) // tk
    lane = lax.broadcasted_iota(I32, (tq, LANES), 1)
    w_t = w_ref[...].T
    q_chunk = lax.shift_right_logical(qb * tq + lax.broadcasted_iota(I32, (1, tq), 1), CHUNK_SHIFT)
    key_chunk_in_blk = lax.shift_right_logical(lax.broadcasted_iota(I32, (tk, tq), 0), CHUNK_SHIFT)
    n_pairs = IDX_HEADS * IDX_HD // LANES
    fold = 64

    q_pairs = []
    for hp in range(n_pairs):
        qp = qi_ref[:, hp * LANES:(hp + 1) * LANES]
        zero = jnp.zeros_like(qp)
        q_pairs.append(jnp.concatenate([jnp.where(lane < IDX_HD, qp, zero), jnp.where(lane < IDX_HD, zero, qp)], axis=0))

    def score_block(kb, carry):
        c0 = pl.multiple_of(kb * tk, tk)
        ki2 = ki_ref[pl.ds(c0, tk), :]
        sc = jnp.zeros((tk, tq), F32)
        for hp in range(n_pairs):
            rel = jnp.maximum(_dot_nt(ki2, q_pairs[hp]), 0.0)
            sc = sc + w_t[2 * hp:2 * hp + 1, :] * rel[:, 0:tq] + w_t[2 * hp + 1:2 * hp + 2, :] * rel[:, tq:2 * tq]
        sc = jnp.where(sc == 0.0, 0.0, sc)
        bits = lax.bitcast_convert_type(sc, I32)
        skey = bits ^ ((bits >> 31) & 0x7FFFFFFF)
        allowed = key_chunk_in_blk <= q_chunk - kb * (tk // CHUNK)
        key_sc[kb] = jnp.where(allowed, skey, INT_MIN)
        return carry

    lax.fori_loop(0, nkb, score_block, 0)

    def count_ge(cand):
        def body(kb, acc):
            hit = jnp.where(key_sc[kb] >= cand, 1.0, 0.0)
            for j in range(tk // fold):
                acc = acc + hit[j * fold:(j + 1) * fold]
            return acc
        acc = lax.fori_loop(0, nkb, body, jnp.zeros((fold, tq), F32))
        return jnp.sum(acc, axis=0, keepdims=True)

    kf = float(topk)
    cnt0 = count_ge(jnp.zeros((1, tq), I32))
    thr = jnp.where(cnt0 >= kf, 0, INT_MIN).astype(I32)
    cnt_thr = jnp.where(cnt0 >= kf, cnt0, kf)

    def bit_step(i, carry):
        thr, cnt_thr = carry
        cand = thr | (jnp.int32(1) << (30 - i))
        cnt = count_ge(cand)
        ok = cnt >= kf
        return jnp.where(ok, cand, thr), jnp.where(ok, cnt, cnt_thr)

    thr, cnt_thr = lax.fori_loop(0, 31, bit_step, (thr, cnt_thr))
    thr = jnp.maximum(thr, INT_MIN + 1)
    any_ties = jnp.max(cnt_thr) > kf

    def bias_plain():
        def bias_block(kb, carry):
            bias_sc[kb] = jnp.where(key_sc[kb] >= thr, 0.0, NEG_BIG)
            return carry
        lax.fori_loop(0, nkb, bias_block, 0)

    def bias_with_ties():
        need = kf - count_ge(thr + 1)
        tri = (lax.broadcasted_iota(I32, (tk, tk), 0) >= lax.broadcasted_iota(I32, (tk, tk), 1)).astype(BF16)

        def bias_block(kb, seen):
            keys = key_sc[kb]
            eq = keys == thr
            eq_f = jnp.where(eq, 1.0, 0.0)
            rank = _dot(tri, eq_f.astype(BF16)) + seen
            take = (keys > thr) | (eq & (rank <= need))
            bias_sc[kb] = jnp.where(take, 0.0, NEG_BIG)
            return rank[tk - 1:tk]

        lax.fori_loop(0, nkb, bias_block, jnp.zeros((1, tq), F32))

    lax.cond(any_ties, bias_with_ties, bias_plain)

    rep = C_HEADS // C_KV_HEADS
    ve = C_HD + V_ONES
    qs = [jnp.concatenate([q_ref[:, (g * rep + r) * C_HD:(g * rep + r + 1) * C_HD] for r in range(rep)], axis=0)
          for g in range(C_KV_HEADS)]
    m_sc[...] = jnp.full(m_sc.shape, NEG_BIG, F32)
    acc_sc[...] = jnp.zeros(acc_sc.shape, F32)

    def body(kb, carry):
        c0 = pl.multiple_of(kb * tk, tk)
        bias = bias_sc[kb]
        s_all = [_dot_nt(k_ref[pl.ds(c0, tk), g * C_HD:(g + 1) * C_HD], qs[g]) for g in range(C_KV_HEADS)]
        for g in range(C_KV_HEADS):
            s = jnp.concatenate([s_all[g][:, r * tq:(r + 1) * tq] + bias for r in range(rep)], axis=1)
            m_prev = m_sc[g]
            m_new = jnp.maximum(m_prev, jnp.max(s, axis=0, keepdims=True))
            p = jnp.exp2(s - m_new)
            acc_sc[g] = jnp.exp2(m_prev - m_new) * acc_sc[g] + _dot(vt_ref[kb, g * ve:(g + 1) * ve, :], p.astype(BF16))
            m_sc[g] = m_new
        return carry

    lax.fori_loop(0, nkb, body, 0)
    for g in range(C_KV_HEADS):
        acc = acc_sc[g]
        out_t = acc[0:C_HD] / acc[C_HD:C_HD + 1]
        for r in range(rep):
            h = g * rep + r
            o_ref[:, h * C_HD:(h + 1) * C_HD] = out_t[:, r * tq:(r + 1) * tq].T.astype(o_ref.dtype)


def sparse_attention(qn, kn, vn, qi, ki2, w, bsz, seq, tq=128, tk=512):
    tk = min(tk, seq)
    topk = min(TOPK_MAX, seq // 4)
    nkb = seq // tk
    rep = C_HEADS // C_KV_HEADS
    ve = C_HD + V_ONES
    r3 = lambda a: a.reshape(bsz, seq, a.shape[-1])
    v_t = vn.reshape(bsz, nkb, tk, C_KV_HEADS, C_HD).transpose(0, 1, 3, 4, 2)
    v_t = jnp.concatenate([v_t, jnp.ones((bsz, nkb, C_KV_HEADS, V_ONES, tk), v_t.dtype)], axis=3)
    v_t = v_t.reshape(bsz, nkb, C_KV_HEADS * ve, tk)
    qblk = lambda width: pl.BlockSpec((None, tq, width), lambda b, i: (b, i, 0))
    sblk = lambda width: pl.BlockSpec((None, seq, width), lambda b, i: (b, 0, 0))
    return pl.pallas_call(
        functools.partial(_attn_kernel, tq=tq, tk=tk, topk=topk),
        grid=(bsz, seq // tq),
        in_specs=[qblk(C_W), sblk(C_KV_W), pl.BlockSpec((None, nkb, C_KV_HEADS * ve, tk), lambda b, i: (b, 0, 0, 0)),
                  qblk(IDX_HEADS * IDX_HD), sblk(LANES), qblk(LANES)],
        out_specs=qblk(C_W),
        out_shape=jax.ShapeDtypeStruct((bsz, seq, C_W), BF16),
        scratch_shapes=[pltpu.VMEM((nkb, tk, tq), I32), pltpu.VMEM((nkb, tk, tq), F32),
                        pltpu.VMEM((C_KV_HEADS, 1, rep * tq), F32), pltpu.VMEM((C_KV_HEADS, ve, rep * tq), F32)],
        compiler_params=_cparams(("parallel", "arbitrary")),
        name="sparse_attention",
    )(r3(qn), r3(kn), v_t, r3(qi), r3(ki2), r3(w))


def _rope_tables(positions):
    pos = positions.astype(F32).reshape(-1, 1)

    def tables(head_dim):
        rot = head_dim // ROPE_FRACTION
        half = rot // 2
        inv_freq = ROPE_THETA ** (-jnp.arange(0, rot, 2, dtype=F32) / rot)
        ang = pos * inv_freq
        cos, sin = jnp.cos(ang), jnp.sin(ang)
        ones = jnp.ones((pos.shape[0], head_dim - rot), F32)
        c = jnp.concatenate([cos, cos, ones], axis=1)
        s = jnp.concatenate([-sin, sin, 0.0 * ones], axis=1)
        reps = LANES // head_dim
        return jnp.tile(c, (1, reps)), jnp.tile(s, (1, reps))

    ca, sa = tables(C_HD)
    ci, si = tables(IDX_HD)
    return ca, sa, ci, si


def _pad_rows(a, before, total):
    return jnp.zeros((total, a.shape[1]), a.dtype).at[before:before + a.shape[0]].set(a)


def kernel(x, positions, norm_mix, w_in, hgrn_lb_logits, hgrn_out_norm, rwkv_mu, rwkv_w0, rwkv_w2, rwkv_a0, rwkv_a2,
           rwkv_g2, rwkv_k_k, rwkv_k_a, rwkv_r_k, rwkv_ln_w, rwkv_ln_b, q_norm, k_norm, w_branch_a, w_branch_b,
           w_branch_c, w_out, norm_ffn, w_up, conv_w, conv_b, w_down):
    bsz, seq, d_model = x.shape
    depth = w_in.shape[0]
    m = bsz * seq
    tabs = _rope_tables(positions)
    lb_all = jnp.cumsum(jax.nn.softmax(hgrn_lb_logits.astype(F32), axis=0), axis=0)
    lb_all = lb_all - lb_all[0:1]
    n_lora = B_LORA_DECAY + B_LORA_A + B_LORA_G
    ab_cols = 4 * A_W + 3 * B_W + n_lora
    c_cols = C_W + 2 * C_KV_W + IDX_HEADS * IDX_HD + IDX_HD + IDX_HEADS
    w_in_t = jnp.swapaxes(w_in, 1, 2)
    w_in_ab = pack_cols(w_in_t, 0, ab_cols, W_AB)
    w_in_c = pack_cols(w_in_t, ab_cols, c_cols, W_C)
    w_in_g = pack_cols(w_in_t, ab_cols + c_cols, 3 * d_model, 3 * d_model)
    w_a, w_b, w_c = w_branch_a.astype(BF16), w_branch_b.astype(BF16), w_branch_c.astype(BF16)
    w_o, w_d = w_out.astype(BF16), w_down.astype(BF16)

    xf = x.reshape(m, d_model)
    for l in range(depth):
        h = rmsnorm(xf, norm_mix[l])
        p = proj_in(h, w_in_ab, l, F32)
        p_c = proj_in(h, w_in_c, l, F32)
        p_g = proj_in(h, w_in_g, l, BF16)

        o_a = hgrn2(p.reshape(bsz, seq, -1), lb_all[l], hgrn_out_norm[l]).reshape(m, A_W)

        mu = rwkv_mu[l]
        mu4 = jnp.stack([mu[0:B_W], mu[B_W:2 * B_W], mu[2 * B_W:3 * B_W],
                         jnp.pad(mu[3 * B_W:], (0, B_W - n_lora))])
        w2p = _pad_rows(rwkv_w2[l], 0, LANES).astype(BF16)
        a2p = _pad_rows(rwkv_a2[l], B_LORA_DECAY, LANES).astype(BF16)
        g2p = _pad_rows(rwkv_g2[l], 0, 2 * LANES).astype(BF16)
        parts = rwkv_prep(p, seq, mu4, rwkv_w0[l], w2p, rwkv_a0[l], a2p, g2p, rwkv_k_k[l], rwkv_k_a[l])
        o_b = rwkv_recurrence(*parts, rwkv_r_k[l], rwkv_ln_w[l], rwkv_ln_b[l], bsz, seq).reshape(m, B_W)

        qn, kn, vn, qi, ki2, wi = attn_prep(p_c, tabs, q_norm[l], k_norm[l])
        o_c = sparse_attention(qn, kn, vn, qi, ki2, wi, bsz, seq).reshape(m, C_W)

        merged = merge_branches(o_a, o_b, o_c, w_a, w_b, w_c, l, p_g, d_model)
        xf = matmul_residual(merged, w_o, l, xf, tm=1024, tn=512)

        h2 = rmsnorm(xf, norm_ffn[l])
        act = ffn_up(h2, w_up, l, conv_w[l], conv_b[l].reshape(1, -1), seq)
        xf = matmul_residual(act, w_d, l, xf, tm=512, tn=512)
    return xf.reshape(bsz, seq, d_model)
```

```python
import functools

import jax
import jax.numpy as jnp
import numpy as np
from jax import lax
from jax.experimental import pallas as pl
from jax.experimental.pallas import tpu as pltpu

F32 = jnp.float32
BF16 = jnp.bfloat16
I32 = jnp.int32

CHUNK = 64
CHUNK_SHIFT = 6
ROPE_THETA = 500000.0
ROPE_FRACTION = 4
NORM_EPS = 1e-6
A_HEADS, A_DK, A_DV = 8, 128, 128
A_W = A_HEADS * A_DV
B_HEADS, B_HD = 16, 64
B_W = B_HEADS * B_HD
B_LORA_DECAY, B_LORA_A, B_LORA_G = 64, 64, 160
B_GN_EPS = 64e-5
C_HEADS, C_KV_HEADS, C_HD = 16, 4, 128
C_W = C_HEADS * C_HD
C_KV_W = C_KV_HEADS * C_HD
IDX_HEADS, IDX_HD = 16, 64
TOPK_MAX = 256
CONV_W = 3

LANES = 128
SUBLANES = 8
VMEM_LIMIT = 56 * 1024 * 1024

COL_A = 0
COL_B_RKV = 4096
COL_B_LORA = 7168
W_AB = 8192
COL_C_Q = 0
COL_C_K = 2048
COL_C_V = 2560
COL_C_QI = 3072
COL_C_KW = 4096
W_C = 4608
LOG2_E = 1.4426950408889634
NEG_BIG = -1e30
INT_MIN = -2147483648


def _sigmoid(x):
    return 1.0 / (1.0 + jnp.exp(-x))


def _dot(a, b):
    return jnp.dot(a, b, preferred_element_type=F32)


def _dot_nt(a, b):
    return lax.dot_general(a, b, (((1,), (1,)), ((), ())), preferred_element_type=F32)


def _dot_tn(a, b):
    return lax.dot_general(a, b, (((0,), (0,)), ((), ())), preferred_element_type=F32)


def _split3(x):
    hi = x.astype(BF16)
    r1 = x - hi.astype(F32)
    mid = r1.astype(BF16)
    lo = (r1 - mid.astype(F32)).astype(BF16)
    return hi, mid, lo


def _dot_exact_lhs(m_bf16, x):
    hi, mid, lo = _split3(x)
    return _dot(m_bf16, hi) + _dot(m_bf16, mid) + _dot(m_bf16, lo)


def _dot_exact_rhs(x, m_bf16):
    hi, mid, lo = _split3(x)
    return _dot(hi, m_bf16) + _dot(mid, m_bf16) + _dot(lo, m_bf16)


def _split2(x):
    hi = x.astype(BF16)
    lo = (x - hi.astype(F32)).astype(BF16)
    return hi, lo


def _cparams(sem):
    return pltpu.CompilerParams(dimension_semantics=sem, vmem_limit_bytes=VMEM_LIMIT)


def _rmsnorm_kernel(x_ref, g_ref, o_ref):
    x = x_ref[...]
    ms = jnp.mean(x * x, axis=-1, keepdims=True)
    o_ref[...] = (x * lax.rsqrt(ms + NORM_EPS) * g_ref[...]).astype(o_ref.dtype)


def rmsnorm(x, gain, tm=256):
    m, d = x.shape
    return pl.pallas_call(
        _rmsnorm_kernel,
        grid=(m // tm,),
        in_specs=[pl.BlockSpec((tm, d), lambda i: (i, 0)), pl.BlockSpec((1, d), lambda i: (0, 0))],
        out_specs=pl.BlockSpec((tm, d), lambda i: (i, 0)),
        out_shape=jax.ShapeDtypeStruct((m, d), BF16),
        compiler_params=_cparams(("parallel",)),
        name="rmsnorm",
    )(x, gain.reshape(1, d))


def _wspec(w, layer, tn, jmap):
    return pl.BlockSpec((None, w.shape[1], tn), lambda *g: (layer, 0, jmap(*g)))


def _proj_in_kernel(h_ref, w_ref, o_ref):
    o_ref[...] = _dot(h_ref[...], w_ref[...]).astype(o_ref.dtype)


def proj_in(h, w, layer, out_dtype, tm=1024, tn=512):
    m, k = h.shape
    n = w.shape[2]
    tm = min(tm, m)
    return pl.pallas_call(
        _proj_in_kernel,
        grid=(m // tm, n // tn),
        in_specs=[pl.BlockSpec((tm, k), lambda i, j: (i, 0)), _wspec(w, layer, tn, lambda i, j: j)],
        out_specs=pl.BlockSpec((tm, tn), lambda i, j: (i, j)),
        out_shape=jax.ShapeDtypeStruct((m, n), out_dtype),
        compiler_params=_cparams(("parallel", "arbitrary")),
        name="proj_in",
    )(h, w)


def _pack_cols_kernel(*refs, shift, width):
    a_ref, o_ref = refs[0], refs[-1]
    t = pl.program_id(1)
    a = a_ref[...]
    if shift:
        a = jnp.concatenate([a[shift:], refs[1][0:shift]], axis=0)
    col = t * LANES + lax.broadcasted_iota(I32, a.shape, 0)
    o_ref[...] = jnp.where(col < width, a, 0.0).T.astype(o_ref.dtype)


def pack_cols(w_t, src_start, width, dst_width):
    depth, n_src, k = w_t.shape
    q0, shift = divmod(src_start, LANES)
    assert shift % SUBLANES == 0
    last = (n_src - 1) // LANES

    def src(extra):
        return pl.BlockSpec((None, LANES, k), lambda l, t: (l, jnp.minimum(q0 + t + extra, last), 0))

    w = w_t
    srcs = [src(0), src(1)] if shift else [src(0)]
    return pl.pallas_call(
        functools.partial(_pack_cols_kernel, shift=shift, width=width),
        grid=(depth, dst_width // LANES),
        in_specs=srcs,
        out_specs=pl.BlockSpec((None, k, LANES), lambda l, t: (l, 0, t)),
        out_shape=jax.ShapeDtypeStruct((depth, k, dst_width), BF16),
        compiler_params=_cparams(("parallel", "arbitrary")),
        name="pack_cols",
    )(*([w] * len(srcs)))


def _mm_res_kernel(a_ref, b_ref, r_ref, o_ref):
    o_ref[...] = r_ref[...] + _dot(a_ref[...], b_ref[...])


def matmul_residual(a, b, layer, res, tm=512, tn=512):
    m, k = a.shape
    n = b.shape[2]
    tm = min(tm, m)
    tn = min(tn, n)
    return pl.pallas_call(
        _mm_res_kernel,
        grid=(m // tm, n // tn),
        in_specs=[
            pl.BlockSpec((tm, k), lambda i, j: (i, 0)),
            _wspec(b, layer, tn, lambda i, j: j),
            pl.BlockSpec((tm, tn), lambda i, j: (i, j)),
        ],
        out_specs=pl.BlockSpec((tm, tn), lambda i, j: (i, j)),
        out_shape=jax.ShapeDtypeStruct((m, n), F32),
        compiler_params=_cparams(("parallel", "arbitrary")),
        name="matmul_residual",
    )(a, b, res)


MERGE_ROWS = 256


def _merge_kernel(oa_ref, ob_ref, oc_ref, wa_ref, wb_ref, wc_ref, ga_ref, gb_ref, gc_ref, o_ref):
    ts = min(MERGE_ROWS, o_ref.shape[0])
    for sb in range(o_ref.shape[0] // ts):
        rows = slice(sb * ts, (sb + 1) * ts)
        acc = _sigmoid(ga_ref[rows, :].astype(F32)) * _dot(oa_ref[rows, :], wa_ref[...])
        acc += _sigmoid(gb_ref[rows, :].astype(F32)) * _dot(ob_ref[rows, :], wb_ref[...])
        acc += _sigmoid(gc_ref[rows, :].astype(F32)) * _dot(oc_ref[rows, :], wc_ref[...])
        o_ref[rows, :] = acc.astype(o_ref.dtype)


def merge_branches(o_a, o_b, o_c, w_a, w_b, w_c, layer, p, d_model, tm=1024, tn=512):
    m = o_a.shape[0]
    tm = min(tm, m)
    g0 = 0
    gstep = d_model // tn
    return pl.pallas_call(
        _merge_kernel,
        grid=(m // tm, d_model // tn),
        in_specs=[
            pl.BlockSpec((tm, o_a.shape[1]), lambda i, j: (i, 0)),
            pl.BlockSpec((tm, o_b.shape[1]), lambda i, j: (i, 0)),
            pl.BlockSpec((tm, o_c.shape[1]), lambda i, j: (i, 0)),
            _wspec(w_a, layer, tn, lambda i, j: j),
            _wspec(w_b, layer, tn, lambda i, j: j),
            _wspec(w_c, layer, tn, lambda i, j: j),
            pl.BlockSpec((tm, tn), lambda i, j: (i, g0 + j)),
            pl.BlockSpec((tm, tn), lambda i, j: (i, g0 + gstep + j)),
            pl.BlockSpec((tm, tn), lambda i, j: (i, g0 + 2 * gstep + j)),
        ],
        out_specs=pl.BlockSpec((tm, tn), lambda i, j: (i, j)),
        out_shape=jax.ShapeDtypeStruct((m, d_model), BF16),
        compiler_params=_cparams(("parallel", "arbitrary")),
        name="merge_branches",
    )(o_a, o_b, o_c, w_a, w_b, w_c, p, p, p)


FFN_K_SPLIT = 8


def _ffn_up_kernel(h_ref, wg_ref, wu_ref, cwg_ref, cwu_ref, cbg_ref, cbu_ref, o_ref, w_sc, u0_sc, u1_sc, cg_ref,
                   cu_ref, *, tiles_per_seq, n_tiles, tm, ts):
    i = pl.program_id(1)
    tn = o_ref.shape[1]
    kdim = h_ref.shape[1]
    n_sub = tm // ts
    tk = kdim // FFN_K_SPLIT
    te = ts // FFN_K_SPLIT
    rows = lax.broadcasted_iota(I32, (te, tn), 0)
    slots = (u0_sc, u1_sc)

    @pl.when(i == 0)
    def _():
        w_sc[:, 0:tn] = wg_ref[...].astype(BF16)
        w_sc[:, tn:2 * tn] = wu_ref[...].astype(BF16)

    @pl.when((i > 0) & (((i - 1) % tiles_per_seq) == 0))
    def _():
        cg_ref[...] = jnp.zeros_like(cg_ref)
        cu_ref[...] = jnp.zeros_like(cu_ref)

    def conv(u, prev, cw_ref, cb_ref):
        u1 = jnp.where(rows == 0, prev[7:8, :], pltpu.roll(u, 1, axis=0))
        u2 = pltpu.roll(u, 2, axis=0)
        u2 = jnp.where(rows == 0, prev[6:7, :], jnp.where(rows == 1, prev[7:8, :], u2))
        cw = cw_ref[...]
        return cw[0:1, :] * u2 + cw[1:2, :] * u1 + cw[2:3, :] * u + cb_ref[...]

    def epilogue_piece(p, src, prev_g, prev_u):
        u = src[p * te:(p + 1) * te, :]
        ug = u[:, 0:tn]
        uu = u[:, tn:2 * tn]
        gate = conv(ug, prev_g, cwg_ref, cbg_ref)
        up = conv(uu, prev_u, cwu_ref, cbu_ref)
        o_ref[p * te:(p + 1) * te, :] = (gate * _sigmoid(gate) * up).astype(o_ref.dtype)
        return ug[te - SUBLANES:te, :], uu[te - SUBLANES:te, :]

    def run(mm_dst, ep_src):
        if ep_src is not None:
            prev_g, prev_u = cg_ref[...], cu_ref[...]
        for sb in range(n_sub):
            acc = None
            for kc in range(FFN_K_SPLIT):
                if mm_dst is not None:
                    part = _dot(h_ref[sb * ts:(sb + 1) * ts, kc * tk:(kc + 1) * tk], w_sc[kc * tk:(kc + 1) * tk, :])
                    acc = part if acc is None else acc + part
                if ep_src is not None:
                    prev_g, prev_u = epilogue_piece(sb * FFN_K_SPLIT + kc, ep_src, prev_g, prev_u)
            if mm_dst is not None:
                mm_dst[sb * ts:(sb + 1) * ts, :] = acc
        if ep_src is not None:
            cg_ref[...] = prev_g
            cu_ref[...] = prev_u

    @pl.when(i == 0)
    def _():
        run(slots[0], None)

    for parity in range(2):
        @pl.when((i > 0) & (i < n_tiles) & (i % 2 == parity))
        def _():
            run(slots[parity], slots[1 - parity])

    @pl.when(i == n_tiles)
    def _():
        run(None, slots[(n_tiles - 1) % 2])


def ffn_up(h, w_up, layer, conv_w, conv_b, seq, tm=1024, tn=256, ts=256):
    m, k = h.shape
    d_ff = w_up.shape[2] // 2
    tm = min(tm, seq)
    ts = min(ts, tm)
    nj = d_ff // tn
    n_tiles = m // tm
    return pl.pallas_call(
        functools.partial(_ffn_up_kernel, tiles_per_seq=seq // tm, n_tiles=n_tiles, tm=tm, ts=ts),
        grid=(nj, n_tiles + 1),
        in_specs=[
            pl.BlockSpec((tm, k), lambda j, i: (jnp.minimum(i, n_tiles - 1), 0)),
            _wspec(w_up, layer, tn, lambda j, i: j),
            _wspec(w_up, layer, tn, lambda j, i: nj + j),
            pl.BlockSpec((CONV_W, tn), lambda j, i: (0, j)),
            pl.BlockSpec((CONV_W, tn), lambda j, i: (0, nj + j)),
            pl.BlockSpec((1, tn), lambda j, i: (0, j)),
            pl.BlockSpec((1, tn), lambda j, i: (0, nj + j)),
        ],
        out_specs=pl.BlockSpec((tm, tn), lambda j, i: (jnp.maximum(i - 1, 0), j)),
        out_shape=jax.ShapeDtypeStruct((m, d_ff), BF16),
        scratch_shapes=[pltpu.VMEM((k, 2 * tn), BF16), pltpu.VMEM((tm, 2 * tn), F32), pltpu.VMEM((tm, 2 * tn), F32),
                        pltpu.VMEM((SUBLANES, tn), F32), pltpu.VMEM((SUBLANES, tn), F32)],
        compiler_params=_cparams(("parallel", "arbitrary")),
        name="ffn_up",
    )(h, w_up, w_up, conv_w, conv_w, conv_b, conv_b)


A_SUB = 16


def _hgrn_kernel(q_ref, f_ref, i_ref, g_ref, lb_ref, gain_ref, o_ref, st_ref, *, n_chunks, heads):
    @pl.when(pl.program_id(2) == 0)
    def _():
        st_ref[...] = jnp.zeros_like(st_ref)

    gain = gain_ref[...]
    tri = (lax.broadcasted_iota(I32, (CHUNK, CHUNK), 0) >= lax.broadcasted_iota(I32, (CHUNK, CHUNK), 1)).astype(BF16)
    rows_half = lax.broadcasted_iota(I32, (SUBLANES, A_DK), 0)
    n_sub = CHUNK // A_SUB

    def chunk(c, carry):
        rows = pl.ds(pl.multiple_of(c * CHUNK, CHUNK), CHUNK)
        hs = range(heads)
        cols = [slice(hh * A_DK, (hh + 1) * A_DK) for hh in hs]
        iv = [i_ref[rows, cl] for cl in cols]
        iv_b = [x.astype(BF16) for x in iv]
        st = [st_ref[hh] for hh in hs]
        qf, kin, b = [], [], []
        for hh in hs:
            lb = lb_ref[:, cols[hh]]
            z = f_ref[rows, cols[hh]]
            qv = q_ref[rows, cols[hh]]
            qf.append(qv * _sigmoid(qv))
            kin.append((1.0 - lb) * _sigmoid(-z))
            b.append(_dot_exact_lhs(tri, jnp.log(lb + (1.0 - lb) * _sigmoid(z))))
        o_inter = [_dot_nt((qf[hh] * jnp.exp(b[hh])).astype(BF16), st[hh].astype(BF16)) for hh in hs]
        outs = [[] for _ in hs]
        for si in range(n_sub):
            lo = si * A_SUB
            sub = slice(lo, lo + A_SUB)
            o_i = [o_inter[hh][sub] for hh in hs]
            if si > 0:
                att = []
                for hh in hs:
                    b_ref_row = b[hh][lo - 1:lo]
                    q_s = (qf[hh][sub] * jnp.exp(b[hh][sub] - b_ref_row)).astype(BF16)
                    k_s = (kin[hh][0:lo] * jnp.exp(b_ref_row - b[hh][0:lo])).astype(BF16)
                    att.append(_dot_nt(q_s, k_s))
                o_i = [o_i[hh] + _dot(att[hh].astype(BF16), iv_b[hh][0:lo]) for hh in hs]
            half = SUBLANES
            o_h = [[o_i[hh][0:half], o_i[hh][half:A_SUB]] for hh in hs]
            for s in range(A_SUB):
                for hh in hs:
                    b_s = b[hh][lo + s:lo + s + 1]
                    k_s = kin[hh][lo + s:lo + s + 1]
                    i_s = iv[hh][lo + s:lo + s + 1]
                    for part in range(s // half, A_SUB // half):
                        rs = slice(lo + part * half, lo + (part + 1) * half)
                        e = b[hh][rs] - b_s
                        if part == s // half:
                            e = jnp.where(rows_half >= s % half, e, -jnp.inf)
                        a = jnp.sum(qf[hh][rs] * jnp.exp(e) * k_s, axis=1, keepdims=True)
                        o_h[hh][part] = o_h[hh][part] + a * i_s
            for hh in hs:
                outs[hh].append(jnp.concatenate(o_h[hh], axis=0))
        for hh in hs:
            b_last = b[hh][CHUNK - 1:CHUNK]
            k_dec = (kin[hh] * jnp.exp(b_last - b[hh])).astype(BF16)
            st_ref[hh] = st[hh] * jnp.exp(b_last) + _dot_tn(iv_b[hh], k_dec)
        for hh in hs:
            o = jnp.concatenate(outs[hh], axis=0)
            gv = g_ref[rows, cols[hh]]
            ms = jnp.mean(o * o, axis=-1, keepdims=True)
            on = o * lax.rsqrt(ms + NORM_EPS) * gain
            o_ref[rows, cols[hh]] = (on * (gv * _sigmoid(gv))).astype(o_ref.dtype)
        return carry

    lax.fori_loop(0, n_chunks, chunk, 0)


def hgrn2(p3, lb, out_gain, s_blk=512, heads=4):
    bsz, seq, _ = p3.shape
    s_blk = min(s_blk, seq)
    width = heads * A_DK
    groups = A_HEADS // heads
    c0 = COL_A // width

    def col(part):
        return pl.BlockSpec((None, s_blk, width), lambda b, h, s, part=part: (b, s, c0 + part * groups + h))

    return pl.pallas_call(
        functools.partial(_hgrn_kernel, n_chunks=s_blk // CHUNK, heads=heads),
        grid=(bsz, groups, seq // s_blk),
        in_specs=[col(0), col(1), col(2), col(3),
                  pl.BlockSpec((1, width), lambda b, h, s: (0, h)),
                  pl.BlockSpec((1, A_DV), lambda b, h, s: (0, 0))],
        out_specs=pl.BlockSpec((None, s_blk, width), lambda b, h, s: (b, s, h)),
        out_shape=jax.ShapeDtypeStruct((bsz, seq, A_W), BF16),
        scratch_shapes=[pltpu.VMEM((heads, A_DV, A_DK), F32)],
        compiler_params=_cparams(("parallel", "parallel", "arbitrary")),
        name="hgrn2",
    )(p3, p3, p3, p3, lb.reshape(1, A_HEADS * A_DK), out_gain.reshape(1, A_DV))


def _rwkv_prep_kernel(r_ref, k_ref, v_ref, l_ref, rp_ref, kp_ref, vp_ref, lp_ref,
                      mu_ref, w0_ref, w2_ref, a0_ref, a2_ref, g2_ref, kk_ref, ka_ref,
                      ro_ref, ld_ref, k2_ref, vo_ref, kko_ref, kka_ref, go_ref, *, tiles_per_seq, tm):
    first = (pl.program_id(0) % tiles_per_seq) == 0
    rows = lax.broadcasted_iota(I32, (tm, B_W), 0)

    def shifted(cur_ref, prev_ref, part):
        cur = cur_ref[...]
        prev = jnp.where(first, 0.0, prev_ref[...])[SUBLANES - 1:SUBLANES, :]
        sh = jnp.where(rows == 0, prev, pltpu.roll(cur, 1, axis=0))
        return cur + (sh - cur) * mu_ref[part:part + 1, :]

    r = shifted(r_ref, rp_ref, 0)
    k = shifted(k_ref, kp_ref, 1)
    v = shifted(v_ref, vp_ref, 2)
    lo = shifted(l_ref, lp_ref, 3)
    lo_a = lo[:, 0:LANES]
    lo_g = lo[:, LANES:3 * LANES]
    wpre = w0_ref[...] + _dot(jnp.tanh(lo_a).astype(BF16), w2_ref[...])
    y = -wpre
    softplus = jnp.maximum(y, 0.0) + jnp.log(1.0 + jnp.exp(-jnp.abs(y)))
    w_log = -softplus - 0.5
    ld_ref[...] = -jnp.exp(w_log)
    a = _sigmoid(a0_ref[...] + _dot(lo_a.astype(BF16), a2_ref[...]))
    go_ref[...] = _dot(_sigmoid(lo_g).astype(BF16), g2_ref[...]).astype(go_ref.dtype)
    kk = k * kk_ref[...]
    bd = (lax.broadcasted_iota(I32, (LANES, LANES), 0) // B_HD
          == lax.broadcasted_iota(I32, (LANES, LANES), 1) // B_HD).astype(BF16)
    sq = kk * kk
    ss = jnp.concatenate([_dot_exact_rhs(sq[:, j * LANES:(j + 1) * LANES], bd) for j in range(B_W // LANES)], axis=1)
    kk = kk / jnp.maximum(jnp.sqrt(ss), 1e-12)
    ro_ref[...] = r.astype(ro_ref.dtype)
    vo_ref[...] = v.astype(vo_ref.dtype)
    kko_ref[...] = kk.astype(kko_ref.dtype)
    kka_ref[...] = (kk * a).astype(kka_ref.dtype)
    k2_ref[...] = (k * (1.0 + (a - 1.0) * ka_ref[...])).astype(k2_ref.dtype)


def rwkv_prep(p, seq, mu4, w0, w2p, a0, a2p, g2p, k_k, k_a, tm=256):
    m = p.shape[0]
    tm = min(tm, seq)
    cb = COL_B_RKV // B_W
    pb = tm // SUBLANES

    def cur(j):
        return pl.BlockSpec((tm, B_W), lambda i, j=j: (i, cb + j))

    def prev(j):
        return pl.BlockSpec((SUBLANES, B_W), lambda i, j=j: (jnp.maximum(i * pb - 1, 0), cb + j))

    def full(a):
        return pl.BlockSpec(a.shape, lambda i: (0, 0))

    row = lambda a: a.reshape(1, B_W)
    params = [mu4, row(w0), w2p, row(a0), a2p, g2p, row(k_k), row(k_a)]
    out = jax.ShapeDtypeStruct((m, B_W), F32)
    out_b = jax.ShapeDtypeStruct((m, B_W), BF16)
    return pl.pallas_call(
        functools.partial(_rwkv_prep_kernel, tiles_per_seq=seq // tm, tm=tm),
        grid=(m // tm,),
        in_specs=[cur(0), cur(1), cur(2), cur(3), prev(0), prev(1), prev(2), prev(3)] + [full(a) for a in params],
        out_specs=[pl.BlockSpec((tm, B_W), lambda i: (i, 0))] * 7,
        out_shape=[out_b, out, out_b, out_b, out_b, out_b, out_b],
        compiler_params=_cparams(("parallel",)),
        name="rwkv_prep",
    )(p, p, p, p, p, p, p, p, *params)


B_T = 64


def _rwkv_kernel(r_ref, ld_ref, k_ref, v_ref, kk_ref, kka_ref, g_ref, rk_ref, lnw_ref, lnb_ref, o_ref, st_ref, *,
                 n_chunks, pairs):
    @pl.when(pl.program_id(2) == 0)
    def _():
        st_ref[...] = jnp.zeros_like(st_ref)

    t = B_T
    ii = lax.broadcasted_iota(I32, (t, t), 0)
    jj = lax.broadcasted_iota(I32, (t, t), 1)
    tri = (ii >= jj).astype(BF16)
    i2 = lax.broadcasted_iota(I32, (2 * t, 2 * t), 0)
    j2 = lax.broadcasted_iota(I32, (2 * t, 2 * t), 1)
    same = (i2 // t) == (j2 // t)
    strict_bd = same & ((i2 % t) > (j2 % t))
    incl_bd = same & ((i2 % t) >= (j2 % t))
    head_bd = same.astype(BF16)
    lane = lax.broadcasted_iota(I32, (t, LANES), 1)
    h0 = lane < B_HD

    def stack(x):
        return jnp.concatenate([jnp.where(h0, x, jnp.zeros_like(x)), jnp.where(h0, jnp.zeros_like(x), x)], axis=0)

    def head_sum(x):
        hi, lo = _split2(x)
        return _dot(hi, head_bd) + _dot(lo, head_bd)

    def chunk(c, carry):
        sl = pl.ds(pl.multiple_of(c * t, t), t)
        prs = range(pairs)
        cols = [slice(pp * LANES, (pp + 1) * LANES) for pp in prs]
        r = [r_ref[sl, cl].astype(F32) for cl in cols]
        ld = [ld_ref[sl, cl] for cl in cols]
        k = [k_ref[sl, cl].astype(F32) for cl in cols]
        v = [v_ref[sl, cl].astype(F32) for cl in cols]
        cs = []
        for pp in prs:
            ld_hi, ld_lo = _split2(ld[pp])
            cs.append(_dot(tri, ld_hi) + _dot(tri, ld_lo))
        lhs, rhs, vs = [], [], []
        for pp in prs:
            e_neg = jnp.exp(-cs[pp])
            kka = kka_ref[sl, cols[pp]].astype(F32)
            a_t = (-kk_ref[sl, cols[pp]].astype(F32) * jnp.exp(cs[pp] - ld[pp])).astype(BF16)
            r_t = (r[pp] * jnp.exp(cs[pp])).astype(BF16)
            b_t = (kka * e_neg).astype(BF16)
            k_t = (k[pp] * e_neg).astype(BF16)
            lhs.append(jnp.concatenate([stack(a_t), stack(r_t)], axis=0))
            rhs.append(jnp.concatenate([b_t, b_t, k_t, k_t], axis=0))
            vs.append(stack(v[pp].astype(BF16)))
        sc = [_dot_nt(lhs[pp], rhs[pp]) for pp in prs]
        st = [st_ref[pp] for pp in prs]
        proj = [_dot_nt(lhs[pp], st[pp].astype(BF16)) for pp in prs]
        n = [jnp.where(strict_bd, sc[pp][0:2 * t, 0:2 * t], 0.0).astype(BF16) for pp in prs]
        xs = [proj[pp][0:2 * t] + _dot(jnp.where(strict_bd, sc[pp][0:2 * t, 2 * t:4 * t], 0.0).astype(BF16), vs[pp])
              for pp in prs]
        for it in range(6):
            xs = [xs[pp] + _dot(n[pp], xs[pp].astype(BF16)) for pp in prs]
            if it < 5:
                n = [_dot(n[pp], n[pp]).astype(BF16) for pp in prs]
        os_ = []
        for pp in prs:
            m_r = jnp.concatenate([jnp.where(incl_bd, sc[pp][2 * t:4 * t, 0:2 * t], 0.0),
                                   jnp.where(incl_bd, sc[pp][2 * t:4 * t, 2 * t:4 * t], 0.0)], axis=1)
            uv = jnp.concatenate([xs[pp].astype(BF16), vs[pp]], axis=0)
            os_.append(proj[pp][2 * t:4 * t] + _dot(m_r.astype(BF16), uv))
        upd = []
        for pp in prs:
            u = xs[pp][0:t] + xs[pp][t:2 * t]
            c_last = cs[pp][t - 1:t]
            dec = jnp.exp(c_last - cs[pp])
            upd.append(_dot_tn(jnp.concatenate([u, v[pp]], axis=0).astype(BF16),
                               jnp.concatenate([kka_ref[sl, cols[pp]].astype(F32) * dec, k[pp] * dec], axis=0).astype(BF16)))
        for pp in prs:
            st_ref[pp] = st[pp] * jnp.exp(cs[pp][t - 1:t]) + jnp.where(same, upd[pp], 0.0)
        inv = 1.0 / B_HD
        o = [os_[pp][0:t] + os_[pp][t:2 * t] for pp in prs]
        mean = [head_sum(o[pp]) * inv for pp in prs]
        d = [o[pp] - mean[pp] for pp in prs]
        var = [head_sum(d[pp] * d[pp]) * inv for pp in prs]
        bonus = [head_sum(r[pp] * k[pp] * rk_ref[:, cols[pp]]) for pp in prs]
        for pp in prs:
            on = d[pp] * lax.rsqrt(var[pp] + B_GN_EPS) * lnw_ref[:, cols[pp]] + lnb_ref[:, cols[pp]]
            o_ref[sl, cols[pp]] = ((on + bonus[pp] * v[pp]) * g_ref[sl, cols[pp]].astype(F32)).astype(o_ref.dtype)
        return carry

    lax.fori_loop(0, n_chunks, chunk, 0)


def rwkv_recurrence(r, ld, k2, v, kk, kka, g, r_k, ln_w, ln_b, bsz, seq, s_blk=512, pairs=8):
    s_blk = min(s_blk, seq)
    width = pairs * LANES
    groups = B_W // width
    args = [a.reshape(bsz, seq, B_W) for a in (r, ld, k2, v, kk, kka, g)]
    blk = pl.BlockSpec((None, s_blk, width), lambda b, h, s: (b, s, h))
    par = pl.BlockSpec((1, width), lambda b, h, s: (0, h))
    return pl.pallas_call(
        functools.partial(_rwkv_kernel, n_chunks=s_blk // B_T, pairs=pairs),
        grid=(bsz, groups, seq // s_blk),
        in_specs=[blk] * 7 + [par] * 3,
        out_specs=blk,
        out_shape=jax.ShapeDtypeStruct((bsz, seq, B_W), BF16),
        scratch_shapes=[pltpu.VMEM((pairs, LANES, LANES), F32)],
        compiler_params=_cparams(("parallel", "parallel", "arbitrary")),
        name="rwkv_recurrence",
    )(*args, r_k.reshape(1, B_W), ln_w.reshape(1, B_W), ln_b.reshape(1, B_W))


def _rope(x, cos_t, sin_t, lane_in_head, half):
    partner = jnp.where(lane_in_head < half, pltpu.roll(x, LANES - half, axis=1), pltpu.roll(x, half, axis=1))
    return x * cos_t + partner * sin_t


def _attn_prep_kernel(q_ref, k_ref, v_ref, qi_ref, kw_ref, ca_ref, sa_ref, ci_ref, si_ref, qg_ref, kg_ref,
                      qo_ref, ko_ref, vo_ref, qio_ref, kio_ref, wo_ref, *, tm):
    ca, sa, ci, si = ca_ref[...], sa_ref[...], ci_ref[...], si_ref[...]
    lane = lax.broadcasted_iota(I32, (tm, LANES), 1)
    lane_i = lane % IDX_HD
    half_a = C_HD // ROPE_FRACTION // 2
    half_i = IDX_HD // ROPE_FRACTION // 2
    scale = C_HD ** -0.5 * LOG2_E

    def norm_rope(x, gain):
        ms = jnp.mean(x * x, axis=-1, keepdims=True)
        return _rope(x * lax.rsqrt(ms + NORM_EPS) * gain, ca, sa, lane, half_a)

    for h in range(C_HEADS):
        sl = slice(h * C_HD, (h + 1) * C_HD)
        qo_ref[:, sl] = (norm_rope(q_ref[:, sl], qg_ref[...]) * scale).astype(qo_ref.dtype)
    for h in range(C_KV_HEADS):
        sl = slice(h * C_HD, (h + 1) * C_HD)
        ko_ref[:, sl] = norm_rope(k_ref[:, sl], kg_ref[...]).astype(ko_ref.dtype)
    vo_ref[...] = v_ref[...].astype(vo_ref.dtype)
    for j in range(IDX_HEADS * IDX_HD // LANES):
        sl = slice(j * LANES, (j + 1) * LANES)
        qio_ref[:, sl] = _rope(qi_ref[:, sl], ci, si, lane_i, half_i).astype(qio_ref.dtype)
    kw = kw_ref[:, 0:LANES]
    kr = _rope(kw, ci, si, lane_i, half_i)
    kio_ref[...] = jnp.where(lane < IDX_HD, kr, pltpu.roll(kr, IDX_HD, axis=1)).astype(kio_ref.dtype)
    w = pltpu.roll(kw, LANES - IDX_HD, axis=1) * (IDX_HEADS ** -0.5 * IDX_HD ** -0.5)
    wo_ref[...] = jnp.where(lane < IDX_HEADS, w, 0.0)


def attn_prep(p, tabs, q_gain, k_gain, tm=256):
    m = p.shape[0]
    tm = min(tm, m)

    def colblk(width, off):
        return pl.BlockSpec((tm, width), lambda i: (i, off // width))

    tab = pl.BlockSpec((tm, LANES), lambda i: (i, 0))
    gain = pl.BlockSpec((1, C_HD), lambda i: (0, 0))
    kw_width = 512

    def out(width, dtype):
        return jax.ShapeDtypeStruct((m, width), dtype), pl.BlockSpec((tm, width), lambda i: (i, 0))

    outs = [out(C_W, BF16), out(C_KV_W, BF16), out(C_KV_W, BF16), out(IDX_HEADS * IDX_HD, BF16),
            out(LANES, BF16), out(LANES, F32)]
    return pl.pallas_call(
        functools.partial(_attn_prep_kernel, tm=tm),
        grid=(m // tm,),
        in_specs=[colblk(C_W, COL_C_Q), colblk(C_KV_W, COL_C_K), colblk(C_KV_W, COL_C_V),
                  colblk(IDX_HEADS * IDX_HD, COL_C_QI), colblk(kw_width, COL_C_KW), tab, tab, tab, tab, gain, gain],
        out_specs=[o[1] for o in outs],
        out_shape=[o[0] for o in outs],
        compiler_params=_cparams(("parallel",)),
        name="attn_prep",
    )(p, p, p, p, p, *tabs, q_gain.reshape(1, C_HD), k_gain.reshape(1, C_HD))


V_ONES = 16


def _attn_kernel(q_ref, k_ref, vt_ref, qi_ref, ki_ref, w_ref, o_ref, key_sc, bias_sc, m_sc, acc_sc, *, tq, tk, topk):
    qb = pl.program_id(1)
    nkb = ((qb + 1) * tq + tk - 1) // tk
    lane = lax.broadcasted_iota(I32, (tq, LANES), 1)
    w_t = w_ref[...].T
    q_chunk = lax.shift_right_logical(qb * tq + lax.broadcasted_iota(I32, (1, tq), 1), CHUNK_SHIFT)
    key_chunk_in_blk = lax.shift_right_logical(lax.broadcasted_iota(I32, (tk, tq), 0), CHUNK_SHIFT)
    n_pairs = IDX_HEADS * IDX_HD // LANES
    fold = 64

    q_pairs = []
    for hp in range(n_pairs):
        qp = qi_ref[:, hp * LANES:(hp + 1) * LANES]
        zero = jnp.zeros_like(qp)
        q_pairs.append(jnp.concatenate([jnp.where(lane < IDX_HD, qp, zero), jnp.where(lane < IDX_HD, zero, qp)], axis=0))

    def score_block(kb, carry):
        c0 = pl.multiple_of(kb * tk, tk)
        ki2 = ki_ref[pl.ds(c0, tk), :]
        sc = jnp.zeros((tk, tq), F32)
        for hp in range(n_pairs):
            rel = jnp.maximum(_dot_nt(ki2, q_pairs[hp]), 0.0)
            sc = sc + w_t[2 * hp:2 * hp + 1, :] * rel[:, 0:tq] + w_t[2 * hp + 1:2 * hp + 2, :] * rel[:, tq:2 * tq]
        sc = jnp.where(sc == 0.0, 0.0, sc)
        bits = lax.bitcast_convert_type(sc, I32)
        skey = bits ^ ((bits >> 31) & 0x7FFFFFFF)
        allowed = key_chunk_in_blk <= q_chunk - kb * (tk // CHUNK)
        key_sc[kb] = jnp.where(allowed, skey, INT_MIN)
        return carry

    lax.fori_loop(0, nkb, score_block, 0)

    def count_ge(cand):
        def body(kb, acc):
            hit = jnp.where(key_sc[kb] >= cand, 1.0, 0.0)
            for j in range(tk // fold):
                acc = acc + hit[j * fold:(j + 1) * fold]
            return acc
        acc = lax.fori_loop(0, nkb, body, jnp.zeros((fold, tq), F32))
        return jnp.sum(acc, axis=0, keepdims=True)

    kf = float(topk)
    cnt0 = count_ge(jnp.zeros((1, tq), I32))
    thr = jnp.where(cnt0 >= kf, 0, INT_MIN).astype(I32)
    cnt_thr = jnp.where(cnt0 >= kf, cnt0, kf)

    def bit_step(i, carry):
        thr, cnt_thr = carry
        cand = thr | (jnp.int32(1) << (30 - i))
        cnt = count_ge(cand)
        ok = cnt >= kf
        return jnp.where(ok, cand, thr), jnp.where(ok, cnt, cnt_thr)

    thr, cnt_thr = lax.fori_loop(0, 31, bit_step, (thr, cnt_thr))
    thr = jnp.maximum(thr, INT_MIN + 1)
    any_ties = jnp.max(cnt_thr) > kf

    def bias_plain():
        def bias_block(kb, carry):
            bias_sc[kb] = jnp.where(key_sc[kb] >= thr, 0.0, NEG_BIG)
            return carry
        lax.fori_loop(0, nkb, bias_block, 0)

    def bias_with_ties():
        need = kf - count_ge(thr + 1)
        tri = (lax.broadcasted_iota(I32, (tk, tk), 0) >= lax.broadcasted_iota(I32, (tk, tk), 1)).astype(BF16)

        def bias_block(kb, seen):
            keys = key_sc[kb]
            eq = keys == thr
            eq_f = jnp.where(eq, 1.0, 0.0)
            rank = _dot(tri, eq_f.astype(BF16)) + seen
            take = (keys > thr) | (eq & (rank <= need))
            bias_sc[kb] = jnp.where(take, 0.0, NEG_BIG)
            return rank[tk - 1:tk]

        lax.fori_loop(0, nkb, bias_block, jnp.zeros((1, tq), F32))

    lax.cond(any_ties, bias_with_ties, bias_plain)

    rep = C_HEADS // C_KV_HEADS
    ve = C_HD + V_ONES
    qs = [jnp.concatenate([q_ref[:, (g * rep + r) * C_HD:(g * rep + r + 1) * C_HD] for r in range(rep)], axis=0)
          for g in range(C_KV_HEADS)]
    m_sc[...] = jnp.full(m_sc.shape, NEG_BIG, F32)
    acc_sc[...] = jnp.zeros(acc_sc.shape, F32)

    def body(kb, carry):
        c0 = pl.multiple_of(kb * tk, tk)
        bias = bias_sc[kb]
        s_all = [_dot_nt(k_ref[pl.ds(c0, tk), g * C_HD:(g + 1) * C_HD], qs[g]) for g in range(C_KV_HEADS)]
        for g in range(C_KV_HEADS):
            s = jnp.concatenate([s_all[g][:, r * tq:(r + 1) * tq] + bias for r in range(rep)], axis=1)
            m_prev = m_sc[g]
            m_new = jnp.maximum(m_prev, jnp.max(s, axis=0, keepdims=True))
            p = jnp.exp2(s - m_new)
            acc_sc[g] = jnp.exp2(m_prev - m_new) * acc_sc[g] + _dot(vt_ref[kb, g * ve:(g + 1) * ve, :], p.astype(BF16))
            m_sc[g] = m_new
        return carry

    lax.fori_loop(0, nkb, body, 0)
    for g in range(C_KV_HEADS):
        acc = acc_sc[g]
        out_t = acc[0:C_HD] / acc[C_HD:C_HD + 1]
        for r in range(rep):
            h = g * rep + r
            o_ref[:, h * C_HD:(h + 1) * C_HD] = out_t[:, r * tq:(r + 1) * tq].T.astype(o_ref.dtype)


def sparse_attention(qn, kn, vn, qi, ki2, w, bsz, seq, tq=128, tk=512):
    tk = min(tk, seq)
    topk = min(TOPK_MAX, seq // 4)
    nkb = seq // tk
    rep = C_HEADS // C_KV_HEADS
    ve = C_HD + V_ONES
    r3 = lambda a: a.reshape(bsz, seq, a.shape[-1])
    v_t = vn.reshape(bsz, nkb, tk, C_KV_HEADS, C_HD).transpose(0, 1, 3, 4, 2)
    v_t = jnp.concatenate([v_t, jnp.ones((bsz, nkb, C_KV_HEADS, V_ONES, tk), v_t.dtype)], axis=3)
    v_t = v_t.reshape(bsz, nkb, C_KV_HEADS * ve, tk)
    qblk = lambda width: pl.BlockSpec((None, tq, width), lambda b, i: (b, i, 0))
    sblk = lambda width: pl.BlockSpec((None, seq, width), lambda b, i: (b, 0, 0))
    return pl.pallas_call(
        functools.partial(_attn_kernel, tq=tq, tk=tk, topk=topk),
        grid=(bsz, seq // tq),
        in_specs=[qblk(C_W), sblk(C_KV_W), pl.BlockSpec((None, nkb, C_KV_HEADS * ve, tk), lambda b, i: (b, 0, 0, 0)),
                  qblk(IDX_HEADS * IDX_HD), sblk(LANES), qblk(LANES)],
        out_specs=qblk(C_W),
        out_shape=jax.ShapeDtypeStruct((bsz, seq, C_W), BF16),
        scratch_shapes=[pltpu.VMEM((nkb, tk, tq), I32), pltpu.VMEM((nkb, tk, tq), F32),
                        pltpu.VMEM((C_KV_HEADS, 1, rep * tq), F32), pltpu.VMEM((C_KV_HEADS, ve, rep * tq), F32)],
        compiler_params=_cparams(("parallel", "arbitrary")),
        name="sparse_attention",
    )(r3(qn), r3(kn), v_t, r3(qi), r3(ki2), r3(w))


def _rope_tables(positions):
    pos = positions.astype(F32).reshape(-1, 1)

    def tables(head_dim):
        rot = head_dim // ROPE_FRACTION
        half = rot // 2
        inv_freq = ROPE_THETA ** (-jnp.arange(0, rot, 2, dtype=F32) / rot)
        ang = pos * inv_freq
        cos, sin = jnp.cos(ang), jnp.sin(ang)
        ones = jnp.ones((pos.shape[0], head_dim - rot), F32)
        c = jnp.concatenate([cos, cos, ones], axis=1)
        s = jnp.concatenate([-sin, sin, 0.0 * ones], axis=1)
        reps = LANES // head_dim
        return jnp.tile(c, (1, reps)), jnp.tile(s, (1, reps))

    ca, sa = tables(C_HD)
    ci, si = tables(IDX_HD)
    return ca, sa, ci, si


def _pad_rows(a, before, total):
    return jnp.zeros((total, a.shape[1]), a.dtype).at[before:before + a.shape[0]].set(a)


def kernel(x, positions, norm_mix, w_in, hgrn_lb_logits, hgrn_out_norm, rwkv_mu, rwkv_w0, rwkv_w2, rwkv_a0, rwkv_a2,
           rwkv_g2, rwkv_k_k, rwkv_k_a, rwkv_r_k, rwkv_ln_w, rwkv_ln_b, q_norm, k_norm, w_branch_a, w_branch_b,
           w_branch_c, w_out, norm_ffn, w_up, conv_w, conv_b, w_down):
    bsz, seq, d_model = x.shape
    depth = w_in.shape[0]
    m = bsz * seq
    tabs = _rope_tables(positions)
    lb_all = jnp.cumsum(jax.nn.softmax(hgrn_lb_logits.astype(F32), axis=0), axis=0)
    lb_all = lb_all - lb_all[0:1]
    n_lora = B_LORA_DECAY + B_LORA_A + B_LORA_G
    ab_cols = 4 * A_W + 3 * B_W + n_lora
    c_cols = C_W + 2 * C_KV_W + IDX_HEADS * IDX_HD + IDX_HD + IDX_HEADS
    w_in_t = jnp.swapaxes(w_in, 1, 2)
    w_in_ab = pack_cols(w_in_t, 0, ab_cols, W_AB)
    w_in_c = pack_cols(w_in_t, ab_cols, c_cols, W_C)
    w_in_g = pack_cols(w_in_t, ab_cols + c_cols, 3 * d_model, 3 * d_model)
    w_a, w_b, w_c = w_branch_a.astype(BF16), w_branch_b.astype(BF16), w_branch_c.astype(BF16)
    w_o, w_d = w_out.astype(BF16), w_down.astype(BF16)

    xf = x.reshape(m, d_model)
    for l in range(depth):
        h = rmsnorm(xf, norm_mix[l])
        p = proj_in(h, w_in_ab, l, F32)
        p_c = proj_in(h, w_in_c, l, F32)
        p_g = proj_in(h, w_in_g, l, BF16)

        o_a = hgrn2(p.reshape(bsz, seq, -1), lb_all[l], hgrn_out_norm[l]).reshape(m, A_W)

        mu = rwkv_mu[l]
        mu4 = jnp.stack([mu[0:B_W], mu[B_W:2 * B_W], mu[2 * B_W:3 * B_W],
                         jnp.pad(mu[3 * B_W:], (0, B_W - n_lora))])
        w2p = _pad_rows(rwkv_w2[l], 0, LANES).astype(BF16)
        a2p = _pad_rows(rwkv_a2[l], B_LORA_DECAY, LANES).astype(BF16)
        g2p = _pad_rows(rwkv_g2[l], 0, 2 * LANES).astype(BF16)
        parts = rwkv_prep(p, seq, mu4, rwkv_w0[l], w2p, rwkv_a0[l], a2p, g2p, rwkv_k_k[l], rwkv_k_a[l])
        o_b = rwkv_recurrence(*parts, rwkv_r_k[l], rwkv_ln_w[l], rwkv_ln_b[l], bsz, seq).reshape(m, B_W)

        qn, kn, vn, qi, ki2, wi = attn_prep(p_c, tabs, q_norm[l], k_norm[l])
        o_c = sparse_attention(qn, kn, vn, qi, ki2, wi, bsz, seq).reshape(m, C_W)

        merged = merge_branches(o_a, o_b, o_c, w_a, w_b, w_c, l, p_g, d_model)
        xf = matmul_residual(merged, w_o, l, xf, tm=1024, tn=1024)

        h2 = rmsnorm(xf, norm_ffn[l])
        act = ffn_up(h2, w_up, l, conv_w[l], conv_b[l].reshape(1, -1), seq)
        xf = matmul_residual(act, w_d, l, xf, tm=512, tn=512)
    return xf.reshape(bsz, seq, d_model)
```
